```python
import math
import jax
import jax.numpy as jnp
from jax import lax
import numpy as np

D_MODEL = 2048
BATCH = 2
SEQ = 8192
DEPTH = 4
DEC_BATCH = 8
DEC_SEQ = 16
PAST_LEN = 2048

CHUNK = 64
HEAD_DIM = 64
SWA_Q_HEADS = 16
SWA_KV_HEADS = 4
SWA_GROUP = SWA_Q_HEADS // SWA_KV_HEADS
SWA_WIDTH = SWA_Q_HEADS * HEAD_DIM
WINDOW = 128
WINDOW_CHUNKS = WINDOW // CHUNK
REL_BUCKETS = 32
REL_MAX_DIST = 128
GMLP_GROUPS = 4
GMLP_GROUP_DIM = 128
GMLP_WIDTH = GMLP_GROUPS * GMLP_GROUP_DIM
GMLP_CHUNK = 128
GLA_HEADS = 4
GLA_DK = 64
GLA_DV = 128
GLA_WIDTH = GLA_HEADS * GLA_DV
GLA_GATE_RANK = 16
GLA_GATE_TEMP = 16.0
MIX_WIDTH = SWA_WIDTH + GMLP_WIDTH + GLA_WIDTH
D_FF = 4 * D_MODEL
PLE_DIM = 256
NORM_EPS = 1e-6
IN_SIZES = (SWA_WIDTH, SWA_KV_HEADS * HEAD_DIM, SWA_KV_HEADS * HEAD_DIM,
            GMLP_WIDTH, GMLP_WIDTH,
            GLA_HEADS * GLA_DK, GLA_HEADS * GLA_DK, GLA_WIDTH, GLA_GATE_RANK, GLA_WIDTH)
IN_WIDTH = sum(IN_SIZES)

kernel_name = 'hymba_swa_gmlp_gla_stream_step'


def rms_norm(x, g):
    xf = x.astype(jnp.float32)
    y = xf * lax.rsqrt(jnp.mean(xf * xf, axis=-1, keepdims=True) + NORM_EPS)
    return (y * g.astype(jnp.float32)).astype(x.dtype)


def layer_norm(x, g):
    xf = x.astype(jnp.float32)
    xc = xf - jnp.mean(xf, axis=-1, keepdims=True)
    y = xc * lax.rsqrt(jnp.mean(xc * xc, axis=-1, keepdims=True) + NORM_EPS)
    return (y * g.astype(jnp.float32)).astype(x.dtype)


def t5_bucket(rel):
    half = REL_BUCKETS // 2
    max_exact = half // 2
    n = -rel
    ret = jnp.where(n < 0, half, 0)
    n = jnp.abs(n)
    nf = jnp.maximum(n, 1).astype(jnp.float32)
    large = max_exact + (jnp.log(nf / max_exact) / math.log(REL_MAX_DIST / max_exact)
                         * (half - max_exact)).astype(jnp.int32)
    large = jnp.minimum(large, half - 1)
    return ret + jnp.where(n < max_exact, n, large)


def swa_mixer(aq, ak, av, past_k, past_v, rel_bias, sinks):
    B, L, _ = aq.shape
    q = aq.reshape(B, L, SWA_KV_HEADS, SWA_GROUP, HEAD_DIM)
    k = ak.reshape(B, L, SWA_KV_HEADS, HEAD_DIM)
    v = av.reshape(B, L, SWA_KV_HEADS, HEAD_DIM)
    if past_k is None:
        n_blk = L // CHUNK
        pad = ((0, 0), (WINDOW, 0), (0, 0), (0, 0))
        kp = jnp.pad(k, pad).reshape(B, n_blk + WINDOW_CHUNKS, CHUNK, SWA_KV_HEADS, HEAD_DIM)
        vp = jnp.pad(v, pad).reshape(B, n_blk + WINDOW_CHUNKS, CHUNK, SWA_KV_HEADS, HEAD_DIM)
        kb = jnp.concatenate([kp[:, j:j + n_blk] for j in range(WINDOW_CHUNKS + 1)], axis=2)
        vb = jnp.concatenate([vp[:, j:j + n_blk] for j in range(WINDOW_CHUNKS + 1)], axis=2)
        qb = q.reshape(B, n_blk, CHUNK, SWA_KV_HEADS, SWA_GROUP, HEAD_DIM)
        key_pos = (jnp.arange(n_blk)[:, None] * CHUNK - WINDOW
                   + jnp.arange(WINDOW + CHUNK)[None, :])
        valid = key_pos >= 0
        n_past = WINDOW
        new_k, new_v = k[:, L - WINDOW:], v[:, L - WINDOW:]
    else:
        n_past = past_k.shape[1]
        kb = jnp.concatenate([past_k.astype(k.dtype), k], axis=1)[:, None]
        vb = jnp.concatenate([past_v.astype(v.dtype), v], axis=1)[:, None]
        qb = q[:, None]
        valid = jnp.ones((1, n_past + L), dtype=bool)
        new_k, new_v = k, v
    lq, lk = qb.shape[2], kb.shape[2]
    rel = jnp.arange(lk)[None, :] - n_past - jnp.arange(lq)[:, None]
    bias = jnp.transpose(rel_bias[t5_bucket(rel)], (2, 0, 1)).reshape(
        SWA_KV_HEADS, SWA_GROUP, lq, lk).astype(jnp.float32)
    s = jnp.einsum('bnqhgd,bnkhd->bnhgqk', qb, kb,
                   preferred_element_type=jnp.float32) * (HEAD_DIM ** -0.5)
    s = jnp.where(valid[None, :, None, None, None, :], s + bias, -jnp.inf)
    sink = sinks.astype(jnp.float32).reshape(SWA_KV_HEADS, SWA_GROUP)[None, None, :, :, None, None]
    m = jnp.maximum(jnp.max(s, axis=-1, keepdims=True), sink)
    e = jnp.exp(s - m)
    p = e / (jnp.sum(e, axis=-1, keepdims=True) + jnp.exp(sink - m))
    o = jnp.einsum('bnhgqk,bnkhd->bnqhgd', p.astype(vb.dtype), vb)
    return o.reshape(B, L, SWA_WIDTH), new_k, new_v


def gmlp_spatial(u, v, w_s, b_s):
    B, L, _ = u.shape
    c = min(L, GMLP_CHUNK)
    n = L // c
    pos = jnp.arange(c)
    mask = (pos[:, None] // CHUNK) >= (pos[None, :] // CHUNK)
    w = jnp.where(mask, w_s[:, :c, :c], 0)
    vb = v.reshape(B, n, c, GMLP_GROUPS, GMLP_GROUP_DIM)
    sv = (jnp.einsum('gnm,bcmgd->bcngd', w.astype(v.dtype), vb)
          + b_s[:, :c].T[None, None, :, :, None])
    return u * sv.reshape(B, L, GMLP_WIDTH)


def gla_chunk_step(s, blk):
    q, k, v, la = blk
    c = q.shape[2]
    b = jnp.cumsum(la, axis=2)
    causal = jnp.tril(jnp.ones((c, c), dtype=bool))
    diff = b[:, :, :, None, :] - b[:, :, None, :, :]
    decay = jnp.exp(jnp.where(causal[None, None, :, :, None], diff, -jnp.inf))
    att = jnp.einsum('bhtd,bhsd,bhtsd->bhts', q, k, decay)
    o = (jnp.einsum('bhts,bhsv->bhtv', att, v)
         + jnp.einsum('bhtd,bhdv->bhtv', q * jnp.exp(b), s))
    b_last = b[:, :, -1:, :]
    s_new = (jnp.exp(b_last[:, :, 0, :])[..., None] * s
             + jnp.einsum('bhsd,bhsv->bhdv', k * jnp.exp(b_last - b), v))
    return s_new, o


def gla_mixer(cq, ck, cv, cg, co, s0, w_gate, b_gate, g_onorm):
    B, L, _ = cq.shape
    log_a = jax.nn.log_sigmoid((cg @ w_gate + b_gate).astype(jnp.float32)) / GLA_GATE_TEMP
    c = min(L, CHUNK)
    n = L // c

    def blocks(t, d):
        t = t.astype(jnp.float32).reshape(B, n, c, GLA_HEADS, d)
        return jnp.transpose(t, (1, 0, 3, 2, 4))

    if s0 is None:
        s0 = jnp.zeros((B, GLA_HEADS, GLA_DK, GLA_DV), jnp.float32)
    s_fin, o = lax.scan(gla_chunk_step, s0.astype(jnp.float32),
                        (blocks(cq * GLA_DK ** -0.5, GLA_DK), blocks(ck, GLA_DK),
                         blocks(cv, GLA_DV), blocks(log_a, GLA_DK)))
    o = jnp.transpose(o, (1, 0, 3, 2, 4)).reshape(B, L, GLA_HEADS, GLA_DV)
    o = rms_norm(o, g_onorm.reshape(GLA_HEADS, GLA_DV)).reshape(B, L, GLA_WIDTH)
    return o.astype(cq.dtype) * jax.nn.silu(co), s_fin


def trunk_layer(x, pe, past_k, past_v, gla_s0, w_in, w_gate, b_gate, rel_bias, sinks,
                w_spatial, b_spatial, g_vnorm, g_onorm, w_out, g_mix_pre, g_mix_post,
                g_ffn_pre, g_ffn_post, w_up, w_down, w_ple, w_ple_gate):
    h = rms_norm(x, g_mix_pre)
    z = h @ w_in
    aq, ak, av, bu, bv, cq, ck, cv, cg, co = jnp.split(
        z, np.cumsum(IN_SIZES)[:-1].tolist(), axis=-1)
    o_a, new_k, new_v = swa_mixer(aq, ak, av, past_k, past_v, rel_bias, sinks)
    v_n = layer_norm(jax.nn.gelu(bv), g_vnorm)
    o_b = gmlp_spatial(jax.nn.gelu(bu), v_n, w_spatial, b_spatial)
    o_c, s_fin = gla_mixer(cq, ck, cv, cg, co, gla_s0, w_gate, b_gate, g_onorm)
    mix = jnp.concatenate([o_a, o_b, o_c], axis=-1) @ w_out
    x = x + rms_norm(mix, g_mix_post)
    h = rms_norm(x, g_ffn_pre)
    f = jnp.square(jax.nn.relu(h @ w_up)) @ w_down
    x = x + rms_norm(f, g_ffn_post)
    x = x + (pe @ w_ple) * jax.nn.sigmoid(x @ w_ple_gate)
    return x, new_k, new_v, s_fin, v_n


def setup_inputs(seed: int = 0) -> dict:
    key = jax.random.key(seed)
    ks = jax.random.split(key, 26)

    def nrm(k, shape, scale):
        return scale * jax.random.normal(k, shape, jnp.float32)

    def gain(k, shape):
        return 1.0 + 0.05 * jax.random.normal(k, shape, jnp.float32)

    win_cache = min(WINDOW, PAST_LEN)
    return {
        'x_prompt': nrm(ks[0], (BATCH, SEQ, D_MODEL), 1.0),
        'x_sample': nrm(ks[1], (DEC_BATCH, DEC_SEQ, D_MODEL), 1.0),
        'cache_swa_k': nrm(ks[2], (DEPTH, DEC_BATCH, win_cache, SWA_KV_HEADS, HEAD_DIM), 1.0),
        'cache_swa_v': nrm(ks[3], (DEPTH, DEC_BATCH, win_cache, SWA_KV_HEADS, HEAD_DIM), 1.0),
        'state_gla': nrm(ks[4], (DEPTH, DEC_BATCH, GLA_HEADS, GLA_DK, GLA_DV), 1.0),
        'p_prompt': nrm(ks[5], (DEPTH, BATCH, SEQ, PLE_DIM), 1.0),
        'p_sample': nrm(ks[6], (DEPTH, DEC_BATCH, DEC_SEQ, PLE_DIM), 1.0),
        'w_in': nrm(ks[7], (DEPTH, D_MODEL, IN_WIDTH), D_MODEL ** -0.5),
        'w_gate': nrm(ks[8], (DEPTH, GLA_GATE_RANK, GLA_HEADS * GLA_DK), GLA_GATE_RANK ** -0.5),
        'b_gate': nrm(ks[9], (DEPTH, GLA_HEADS * GLA_DK), 0.1),
        'rel_bias': nrm(ks[10], (REL_BUCKETS, SWA_Q_HEADS), 0.5),
        'attn_sinks': nrm(ks[11], (DEPTH, SWA_Q_HEADS), 0.5),
        'w_spatial': nrm(ks[12], (DEPTH, GMLP_GROUPS, GMLP_CHUNK, GMLP_CHUNK), GMLP_CHUNK ** -0.5),
        'b_spatial': nrm(ks[13], (DEPTH, GMLP_GROUPS, GMLP_CHUNK), 0.1),
        'g_gmlp_vnorm': gain(ks[14], (DEPTH, GMLP_WIDTH)),
        'g_gla_onorm': gain(ks[15], (DEPTH, GLA_WIDTH)),
        'w_out': nrm(ks[16], (DEPTH, MIX_WIDTH, D_MODEL), MIX_WIDTH ** -0.5),
        'g_mix_pre': gain(ks[17], (DEPTH, D_MODEL)),
        'g_mix_post': gain(ks[18], (DEPTH, D_MODEL)),
        'g_ffn_pre': gain(ks[19], (DEPTH, D_MODEL)),
        'g_ffn_post': gain(ks[20], (DEPTH, D_MODEL)),
        'w_up': nrm(ks[21], (DEPTH, D_MODEL, D_FF), D_MODEL ** -0.5),
        'w_down': nrm(ks[22], (DEPTH, D_FF, D_MODEL), D_FF ** -0.5),
        'w_ple': nrm(ks[23], (DEPTH, PLE_DIM, D_MODEL), PLE_DIM ** -0.5),
        'w_ple_gate': nrm(ks[24], (DEPTH, D_MODEL, D_MODEL), D_MODEL ** -0.5),
    }


def reference(x_prompt, x_sample, cache_swa_k, cache_swa_v, state_gla, p_prompt, p_sample,
              w_in, w_gate, b_gate, rel_bias, attn_sinks, w_spatial, b_spatial,
              g_gmlp_vnorm, g_gla_onorm, w_out, g_mix_pre, g_mix_post, g_ffn_pre,
              g_ffn_post, w_up, w_down, w_ple, w_ple_gate):
    y_prompt, y_sample = x_prompt, x_sample
    kp_rows, vp_rows, ks_rows, vs_rows = [], [], [], []
    sp_states, ss_states, gv_rows = [], [], []
    for i in range(DEPTH):
        lw = dict(w_in=w_in[i], w_gate=w_gate[i], b_gate=b_gate[i], rel_bias=rel_bias,
                  sinks=attn_sinks[i], w_spatial=w_spatial[i], b_spatial=b_spatial[i],
                  g_vnorm=g_gmlp_vnorm[i], g_onorm=g_gla_onorm[i], w_out=w_out[i],
                  g_mix_pre=g_mix_pre[i], g_mix_post=g_mix_post[i], g_ffn_pre=g_ffn_pre[i],
                  g_ffn_post=g_ffn_post[i], w_up=w_up[i], w_down=w_down[i],
                  w_ple=w_ple[i], w_ple_gate=w_ple_gate[i])
        y_prompt, k_p, v_p, s_p, _ = trunk_layer(y_prompt, p_prompt[i], None, None, None, **lw)
        y_sample, k_s, v_s, s_s, gv_s = trunk_layer(y_sample, p_sample[i], cache_swa_k[i],
                                                   cache_swa_v[i], state_gla[i], **lw)
        kp_rows.append(k_p)
        vp_rows.append(v_p)
        ks_rows.append(k_s)
        vs_rows.append(v_s)
        sp_states.append(s_p)
        ss_states.append(s_s.astype(state_gla.dtype))
        gv_rows.append(gv_s)
    return (y_prompt, y_sample, jnp.stack(kp_rows), jnp.stack(vp_rows), jnp.stack(ks_rows),
            jnp.stack(vs_rows), jnp.stack(sp_states), jnp.stack(ss_states), jnp.stack(gv_rows))
```

```python
import functools
import math

import jax
import jax.numpy as jnp
from jax import lax
from jax.experimental import pallas as pl
from jax.experimental.pallas import tpu as pltpu

F32 = jnp.float32
BF16 = jnp.bfloat16

D_MODEL = 2048
DEPTH = 4
CHUNK = 64
HEAD_DIM = 64
SWA_Q_HEADS = 16
SWA_KV_HEADS = 4
SWA_WIDTH = 1024
KV_WIDTH = 256
WINDOW = 128
REL_BUCKETS = 32
REL_MAX_DIST = 128
GMLP_GROUPS = 4
GMLP_WIDTH = 512
GMLP_CHUNK = 128
GLA_HEADS = 4
GLA_DK = 64
GLA_DV = 128
GLA_WIDTH = 512
GLA_GATE_RANK = 16
GLA_GATE_TEMP = 16.0
D_FF = 4 * D_MODEL
PLE_DIM = 256
NORM_EPS = 1e-6
LANES = 128
HALF = LANES // 2

OFF_Q, OFF_K, OFF_V = 0, 1024, 1280
OFF_BU, OFF_BV = 1536, 2048
OFF_CQ, OFF_CK, OFF_CV, OFF_CO, OFF_CG = 2560, 2816, 3072, 3584, 4096
Z_WIDTH = 4224
CG_PAD = LANES

MASK_VALUE = -1e30
VMEM_LIMIT = 52 * 1024 * 1024
FFN_VMEM_LIMIT = 60 * 1024 * 1024


def _cparams(sem, vmem=VMEM_LIMIT):
    return pltpu.CompilerParams(dimension_semantics=sem, vmem_limit_bytes=vmem)


def _rms(x, g):
    return (x * lax.rsqrt(jnp.mean(x * x, axis=-1, keepdims=True) + NORM_EPS)) * g


def _inproj_kernel(x_ref, g_ref, w_ref, z_ref, h_ref):
    @pl.when(pl.program_id(1) == 0)
    def _():
        h_ref[...] = _rms(x_ref[...], g_ref[...]).astype(BF16)

    z_ref[...] = jnp.dot(h_ref[...], w_ref[...], preferred_element_type=F32)


def _inproj(x, g, w, tm, tn):
    m = x.shape[0]
    return pl.pallas_call(
        _inproj_kernel,
        grid=(m // tm, Z_WIDTH // tn),
        in_specs=[pl.BlockSpec((tm, D_MODEL), lambda i, j: (i, 0)),
                  pl.BlockSpec((1, D_MODEL), lambda i, j: (0, 0)),
                  pl.BlockSpec((D_MODEL, tn), lambda i, j: (0, j))],
        out_specs=pl.BlockSpec((tm, tn), lambda i, j: (i, j)),
        out_shape=jax.ShapeDtypeStruct((m, Z_WIDTH), F32),
        scratch_shapes=[pltpu.VMEM((tm, D_MODEL), BF16)],
        compiler_params=_cparams(("parallel", "arbitrary")),
        name="inproj",
    )(x, g, w)


def _attn_kernel(sink_ref, q_ref, kp_ref, kc_ref, vp_ref, vc_ref, bias_ref, o_ref):
    q = (q_ref[...] * (HEAD_DIM ** -0.5)).astype(BF16)
    k = jnp.concatenate([kp_ref[...], kc_ref[...]], axis=0)
    v = jnp.concatenate([vp_ref[...], vc_ref[...]], axis=0)
    lk = k.shape[0]
    lane = lax.broadcasted_iota(jnp.int32, (lk, LANES), 1)
    lo = lane < HALF
    qlane = lax.broadcasted_iota(jnp.int32, (q.shape[0], LANES), 1)
    for t in range(KV_WIDTH // LANES):
        kt = k[:, t * LANES:(t + 1) * LANES]
        vt = v[:, t * LANES:(t + 1) * LANES]
        kt_r = pltpu.roll(kt, HALF, 1)
        vt_r = pltpu.roll(vt, HALF, 1)
        for half in range(2):
            kv = 2 * t + half
            k_in_lo, k_in_hi = (kt, kt_r) if half == 0 else (kt_r, kt)
            v_in_lo, v_in_hi = (vt, vt_r) if half == 0 else (vt_r, vt)
            k_half = (jnp.where(lo, k_in_lo, 0.0).astype(BF16), jnp.where(lo, 0.0, k_in_hi).astype(BF16))
            v_half = (jnp.where(lo, v_in_lo, 0.0).astype(BF16), jnp.where(lo, 0.0, v_in_hi).astype(BF16))
            for qt in (2 * kv, 2 * kv + 1):
                q2 = q[:, qt * LANES:(qt + 1) * LANES]
                acc = None
                inv = []
                for hh in range(2):
                    h = 2 * qt + hh
                    s = lax.dot_general(q2, k_half[hh], (((1,), (1,)), ((), ())),
                                        preferred_element_type=F32)
                    s = s + bias_ref[h]
                    sink = sink_ref[h]
                    m = jnp.maximum(jnp.max(s, axis=-1, keepdims=True), sink)
                    e = jnp.exp(s - m)
                    denom = jnp.sum(e, axis=-1, keepdims=True) + jnp.exp(sink - m)
                    inv.append(1.0 / denom)
                    pv = jnp.dot(e.astype(BF16), v_half[hh], preferred_element_type=F32)
                    acc = pv if acc is None else acc + pv
                o2 = acc * jnp.where(qlane < HALF, inv[0], inv[1])
                o_ref[:, qt * LANES:(qt + 1) * LANES] = o2.astype(o_ref.dtype)


def _attn(z, kprev, vprev, bias, sinks, tq):
    b, l, _ = z.shape
    n_i = l // tq
    kcol, vcol = OFF_K // KV_WIDTH, OFF_V // KV_WIDTH
    if kprev is None:
        assert tq == WINDOW
        kprev = vprev = z
        kp_spec = pl.BlockSpec((None, WINDOW, KV_WIDTH), lambda bi, i: (bi, jnp.maximum(i - 1, 0), kcol))
        vp_spec = pl.BlockSpec((None, WINDOW, KV_WIDTH), lambda bi, i: (bi, jnp.maximum(i - 1, 0), vcol))
        bias_spec = pl.BlockSpec((None,) + bias.shape[1:], lambda bi, i: (jnp.minimum(i, 1), 0, 0, 0))
    else:
        assert n_i == 1
        kp_spec = pl.BlockSpec((None, WINDOW, KV_WIDTH), lambda bi, i: (bi, 0, 0))
        vp_spec = pl.BlockSpec((None, WINDOW, KV_WIDTH), lambda bi, i: (bi, 0, 0))
        bias_spec = pl.BlockSpec((None,) + bias.shape[1:], lambda bi, i: (0, 0, 0, 0))
    return pl.pallas_call(
        _attn_kernel,
        grid=(b, n_i),
        in_specs=[pl.BlockSpec(memory_space=pltpu.SMEM),
                  pl.BlockSpec((None, tq, SWA_WIDTH), lambda bi, i: (bi, i, 0)),
                  kp_spec,
                  pl.BlockSpec((None, tq, KV_WIDTH), lambda bi, i: (bi, i, kcol)),
                  vp_spec,
                  pl.BlockSpec((None, tq, KV_WIDTH), lambda bi, i: (bi, i, vcol)),
                  bias_spec],
        out_specs=pl.BlockSpec((None, tq, SWA_WIDTH), lambda bi, i: (bi, i, 0)),
        out_shape=jax.ShapeDtypeStruct((b, l, SWA_WIDTH), BF16),
        compiler_params=_cparams(("parallel", "arbitrary")),
        name="swa_attn",
    )(sinks, z, kprev, z, vprev, z, bias)


def _gmlp_kernel(bu_ref, bv_ref, w_ref, bs_ref, gv_ref, o_ref, *maybe_vn_ref, c, n_chunks):
    row = lax.broadcasted_iota(jnp.int32, (c, c), 0)
    col = lax.broadcasted_iota(jnp.int32, (c, c), 1)
    causal = (row // CHUNK) >= (col // CHUNK)
    v = jax.nn.gelu(bv_ref[...])
    vc = v - jnp.mean(v, axis=-1, keepdims=True)
    vn = (vc * lax.rsqrt(jnp.mean(vc * vc, axis=-1, keepdims=True) + NORM_EPS)) * gv_ref[...]
    if maybe_vn_ref:
        maybe_vn_ref[0][...] = vn
    u = jax.nn.gelu(bu_ref[...])
    vn16 = vn.astype(BF16)
    for g in range(GMLP_GROUPS):
        wg = jnp.where(causal, w_ref[g], 0.0).astype(BF16)
        bias = bs_ref[:, g:g + 1]
        cols = slice(g * LANES, (g + 1) * LANES)
        for ci in range(n_chunks):
            rows = slice(ci * c, (ci + 1) * c)
            sv = jnp.dot(wg, vn16[rows, cols], preferred_element_type=F32) + bias
            o_ref[rows, cols] = (u[rows, cols] * sv).astype(o_ref.dtype)


def _gmlp(z, w_s, b_st, g_vnorm, c, tl, want_vn):
    b, l, _ = z.shape
    ucol, vcol = OFF_BU // GMLP_WIDTH, OFF_BV // GMLP_WIDTH
    out_shape = [jax.ShapeDtypeStruct((b, l, GMLP_WIDTH), BF16)]
    out_specs = [pl.BlockSpec((None, tl, GMLP_WIDTH), lambda bi, i: (bi, i, 0))]
    if want_vn:
        out_shape.append(jax.ShapeDtypeStruct((b, l, GMLP_WIDTH), F32))
        out_specs.append(pl.BlockSpec((None, tl, GMLP_WIDTH), lambda bi, i: (bi, i, 0)))
    return pl.pallas_call(
        functools.partial(_gmlp_kernel, c=c, n_chunks=tl // c),
        grid=(b, l // tl),
        in_specs=[pl.BlockSpec((None, tl, GMLP_WIDTH), lambda bi, i: (bi, i, ucol)),
                  pl.BlockSpec((None, tl, GMLP_WIDTH), lambda bi, i: (bi, i, vcol)),
                  pl.BlockSpec((GMLP_GROUPS, c, c), lambda bi, i: (0, 0, 0)),
                  pl.BlockSpec((c, GMLP_GROUPS), lambda bi, i: (0, 0)),
                  pl.BlockSpec((1, GMLP_WIDTH), lambda bi, i: (0, 0))],
        out_specs=out_specs,
        out_shape=out_shape,
        compiler_params=_cparams(("parallel", "parallel")),
        name="gmlp",
    )(z, z, w_s, b_st, g_vnorm)


def _log_sigmoid(x):
    return jnp.minimum(x, 0.0) - jnp.log1p(jnp.exp(-jnp.abs(x)))


def _cumsum_rows(x):
    n = x.shape[0]
    row = lax.broadcasted_iota(jnp.int32, x.shape, 0)
    shift = 1
    while shift < n:
        x = x + jnp.where(row >= shift, pltpu.roll(x, shift, 0), 0.0)
        shift *= 2
    return x


def _gla_kernel(cq_ref, ck_ref, cv_ref, co_ref, cg_ref, wg_ref, bg_ref, gon_ref, s0_ref,
                o_ref, sfin_ref, st_ref, *, c, n_chunks):
    i = pl.program_id(1)

    @pl.when(i == 0)
    def _():
        for j in range(2):
            st_ref[j] = s0_ref[j].T

    lane = lax.broadcasted_iota(jnp.int32, (c, LANES), 1)
    lo = lane < HALF
    row = lax.broadcasted_iota(jnp.int32, (c, c), 0)
    col = lax.broadcasted_iota(jnp.int32, (c, c), 1)
    tril = row >= col

    def chunk(ci, carry):
        r0 = pl.multiple_of(ci * c, c)
        rows = pl.ds(r0, c)
        gate = jnp.dot(cg_ref[rows, :].astype(BF16), wg_ref[...], preferred_element_type=F32) + bg_ref[...]
        la = _log_sigmoid(gate) / GLA_GATE_TEMP
        bcum = _cumsum_rows(la)
        blast = bcum[c - 1:c, :]
        q = cq_ref[rows, :] * (GLA_DK ** -0.5)
        k = ck_ref[rows, :]
        qd = q * jnp.exp(bcum)
        kd = k * jnp.exp(-bcum)
        kl = k * jnp.exp(blast - bcum)
        dec = jnp.exp(blast)
        for j in range(2):
            tile = slice(j * LANES, (j + 1) * LANES)
            qd_t, kd_t, kl_t = qd[:, tile], kd[:, tile], kl[:, tile]
            st = st_ref[j]
            st16 = st.astype(BF16)
            st_new = st * dec[:, tile]
            for hh in range(2):
                h = 2 * j + hh
                keep = lo if hh == 0 else jnp.logical_not(lo)
                v_h = cv_ref[rows, h * GLA_DV:(h + 1) * GLA_DV].astype(BF16)
                att = lax.dot_general(qd_t.astype(BF16), jnp.where(keep, kd_t, 0.0).astype(BF16),
                                      (((1,), (1,)), ((), ())), preferred_element_type=F32)
                att = jnp.where(tril, att, 0.0)
                o_h = jnp.dot(att.astype(BF16), v_h, preferred_element_type=F32)
                o_h = o_h + lax.dot_general(jnp.where(keep, qd_t, 0.0).astype(BF16), st16,
                                            (((1,), (1,)), ((), ())), preferred_element_type=F32)
                st_new = st_new + lax.dot_general(v_h, jnp.where(keep, kl_t, 0.0).astype(BF16),
                                                  (((0,), (0,)), ((), ())), preferred_element_type=F32)
                hc = slice(h * GLA_DV, (h + 1) * GLA_DV)
                o_n = _rms(o_h, gon_ref[:, hc])
                o_ref[rows, hc] = (o_n * jax.nn.silu(co_ref[rows, hc])).astype(o_ref.dtype)
            st_ref[j] = st_new
        return carry

    lax.fori_loop(0, n_chunks, chunk, 0)

    @pl.when(i == pl.num_programs(1) - 1)
    def _():
        for j in range(2):
            sfin_ref[j] = st_ref[j].T


def _gla(z, w_gate, b_gate, g_onorm, s0, c, tl):
    b, l, _ = z.shape
    cq, ck = OFF_CQ // KV_WIDTH, OFF_CK // KV_WIDTH
    cv, co = OFF_CV // GLA_WIDTH, OFF_CO // GLA_WIDTH
    cg = OFF_CG // CG_PAD
    return pl.pallas_call(
        functools.partial(_gla_kernel, c=c, n_chunks=tl // c),
        grid=(b, l // tl),
        in_specs=[pl.BlockSpec((None, tl, KV_WIDTH), lambda bi, i: (bi, i, cq)),
                  pl.BlockSpec((None, tl, KV_WIDTH), lambda bi, i: (bi, i, ck)),
                  pl.BlockSpec((None, tl, GLA_WIDTH), lambda bi, i: (bi, i, cv)),
                  pl.BlockSpec((None, tl, GLA_WIDTH), lambda bi, i: (bi, i, co)),
                  pl.BlockSpec((None, tl, CG_PAD), lambda bi, i: (bi, i, cg)),
                  pl.BlockSpec((CG_PAD, KV_WIDTH), lambda bi, i: (0, 0)),
                  pl.BlockSpec((1, KV_WIDTH), lambda bi, i: (0, 0)),
                  pl.BlockSpec((1, GLA_WIDTH), lambda bi, i: (0, 0)),
                  pl.BlockSpec((None, 2, LANES, GLA_DV), lambda bi, i: (bi, 0, 0, 0))],
        out_specs=[pl.BlockSpec((None, tl, GLA_WIDTH), lambda bi, i: (bi, i, 0)),
                   pl.BlockSpec((None, 2, LANES, GLA_DV), lambda bi, i: (bi, 0, 0, 0))],
        out_shape=[jax.ShapeDtypeStruct((b, l, GLA_WIDTH), BF16),
                   jax.ShapeDtypeStruct((b, 2, LANES, GLA_DV), F32)],
        scratch_shapes=[pltpu.VMEM((2, GLA_DV, LANES), F32)],
        compiler_params=_cparams(("parallel", "arbitrary")),
        name="gla",
    )(z, z, z, z, z, w_gate, b_gate, g_onorm, s0)


def _outproj_kernel(x_ref, oa_ref, ob_ref, oc_ref, w_ref, g_ref, y_ref):
    mix = jnp.dot(oa_ref[...], w_ref[0:SWA_WIDTH, :], preferred_element_type=F32)
    mix = mix + jnp.dot(ob_ref[...], w_ref[SWA_WIDTH:SWA_WIDTH + GMLP_WIDTH, :], preferred_element_type=F32)
    mix = mix + jnp.dot(oc_ref[...], w_ref[SWA_WIDTH + GMLP_WIDTH:, :], preferred_element_type=F32)
    y_ref[...] = x_ref[...] + _rms(mix, g_ref[...])


def _outproj(x, oa, ob, oc, w, g, tm):
    m = x.shape[0]
    row = lambda width: pl.BlockSpec((tm, width), lambda i: (i, 0))
    return pl.pallas_call(
        _outproj_kernel,
        grid=(m // tm,),
        in_specs=[row(D_MODEL), row(SWA_WIDTH), row(GMLP_WIDTH), row(GLA_WIDTH),
                  pl.BlockSpec((D_MODEL, D_MODEL), lambda i: (0, 0)),
                  pl.BlockSpec((1, D_MODEL), lambda i: (0, 0))],
        out_specs=row(D_MODEL),
        out_shape=jax.ShapeDtypeStruct((m, D_MODEL), F32),
        compiler_params=_cparams(("parallel",)),
        name="outproj",
    )(x, oa, ob, oc, w, g)


def _ffn_kernel(x_ref, gpre_ref, wup_ref, wdn_ref, gpost_ref, y_ref, h_ref):
    k = pl.program_id(1)

    @pl.when(k == 0)
    def _():
        h_ref[...] = _rms(x_ref[...], gpre_ref[...]).astype(BF16)
        y_ref[...] = jnp.zeros_like(y_ref)

    u = jnp.maximum(jnp.dot(h_ref[...], wup_ref[...], preferred_element_type=F32), 0.0)
    y_ref[...] += jnp.dot((u * u).astype(BF16), wdn_ref[...], preferred_element_type=F32)

    @pl.when(k == pl.num_programs(1) - 1)
    def _():
        y_ref[...] = x_ref[...] + _rms(y_ref[...], gpost_ref[...])


def _ffn(x, gpre, wup, wdn, gpost, tm, tk):
    m = x.shape[0]
    return pl.pallas_call(
        _ffn_kernel,
        grid=(m // tm, D_FF // tk),
        in_specs=[pl.BlockSpec((tm, D_MODEL), lambda i, k: (i, 0)),
                  pl.BlockSpec((1, D_MODEL), lambda i, k: (0, 0)),
                  pl.BlockSpec((D_MODEL, tk), lambda i, k: (0, k)),
                  pl.BlockSpec((tk, D_MODEL), lambda i, k: (k, 0)),
                  pl.BlockSpec((1, D_MODEL), lambda i, k: (0, 0))],
        out_specs=pl.BlockSpec((tm, D_MODEL), lambda i, k: (i, 0)),
        out_shape=jax.ShapeDtypeStruct((m, D_MODEL), F32),
        scratch_shapes=[pltpu.VMEM((tm, D_MODEL), BF16)],
        compiler_params=_cparams(("parallel", "arbitrary"), FFN_VMEM_LIMIT),
        name="ffn",
    )(x, gpre, wup, wdn, gpost)


def _ple_kernel(x_ref, pe_ref, wp_ref, wg_ref, y_ref):
    x = x_ref[...]
    gate = jax.nn.sigmoid(jnp.dot(x.astype(BF16), wg_ref[...], preferred_element_type=F32))
    y_ref[...] = x + jnp.dot(pe_ref[...].astype(BF16), wp_ref[...], preferred_element_type=F32) * gate


def _ple(x, pe, wp, wg, tm):
    m = x.shape[0]
    return pl.pallas_call(
        _ple_kernel,
        grid=(m // tm,),
        in_specs=[pl.BlockSpec((tm, D_MODEL), lambda i: (i, 0)),
                  pl.BlockSpec((tm, PLE_DIM), lambda i: (i, 0)),
                  pl.BlockSpec((PLE_DIM, D_MODEL), lambda i: (0, 0)),
                  pl.BlockSpec((D_MODEL, D_MODEL), lambda i: (0, 0))],
        out_specs=pl.BlockSpec((tm, D_MODEL), lambda i: (i, 0)),
        out_shape=jax.ShapeDtypeStruct((m, D_MODEL), F32),
        compiler_params=_cparams(("parallel",)),
        name="ple",
    )(x, pe, wp, wg)


def _t5_bucket(rel):
    half = REL_BUCKETS // 2
    max_exact = half // 2
    n = -rel
    ret = jnp.where(n < 0, half, 0)
    n = jnp.abs(n)
    nf = jnp.maximum(n, 1).astype(jnp.float32)
    large = max_exact + (jnp.log(nf / max_exact) / math.log(REL_MAX_DIST / max_exact)
                         * (half - max_exact)).astype(jnp.int32)
    large = jnp.minimum(large, half - 1)
    return ret + jnp.where(n < max_exact, n, large)


def _bias_tables(rel_bias, dec_seq):
    qpos = jnp.arange(WINDOW)[:, None]
    kpos = jnp.arange(-WINDOW, WINDOW)[None, :]
    bias = jnp.transpose(rel_bias[_t5_bucket(kpos - qpos)], (2, 0, 1)).astype(F32)
    kchunk = jnp.floor_divide(kpos, CHUNK)
    qchunk = qpos // CHUNK
    valid = (kchunk >= qchunk - WINDOW // CHUNK) & (kchunk <= qchunk)
    bias_p = jnp.stack([jnp.where(valid & (kpos >= 0), bias, MASK_VALUE),
                        jnp.where(valid, bias, MASK_VALUE)])
    qpos = jnp.arange(dec_seq)[:, None]
    kpos = jnp.arange(-WINDOW, dec_seq)[None, :]
    bias_s = jnp.transpose(rel_bias[_t5_bucket(kpos - qpos)], (2, 0, 1)).astype(F32)[None]
    return bias_p, bias_s


def _prep_w_in(w):
    real = OFF_CO + GLA_GATE_RANK
    return jnp.concatenate([w[:, :OFF_CO], w[:, real:], w[:, OFF_CO:real],
                            jnp.zeros((D_MODEL, CG_PAD - GLA_GATE_RANK), w.dtype)], axis=1).astype(BF16)


def _layer(x, pe, z_shape, lw, attn_fn, gmlp_fn, gla_fn, tm, tn, tk):
    b, l = z_shape
    z = _inproj(x, lw["g_mix_pre"], lw["w_in"], tm, tn).reshape(b, l, Z_WIDTH)
    oa = attn_fn(z)
    gm = gmlp_fn(z)
    oc, s_fin = gla_fn(z)
    ob = gm[0]
    m = b * l
    x = _outproj(x, oa.reshape(m, -1), ob.reshape(m, -1), oc.reshape(m, -1), lw["w_out"], lw["g_mix_post"], min(tm, 512))
    x = _ffn(x, lw["g_ffn_pre"], lw["w_up"], lw["w_down"], lw["g_ffn_post"], tm, tk)
    x = _ple(x, pe, lw["w_ple"], lw["w_ple_gate"], min(tm, 512))
    return x, z, s_fin, gm


def kernel(x_prompt, x_sample, cache_swa_k, cache_swa_v, state_gla, p_prompt, p_sample, w_in, w_gate, b_gate, rel_bias, attn_sinks, w_spatial, b_spatial, g_gmlp_vnorm, g_gla_onorm, w_out, g_mix_pre, g_mix_post, g_ffn_pre, g_ffn_post, w_up, w_down, w_ple, w_ple_gate):
    bp, lp, _ = x_prompt.shape
    bs, ls, _ = x_sample.shape
    mp, ms = bp * lp, bs * ls
    bias_p, bias_s = _bias_tables(rel_bias, ls)
    xp = x_prompt.reshape(mp, D_MODEL)
    xs = x_sample.reshape(ms, D_MODEL)
    zeros_state = jnp.zeros((bp, 2, LANES, GLA_DV), F32)
    row = lambda a: a.reshape(1, -1)

    kp_rows, vp_rows, ks_rows, vs_rows, sp_states, ss_states, gv_rows = [], [], [], [], [], [], []
    for i in range(DEPTH):
        wg_pad = jnp.concatenate([w_gate[i], jnp.zeros((CG_PAD - GLA_GATE_RANK, KV_WIDTH), F32)], axis=0).astype(BF16)
        lw = dict(w_in=_prep_w_in(w_in[i]), g_mix_pre=row(g_mix_pre[i]), w_out=w_out[i].astype(BF16),
                  g_mix_post=row(g_mix_post[i]), g_ffn_pre=row(g_ffn_pre[i]), g_ffn_post=row(g_ffn_post[i]),
                  w_up=w_up[i].astype(BF16), w_down=w_down[i].astype(BF16),
                  w_ple=w_ple[i].astype(BF16), w_ple_gate=w_ple_gate[i].astype(BF16))
        sinks = attn_sinks[i]
        bg, gon, gvn = row(b_gate[i]), row(g_gla_onorm[i]), row(g_gmlp_vnorm[i])

        xp, zp, s_p, _ = _layer(
            xp, p_prompt[i].reshape(mp, PLE_DIM), (bp, lp), lw,
            lambda z: _attn(z, None, None, bias_p, sinks, WINDOW),
            lambda z: _gmlp(z, w_spatial[i], b_spatial[i].T, gvn, GMLP_CHUNK, 512, False),
            lambda z: _gla(z, wg_pad, bg, gon, zeros_state, CHUNK, 512),
            1024, 1408, 512)
        kp_rows.append(zp[:, lp - WINDOW:, OFF_K:OFF_K + KV_WIDTH].reshape(bp, WINDOW, SWA_KV_HEADS, HEAD_DIM))
        vp_rows.append(zp[:, lp - WINDOW:, OFF_V:OFF_V + KV_WIDTH].reshape(bp, WINDOW, SWA_KV_HEADS, HEAD_DIM))
        sp_states.append(s_p.reshape(bp, GLA_HEADS, GLA_DK, GLA_DV))

        ck = cache_swa_k[i].reshape(bs, WINDOW, KV_WIDTH)
        cv = cache_swa_v[i].reshape(bs, WINDOW, KV_WIDTH)
        s0 = state_gla[i].reshape(bs, 2, LANES, GLA_DV)
        xs, zs, s_s, gm_s = _layer(
            xs, p_sample[i].reshape(ms, PLE_DIM), (bs, ls), lw,
            lambda z: _attn(z, ck, cv, bias_s, sinks, ls),
            lambda z: _gmlp(z, w_spatial[i][:, :ls, :ls], b_spatial[i][:, :ls].T, gvn, ls, ls, True),
            lambda z: _gla(z, wg_pad, bg, gon, s0, ls, ls),
            ms, 1408, 512)
        ks_rows.append(zs[:, :, OFF_K:OFF_K + KV_WIDTH].reshape(bs, ls, SWA_KV_HEADS, HEAD_DIM))
        vs_rows.append(zs[:, :, OFF_V:OFF_V + KV_WIDTH].reshape(bs, ls, SWA_KV_HEADS, HEAD_DIM))
        ss_states.append(s_s.reshape(bs, GLA_HEADS, GLA_DK, GLA_DV).astype(state_gla.dtype))
        gv_rows.append(gm_s[1])

    return (xp.reshape(bp, lp, D_MODEL), xs.reshape(bs, ls, D_MODEL),
            jnp.stack(kp_rows), jnp.stack(vp_rows), jnp.stack(ks_rows), jnp.stack(vs_rows),
            jnp.stack(sp_states), jnp.stack(ss_states), jnp.stack(gv_rows))
```

```python
import functools
import math

import jax
import jax.numpy as jnp
from jax import lax
from jax.experimental import pallas as pl
from jax.experimental.pallas import tpu as pltpu

F32 = jnp.float32
BF16 = jnp.bfloat16

D_MODEL = 2048
DEPTH = 4
CHUNK = 64
HEAD_DIM = 64
SWA_Q_HEADS = 16
SWA_KV_HEADS = 4
SWA_GROUP = SWA_Q_HEADS // SWA_KV_HEADS
SWA_WIDTH = 1024
KV_WIDTH = 256
WINDOW = 128
REL_BUCKETS = 32
REL_MAX_DIST = 128
GMLP_GROUPS = 4
GMLP_WIDTH = 512
GMLP_CHUNK = 128
GLA_HEADS = 4
GLA_DK = 64
GLA_DV = 128
GLA_WIDTH = 512
GLA_GATE_RANK = 16
GLA_GATE_TEMP = 16.0
D_FF = 4 * D_MODEL
PLE_DIM = 256
NORM_EPS = 1e-6
LANES = 128
HALF = LANES // 2
BF16_ROWS = 16

OFF_Q, OFF_K, OFF_V = 0, 1024, 1280
OFF_BU, OFF_BV = 1536, 2048
OFF_CQ, OFF_CK, OFF_CV, OFF_CO, OFF_CG = 2560, 2816, 3072, 3584, 4096
Z_WIDTH = 4224
CG_PAD = LANES

LOG2E = math.log2(math.e)
MASK_VALUE = -1e30
VMEM_LIMIT = 52 * 1024 * 1024
FFN_VMEM_LIMIT = 60 * 1024 * 1024

ATTN_ROWS = 512
GLA_ROWS = 512
GLA_SUPER = 2 * CHUNK


def _cparams(sem, vmem=VMEM_LIMIT):
    return pltpu.CompilerParams(dimension_semantics=sem, vmem_limit_bytes=vmem)


def _rms(x, g):
    return (x * lax.rsqrt(jnp.mean(x * x, axis=-1, keepdims=True) + NORM_EPS)) * g


def _nt(a, b):
    return lax.dot_general(a, b, (((1,), (1,)), ((), ())), preferred_element_type=F32)


def _tn(a, b):
    return lax.dot_general(a, b, (((0,), (0,)), ((), ())), preferred_element_type=F32)


def _inproj_kernel(x_ref, g_ref, w_ref, z_ref, h_ref):
    @pl.when(pl.program_id(1) == 0)
    def _():
        h_ref[...] = _rms(x_ref[...], g_ref[...]).astype(BF16)

    z_ref[...] = jnp.dot(h_ref[...], w_ref[...], preferred_element_type=F32)


def _inproj(x, g, w, li, tm, tn):
    m = x.shape[0]
    return pl.pallas_call(
        _inproj_kernel,
        grid=(m // tm, Z_WIDTH // tn),
        in_specs=[pl.BlockSpec((tm, D_MODEL), lambda i, j: (i, 0)),
                  pl.BlockSpec((None, 1, D_MODEL), lambda i, j: (li, 0, 0)),
                  pl.BlockSpec((None, D_MODEL, tn), lambda i, j: (li, 0, j))],
        out_specs=pl.BlockSpec((tm, tn), lambda i, j: (i, j)),
        out_shape=jax.ShapeDtypeStruct((m, Z_WIDTH), F32),
        scratch_shapes=[pltpu.VMEM((tm, D_MODEL), BF16)],
        compiler_params=_cparams(("parallel", "arbitrary")),
        name="inproj",
    )(x, g, w)


def _attn_prompt_kernel(q_ref, kp_ref, kc_ref, vp_ref, vc_ref, bias_ref, sink_ref, o_ref, *, nsub):
    first = pl.program_id(1) == 0
    kfull = jnp.concatenate([kp_ref[...], kc_ref[...]], axis=0)
    vfull = jnp.concatenate([vp_ref[...], vc_ref[...]], axis=0)
    rows = kfull.shape[0]
    lo_k = lax.broadcasted_iota(jnp.int32, (rows, LANES), 1) < HALF
    lo_q = lax.broadcasted_iota(jnp.int32, (WINDOW, LANES), 1) < HALF
    kdup, v_t = [], []
    for t in range(KV_WIDTH // LANES):
        kt = kfull[:, t * LANES:(t + 1) * LANES]
        kt_r = pltpu.roll(kt, HALF, 1)
        kdup.append(jnp.where(lo_k, kt, kt_r).astype(BF16))
        kdup.append(jnp.where(lo_k, kt_r, kt).astype(BF16))
        vt_t = vfull[:, t * LANES:(t + 1) * LANES].T
        v_t.append(vt_t[:HALF].astype(BF16))
        v_t.append(vt_t[HALF:].astype(BF16))
    ones = jnp.ones((BF16_ROWS, 2 * WINDOW), BF16)

    def scores(j, kv):
        qa = q_ref[j * WINDOW:(j + 1) * WINDOW, (2 * kv) * LANES:(2 * kv + 1) * LANES] * (HEAD_DIM ** -0.5 * LOG2E)
        qb = q_ref[j * WINDOW:(j + 1) * WINDOW, (2 * kv + 1) * LANES:(2 * kv + 2) * LANES] * (HEAD_DIM ** -0.5 * LOG2E)
        qm = jnp.concatenate([jnp.where(lo_q, qa, 0.0), jnp.where(lo_q, 0.0, qa),
                              jnp.where(lo_q, qb, 0.0), jnp.where(lo_q, 0.0, qb)], axis=0).astype(BF16)
        return _nt(kdup[kv][j * WINDOW:(j + 2) * WINDOW], qm)

    steps = [(j, kv) for j in range(nsub) for kv in range(SWA_KV_HEADS)]
    s_next = scores(*steps[0])
    for n, (j, kv) in enumerate(steps):
        s_t = s_next
        if n + 1 < len(steps):
            s_next = scores(*steps[n + 1])
        keys = slice(j * WINDOW, (j + 2) * WINDOW)
        var = jnp.where(first, 0, 1) if j == 0 else 1
        sink = sink_ref[kv]
        e_cols, m_cols = [], []
        for hq in range(SWA_GROUP):
            cols = slice(hq * LANES, (hq + 1) * LANES)
            s_h = s_t[:, cols] + bias_ref[var, kv, :, cols]
            m_h = jnp.maximum(jnp.max(s_h, axis=0, keepdims=True), sink[:, cols])
            e_cols.append(jnp.exp2(s_h - m_h).astype(BF16))
            m_cols.append(m_h)
        e = jnp.concatenate(e_cols, axis=1)
        m = jnp.concatenate(m_cols, axis=1)
        lhs = jnp.concatenate([v_t[kv][:, keys], ones], axis=0)
        r = jnp.dot(lhs, e, preferred_element_type=F32)
        denom = r[HALF:HALF + 1] + jnp.exp2(sink - m)
        o_t = r[:HALF] * (1.0 / denom)
        for qt in range(2):
            pair = jnp.concatenate([o_t[:, (2 * qt) * LANES:(2 * qt + 1) * LANES],
                                    o_t[:, (2 * qt + 1) * LANES:(2 * qt + 2) * LANES]], axis=0)
            col = (2 * kv + qt) * LANES
            o_ref[j * WINDOW:(j + 1) * WINDOW, col:col + LANES] = pair.T.astype(o_ref.dtype)


def _attn_prompt(z, bias_t, sink_t, tl):
    b, l, _ = z.shape
    kcol, vcol = OFF_K // KV_WIDTH, OFF_V // KV_WIDTH
    per = tl // WINDOW
    prev = lambda col: pl.BlockSpec((None, WINDOW, KV_WIDTH),
                                    lambda bi, i: (bi, jnp.maximum(i * per - 1, 0), col))
    cur = lambda col: pl.BlockSpec((None, tl, KV_WIDTH), lambda bi, i: (bi, i, col))
    return pl.pallas_call(
        functools.partial(_attn_prompt_kernel, nsub=per),
        grid=(b, l // tl),
        in_specs=[pl.BlockSpec((None, tl, SWA_WIDTH), lambda bi, i: (bi, i, 0)),
                  prev(kcol), cur(kcol), prev(vcol), cur(vcol),
                  pl.BlockSpec(bias_t.shape, lambda bi, i: (0, 0, 0, 0)),
                  pl.BlockSpec(sink_t.shape, lambda bi, i: (0, 0, 0))],
        out_specs=pl.BlockSpec((None, tl, SWA_WIDTH), lambda bi, i: (bi, i, 0)),
        out_shape=jax.ShapeDtypeStruct((b, l, SWA_WIDTH), BF16),
        compiler_params=_cparams(("parallel", "arbitrary")),
        name="swa_prompt",
    )(z, z, z, z, z, bias_t, sink_t)


def _attn_sample_kernel(sink_ref, q_ref, kp_ref, kc_ref, vp_ref, vc_ref, bias_ref, o_ref):
    q = (q_ref[...] * (HEAD_DIM ** -0.5)).astype(BF16)
    k = jnp.concatenate([kp_ref[...], kc_ref[...]], axis=0)
    v = jnp.concatenate([vp_ref[...], vc_ref[...]], axis=0)
    lk = k.shape[0]
    lo = lax.broadcasted_iota(jnp.int32, (lk, LANES), 1) < HALF
    qlane = lax.broadcasted_iota(jnp.int32, (q.shape[0], LANES), 1)
    for t in range(KV_WIDTH // LANES):
        kt = k[:, t * LANES:(t + 1) * LANES]
        vt = v[:, t * LANES:(t + 1) * LANES]
        kt_r = pltpu.roll(kt, HALF, 1)
        vt_r = pltpu.roll(vt, HALF, 1)
        for half in range(2):
            kv = 2 * t + half
            k_in_lo, k_in_hi = (kt, kt_r) if half == 0 else (kt_r, kt)
            v_in_lo, v_in_hi = (vt, vt_r) if half == 0 else (vt_r, vt)
            k_half = (jnp.where(lo, k_in_lo, 0.0).astype(BF16), jnp.where(lo, 0.0, k_in_hi).astype(BF16))
            v_half = (jnp.where(lo, v_in_lo, 0.0).astype(BF16), jnp.where(lo, 0.0, v_in_hi).astype(BF16))
            for qt in (2 * kv, 2 * kv + 1):
                q2 = q[:, qt * LANES:(qt + 1) * LANES]
                acc = None
                inv = []
                for hh in range(2):
                    h = 2 * qt + hh
                    s = _nt(q2, k_half[hh]) + bias_ref[h]
                    sink = sink_ref[h]
                    m = jnp.maximum(jnp.max(s, axis=-1, keepdims=True), sink)
                    e = jnp.exp(s - m)
                    denom = jnp.sum(e, axis=-1, keepdims=True) + jnp.exp(sink - m)
                    inv.append(1.0 / denom)
                    pv = jnp.dot(e.astype(BF16), v_half[hh], preferred_element_type=F32)
                    acc = pv if acc is None else acc + pv
                o2 = acc * jnp.where(qlane < HALF, inv[0], inv[1])
                o_ref[:, qt * LANES:(qt + 1) * LANES] = o2.astype(o_ref.dtype)


def _attn_sample(z, cache_k, cache_v, bias, sinks, li):
    b, tq, _ = z.shape
    kcol, vcol = OFF_K // KV_WIDTH, OFF_V // KV_WIDTH
    cache = pl.BlockSpec((None, None, WINDOW, KV_WIDTH), lambda bi: (li, bi, 0, 0))
    return pl.pallas_call(
        _attn_sample_kernel,
        grid=(b,),
        in_specs=[pl.BlockSpec(memory_space=pltpu.SMEM),
                  pl.BlockSpec((None, tq, SWA_WIDTH), lambda bi: (bi, 0, 0)),
                  cache,
                  pl.BlockSpec((None, tq, KV_WIDTH), lambda bi: (bi, 0, kcol)),
                  cache,
                  pl.BlockSpec((None, tq, KV_WIDTH), lambda bi: (bi, 0, vcol)),
                  pl.BlockSpec(bias.shape, lambda bi: (0, 0, 0))],
        out_specs=pl.BlockSpec((None, tq, SWA_WIDTH), lambda bi: (bi, 0, 0)),
        out_shape=jax.ShapeDtypeStruct((b, tq, SWA_WIDTH), BF16),
        compiler_params=_cparams(("parallel",)),
        name="swa_sample",
    )(sinks, z, cache_k, z, cache_v, z, bias)


def _gmlp_kernel(bu_ref, bv_ref, w_ref, bs_ref, gv_ref, o_ref, *maybe_vn_ref, c, n_chunks):
    row = lax.broadcasted_iota(jnp.int32, (c, c), 0)
    col = lax.broadcasted_iota(jnp.int32, (c, c), 1)
    causal = (row // CHUNK) >= (col // CHUNK)
    v = jax.nn.gelu(bv_ref[...])
    vc = v - jnp.mean(v, axis=-1, keepdims=True)
    vn = (vc * lax.rsqrt(jnp.mean(vc * vc, axis=-1, keepdims=True) + NORM_EPS)) * gv_ref[...]
    if maybe_vn_ref:
        maybe_vn_ref[0][...] = vn
    u = jax.nn.gelu(bu_ref[...])
    vn16 = vn.astype(BF16)
    for g in range(GMLP_GROUPS):
        wg = jnp.where(causal, w_ref[g], 0.0).astype(BF16)
        bias = bs_ref[:, g:g + 1]
        cols = slice(g * LANES, (g + 1) * LANES)
        for ci in range(n_chunks):
            rows = slice(ci * c, (ci + 1) * c)
            sv = jnp.dot(wg, vn16[rows, cols], preferred_element_type=F32) + bias
            o_ref[rows, cols] = (u[rows, cols] * sv).astype(o_ref.dtype)


def _gmlp(z, w_s, b_st, g_vnorm, li, c, tl, want_vn):
    b, l, _ = z.shape
    ucol, vcol = OFF_BU // GMLP_WIDTH, OFF_BV // GMLP_WIDTH
    out_shape = [jax.ShapeDtypeStruct((b, l, GMLP_WIDTH), BF16)]
    out_specs = [pl.BlockSpec((None, tl, GMLP_WIDTH), lambda bi, i: (bi, i, 0))]
    if want_vn:
        out_shape.append(jax.ShapeDtypeStruct((b, l, GMLP_WIDTH), F32))
        out_specs.append(pl.BlockSpec((None, tl, GMLP_WIDTH), lambda bi, i: (bi, i, 0)))
    return pl.pallas_call(
        functools.partial(_gmlp_kernel, c=c, n_chunks=tl // c),
        grid=(b, l // tl),
        in_specs=[pl.BlockSpec((None, tl, GMLP_WIDTH), lambda bi, i: (bi, i, ucol)),
                  pl.BlockSpec((None, tl, GMLP_WIDTH), lambda bi, i: (bi, i, vcol)),
                  pl.BlockSpec((None, GMLP_GROUPS, c, c), lambda bi, i: (li, 0, 0, 0)),
                  pl.BlockSpec((None, c, GMLP_GROUPS), lambda bi, i: (li, 0, 0)),
                  pl.BlockSpec((None, 1, GMLP_WIDTH), lambda bi, i: (li, 0, 0))],
        out_specs=out_specs,
        out_shape=out_shape,
        compiler_params=_cparams(("parallel", "parallel")),
        name="gmlp",
    )(z, z, w_s, b_st, g_vnorm)


def _log_sigmoid(x):
    return jnp.minimum(x, 0.0) - jnp.log1p(jnp.exp(-jnp.abs(x)))


def _cumsum_rows(x):
    n = x.shape[0]
    row = lax.broadcasted_iota(jnp.int32, x.shape, 0)
    shift = 1
    while shift < n:
        x = x + jnp.where(row >= shift, pltpu.roll(x, shift, 0), 0.0)
        shift *= 2
    return x


def _gla_specs(tl, li, nb_axes):
    cq, ck = OFF_CQ // KV_WIDTH, OFF_CK // KV_WIDTH
    cv, co = OFF_CV // GLA_WIDTH, OFF_CO // GLA_WIDTH
    cg = OFF_CG // CG_PAD
    zrow = lambda width, col: pl.BlockSpec((None, tl, width), lambda bi, i: (bi, i, col))
    return [zrow(KV_WIDTH, cq), zrow(KV_WIDTH, ck), zrow(GLA_WIDTH, cv), zrow(GLA_WIDTH, co), zrow(CG_PAD, cg),
            pl.BlockSpec((None, CG_PAD, KV_WIDTH), lambda bi, i: (li, 0, 0)),
            pl.BlockSpec((None, 1, KV_WIDTH), lambda bi, i: (li, 0, 0)),
            pl.BlockSpec((None, 1, GLA_WIDTH), lambda bi, i: (li, 0, 0))]


def _gla_sample_kernel(cq_ref, ck_ref, cv_ref, co_ref, cg_ref, wg_ref, bg_ref, gon_ref, s0_ref,
                       o_ref, sfin_ref):
    c = cq_ref.shape[0]
    lane = lax.broadcasted_iota(jnp.int32, (c, LANES), 1)
    lo = lane < HALF
    row = lax.broadcasted_iota(jnp.int32, (c, c), 0)
    col = lax.broadcasted_iota(jnp.int32, (c, c), 1)
    tril = row >= col
    gate = jnp.dot(cg_ref[...].astype(BF16), wg_ref[...], preferred_element_type=F32) + bg_ref[...]
    la = _log_sigmoid(gate) / GLA_GATE_TEMP
    bcum = _cumsum_rows(la)
    blast = bcum[c - 1:c, :]
    q = cq_ref[...] * (GLA_DK ** -0.5)
    k = ck_ref[...]
    qd = q * jnp.exp(bcum)
    kd = k * jnp.exp(-bcum)
    kl = k * jnp.exp(blast - bcum)
    dec = jnp.exp(blast)
    for j in range(2):
        tile = slice(j * LANES, (j + 1) * LANES)
        qd_t, kd_t, kl_t = qd[:, tile], kd[:, tile], kl[:, tile]
        st = s0_ref[j].T
        st16 = st.astype(BF16)
        st_new = st * dec[:, tile]
        for hh in range(2):
            h = 2 * j + hh
            keep = lo if hh == 0 else jnp.logical_not(lo)
            hc = slice(h * GLA_DV, (h + 1) * GLA_DV)
            v_h = cv_ref[:, hc].astype(BF16)
            att = _nt(qd_t.astype(BF16), jnp.where(keep, kd_t, 0.0).astype(BF16))
            att = jnp.where(tril, att, 0.0)
            o_h = jnp.dot(att.astype(BF16), v_h, preferred_element_type=F32)
            o_h = o_h + _nt(jnp.where(keep, qd_t, 0.0).astype(BF16), st16)
            st_new = st_new + _tn(v_h, jnp.where(keep, kl_t, 0.0).astype(BF16))
            o_n = _rms(o_h, gon_ref[:, hc])
            o_ref[:, hc] = (o_n * jax.nn.silu(co_ref[:, hc])).astype(o_ref.dtype)
        sfin_ref[j] = st_new.T


def _gla_sample(z, w_gate, b_gate, g_onorm, state, li):
    b, tl, _ = z.shape
    st_spec = pl.BlockSpec((None, 2, LANES, GLA_DV), lambda bi, i: (bi, 0, 0, 0))
    return pl.pallas_call(
        _gla_sample_kernel,
        grid=(b, 1),
        in_specs=_gla_specs(tl, li, 2) + [pl.BlockSpec((None, None, 2, LANES, GLA_DV),
                                                       lambda bi, i: (li, bi, 0, 0, 0))],
        out_specs=[pl.BlockSpec((None, tl, GLA_WIDTH), lambda bi, i: (bi, i, 0)), st_spec],
        out_shape=[jax.ShapeDtypeStruct((b, tl, GLA_WIDTH), BF16),
                   jax.ShapeDtypeStruct((b, 2, LANES, GLA_DV), F32)],
        compiler_params=_cparams(("parallel", "arbitrary")),
        name="gla_sample",
    )(z, z, z, z, z, w_gate, b_gate, g_onorm, state)


def _gla_prompt_kernel(cq_ref, ck_ref, cv_ref, co_ref, cg_ref, wg_ref, bg_ref, gon_ref,
                       o_ref, sfin_ref, st_ref):
    tl = cq_ref.shape[0]
    n_chunks = tl // CHUNK

    @pl.when(pl.program_id(1) == 0)
    def _():
        st_ref[...] = jnp.zeros_like(st_ref)

    gate = jnp.dot(cg_ref[...].astype(BF16), wg_ref[...], preferred_element_type=F32) + bg_ref[...]
    la = _log_sigmoid(gate) / GLA_GATE_TEMP
    la_hi = la.astype(BF16)
    la_lo = (la - la_hi.astype(F32)).astype(BF16)
    row = lax.broadcasted_iota(jnp.int32, (GLA_SUPER, GLA_SUPER), 0)
    col = lax.broadcasted_iota(jnp.int32, (GLA_SUPER, GLA_SUPER), 1)
    tri = ((row // CHUNK) == (col // CHUNK)) & (row >= col)
    tri16 = jnp.where(tri, 1.0, 0.0).astype(BF16)
    bcum = jnp.concatenate(
        [jnp.dot(tri16, la_hi[s * GLA_SUPER:(s + 1) * GLA_SUPER], preferred_element_type=F32)
         + jnp.dot(tri16, la_lo[s * GLA_SUPER:(s + 1) * GLA_SUPER], preferred_element_type=F32)
         for s in range(tl // GLA_SUPER)], axis=0)
    blast = [bcum[(ci + 1) * CHUNK - 1:(ci + 1) * CHUNK, :] for ci in range(n_chunks)]
    blast_b = jnp.concatenate([jnp.broadcast_to(bl, (CHUNK, KV_WIDTH)) for bl in blast], axis=0)
    k = ck_ref[...]
    qd = cq_ref[...] * (GLA_DK ** -0.5) * jnp.exp(bcum)
    kd = k * jnp.exp(-bcum)
    kl = k * jnp.exp(blast_b - bcum)
    lo = lax.broadcasted_iota(jnp.int32, (tl, LANES), 1) < HALF

    for j in range(2):
        tile = slice(j * LANES, (j + 1) * LANES)
        qd_t, kd_t, kl_t = qd[:, tile], kd[:, tile], kl[:, tile]
        qd16 = qd_t.astype(BF16)
        hcs = [slice((2 * j + hh) * GLA_DV, (2 * j + hh + 1) * GLA_DV) for hh in range(2)]
        keeps = [lo, jnp.logical_not(lo)]
        v = [cv_ref[:, hc].astype(BF16) for hc in hcs]
        qm = [jnp.where(kp, qd_t, 0.0).astype(BF16) for kp in keeps]
        kdm = [jnp.where(kp, kd_t, 0.0).astype(BF16) for kp in keeps]
        klm = [jnp.where(kp, kl_t, 0.0).astype(BF16) for kp in keeps]
        chunk_rows = [slice(ci * CHUNK, (ci + 1) * CHUNK) for ci in range(n_chunks)]
        stack = lambda pair, rows: jnp.concatenate([pair[0][rows], pair[1][rows]], axis=0)
        upd = [_tn(stack(v, rows), stack(klm, rows)) for rows in chunk_rows]
        st = st_ref[j]
        st16 = []
        for ci in range(n_chunks):
            st16.append(st.astype(BF16))
            st = st * jnp.exp(blast[ci][:, tile]) + upd[ci]
        st_ref[j] = st
        inter = [_nt(stack(qm, rows), st16[ci]) for ci, rows in enumerate(chunk_rows)]
        intra = [[], []]
        for s in range(tl // GLA_SUPER):
            rows = slice(s * GLA_SUPER, (s + 1) * GLA_SUPER)
            att = _nt(qd16[rows], jnp.concatenate([kdm[0][rows], kdm[1][rows]], axis=0))
            for hh in range(2):
                a_h = jnp.where(tri, att[:, hh * GLA_SUPER:(hh + 1) * GLA_SUPER], 0.0).astype(BF16)
                intra[hh].append(jnp.dot(a_h, v[hh][rows], preferred_element_type=F32))
        for hh in range(2):
            o_h = jnp.concatenate(intra[hh], axis=0) + jnp.concatenate(
                [blk[hh * CHUNK:(hh + 1) * CHUNK] for blk in inter], axis=0)
            o_n = _rms(o_h, gon_ref[:, hcs[hh]])
            o_ref[:, hcs[hh]] = (o_n * jax.nn.silu(co_ref[:, hcs[hh]])).astype(o_ref.dtype)

    @pl.when(pl.program_id(1) == pl.num_programs(1) - 1)
    def _():
        for j in range(2):
            sfin_ref[j] = st_ref[j].T


def _gla_prompt(z, w_gate, b_gate, g_onorm, li, tl):
    b, l, _ = z.shape
    return pl.pallas_call(
        _gla_prompt_kernel,
        grid=(b, l // tl),
        in_specs=_gla_specs(tl, li, 2),
        out_specs=[pl.BlockSpec((None, tl, GLA_WIDTH), lambda bi, i: (bi, i, 0)),
                   pl.BlockSpec((None, 2, LANES, GLA_DV), lambda bi, i: (bi, 0, 0, 0))],
        out_shape=[jax.ShapeDtypeStruct((b, l, GLA_WIDTH), BF16),
                   jax.ShapeDtypeStruct((b, 2, LANES, GLA_DV), F32)],
        scratch_shapes=[pltpu.VMEM((2, GLA_DV, LANES), F32)],
        compiler_params=_cparams(("parallel", "arbitrary")),
        name="gla_prompt",
    )(z, z, z, z, z, w_gate, b_gate, g_onorm)


def _outproj_kernel(x_ref, oa_ref, ob_ref, oc_ref, w_ref, g_ref, y_ref):
    mix = jnp.dot(oa_ref[...], w_ref[0:SWA_WIDTH, :], preferred_element_type=F32)
    mix = mix + jnp.dot(ob_ref[...], w_ref[SWA_WIDTH:SWA_WIDTH + GMLP_WIDTH, :], preferred_element_type=F32)
    mix = mix + jnp.dot(oc_ref[...], w_ref[SWA_WIDTH + GMLP_WIDTH:, :], preferred_element_type=F32)
    y_ref[...] = x_ref[...] + _rms(mix, g_ref[...])


def _outproj(x, oa, ob, oc, w, g, li, tm):
    m = x.shape[0]
    row = lambda width: pl.BlockSpec((tm, width), lambda i: (i, 0))
    return pl.pallas_call(
        _outproj_kernel,
        grid=(m // tm,),
        in_specs=[row(D_MODEL), row(SWA_WIDTH), row(GMLP_WIDTH), row(GLA_WIDTH),
                  pl.BlockSpec((None, D_MODEL, D_MODEL), lambda i: (li, 0, 0)),
                  pl.BlockSpec((None, 1, D_MODEL), lambda i: (li, 0, 0))],
        out_specs=row(D_MODEL),
        out_shape=jax.ShapeDtypeStruct((m, D_MODEL), F32),
        compiler_params=_cparams(("parallel",)),
        name="outproj",
    )(x, oa, ob, oc, w, g)


def _ffn_kernel(x_ref, gpre_ref, wup_ref, wdn_ref, gpost_ref, y_ref, h_ref):
    k = pl.program_id(1)

    @pl.when(k == 0)
    def _():
        h_ref[...] = _rms(x_ref[...], gpre_ref[...]).astype(BF16)
        y_ref[...] = jnp.zeros_like(y_ref)

    u = jnp.maximum(jnp.dot(h_ref[...], wup_ref[...], preferred_element_type=F32), 0.0)
    y_ref[...] += jnp.dot((u * u).astype(BF16), wdn_ref[...], preferred_element_type=F32)

    @pl.when(k == pl.num_programs(1) - 1)
    def _():
        y_ref[...] = x_ref[...] + _rms(y_ref[...], gpost_ref[...])


def _ffn(x, gpre, wup, wdn, gpost, li, tm, tk):
    m = x.shape[0]
    gain = pl.BlockSpec((None, 1, D_MODEL), lambda i, k: (li, 0, 0))
    return pl.pallas_call(
        _ffn_kernel,
        grid=(m // tm, D_FF // tk),
        in_specs=[pl.BlockSpec((tm, D_MODEL), lambda i, k: (i, 0)),
                  gain,
                  pl.BlockSpec((None, D_MODEL, tk), lambda i, k: (li, 0, k)),
                  pl.BlockSpec((None, tk, D_MODEL), lambda i, k: (li, k, 0)),
                  gain],
        out_specs=pl.BlockSpec((tm, D_MODEL), lambda i, k: (i, 0)),
        out_shape=jax.ShapeDtypeStruct((m, D_MODEL), F32),
        scratch_shapes=[pltpu.VMEM((tm, D_MODEL), BF16)],
        compiler_params=_cparams(("parallel", "arbitrary"), FFN_VMEM_LIMIT),
        name="ffn",
    )(x, gpre, wup, wdn, gpost)


def _ple_kernel(x_ref, pe_ref, wp_ref, wg_ref, y_ref):
    x = x_ref[...]
    gate = jax.nn.sigmoid(jnp.dot(x.astype(BF16), wg_ref[...], preferred_element_type=F32))
    y_ref[...] = x + jnp.dot(pe_ref[...].astype(BF16), wp_ref[...], preferred_element_type=F32) * gate


def _ple(x, pe, wp, wg, li, tm):
    m = x.shape[0]
    return pl.pallas_call(
        _ple_kernel,
        grid=(m // tm,),
        in_specs=[pl.BlockSpec((tm, D_MODEL), lambda i: (i, 0)),
                  pl.BlockSpec((None, tm, PLE_DIM), lambda i: (li, i, 0)),
                  pl.BlockSpec((None, PLE_DIM, D_MODEL), lambda i: (li, 0, 0)),
                  pl.BlockSpec((None, D_MODEL, D_MODEL), lambda i: (li, 0, 0))],
        out_specs=pl.BlockSpec((tm, D_MODEL), lambda i: (i, 0)),
        out_shape=jax.ShapeDtypeStruct((m, D_MODEL), F32),
        compiler_params=_cparams(("parallel",)),
        name="ple",
    )(x, pe, wp, wg)


def _t5_bucket(rel):
    half = REL_BUCKETS // 2
    max_exact = half // 2
    n = -rel
    ret = jnp.where(n < 0, half, 0)
    n = jnp.abs(n)
    nf = jnp.maximum(n, 1).astype(jnp.float32)
    large = max_exact + (jnp.log(nf / max_exact) / math.log(REL_MAX_DIST / max_exact)
                         * (half - max_exact)).astype(jnp.int32)
    large = jnp.minimum(large, half - 1)
    return ret + jnp.where(n < max_exact, n, large)


def _bias_kernel(tab_ref, bkt_p_ref, valid_p_ref, bkt_s_ref, out_p_ref, out_s_ref):
    bkt_p = bkt_p_ref[...]
    bkt_s = bkt_s_ref[...]
    for h in range(SWA_Q_HEADS):
        def pick(b, accs):
            val = tab_ref[b, h]
            return (jnp.where(bkt_p == b, val, accs[0]), jnp.where(bkt_s == b, val, accs[1]))

        acc_p, acc_s = lax.fori_loop(0, REL_BUCKETS, pick,
                                     (jnp.zeros(bkt_p.shape, F32), jnp.zeros(bkt_s.shape, F32)))
        kv, hq = divmod(h, SWA_GROUP)
        for var in range(2):
            out_p_ref[var, kv, :, hq * LANES:(hq + 1) * LANES] = jnp.where(valid_p_ref[var] != 0, acc_p * LOG2E, MASK_VALUE)
        out_s_ref[h] = acc_s


def _bias_tables(rel_bias, dec_seq):
    qpos = jnp.arange(WINDOW)[None, :]
    kpos = jnp.arange(-WINDOW, WINDOW)[:, None]
    bkt_p = _t5_bucket(kpos - qpos).astype(jnp.int32)
    kchunk = jnp.floor_divide(kpos, CHUNK)
    qchunk = qpos // CHUNK
    valid = (kchunk >= qchunk - WINDOW // CHUNK) & (kchunk <= qchunk)
    valid_p = jnp.stack([valid & (kpos >= 0), valid]).astype(jnp.int32)
    bkt_s = _t5_bucket(jnp.arange(-WINDOW, dec_seq)[None, :] - jnp.arange(dec_seq)[:, None]).astype(jnp.int32)
    full = lambda a: pl.BlockSpec(a.shape, lambda: (0,) * a.ndim)
    out_shape = [jax.ShapeDtypeStruct((2, SWA_KV_HEADS, 2 * WINDOW, SWA_GROUP * LANES), F32),
                 jax.ShapeDtypeStruct((SWA_Q_HEADS, dec_seq, WINDOW + dec_seq), F32)]
    return pl.pallas_call(
        _bias_kernel,
        in_specs=[pl.BlockSpec(memory_space=pltpu.SMEM), full(bkt_p), full(valid_p), full(bkt_s)],
        out_specs=[full(s) for s in out_shape],
        out_shape=out_shape,
        name="rel_bias",
    )(rel_bias, bkt_p, valid_p, bkt_s)


def _prep_w_in(w):
    real = OFF_CO + GLA_GATE_RANK
    pad = jnp.zeros(w.shape[:2] + (CG_PAD - GLA_GATE_RANK,), w.dtype)
    return jnp.concatenate([w[..., :OFF_CO], w[..., real:], w[..., OFF_CO:real], pad], axis=-1).astype(BF16)


def kernel(x_prompt, x_sample, cache_swa_k, cache_swa_v, state_gla, p_prompt, p_sample, w_in, w_gate, b_gate, rel_bias, attn_sinks, w_spatial, b_spatial, g_gmlp_vnorm, g_gla_onorm, w_out, g_mix_pre, g_mix_post, g_ffn_pre, g_ffn_post, w_up, w_down, w_ple, w_ple_gate):
    bp, lp, _ = x_prompt.shape
    bs, ls, _ = x_sample.shape
    mp, ms = bp * lp, bs * ls
    row3 = lambda a: a.reshape(DEPTH, 1, -1)

    bias_p, bias_s = _bias_tables(rel_bias, ls)
    sink_t = jnp.repeat(attn_sinks * LOG2E, LANES, axis=-1).reshape(DEPTH, SWA_KV_HEADS, 1, SWA_GROUP * LANES)
    sinks_flat = attn_sinks.reshape(-1)

    w_in16 = _prep_w_in(w_in)
    w_out16, w_up16, w_dn16 = w_out.astype(BF16), w_up.astype(BF16), w_down.astype(BF16)
    w_ple16, w_plg16 = w_ple.astype(BF16), w_ple_gate.astype(BF16)
    wg16 = jnp.concatenate([w_gate, jnp.zeros((DEPTH, CG_PAD - GLA_GATE_RANK, KV_WIDTH), F32)], axis=1).astype(BF16)
    g_pre, g_post, g_fpre, g_fpost = row3(g_mix_pre), row3(g_mix_post), row3(g_ffn_pre), row3(g_ffn_post)
    bg, gon, gvn = row3(b_gate), row3(g_gla_onorm), row3(g_gmlp_vnorm)
    b_st = jnp.swapaxes(b_spatial, 1, 2)
    w_s_small, b_st_small = w_spatial[:, :, :ls, :ls], b_st[:, :ls, :]
    pe_p = p_prompt.reshape(DEPTH, mp, PLE_DIM)
    pe_s = p_sample.reshape(DEPTH, ms, PLE_DIM)
    cache_k = cache_swa_k.reshape(DEPTH, bs, WINDOW, KV_WIDTH)
    cache_v = cache_swa_v.reshape(DEPTH, bs, WINDOW, KV_WIDTH)
    state = state_gla.reshape(DEPTH, bs, 2, LANES, GLA_DV)

    def dense_tail(x, oa, ob, oc, pe, li, tm):
        m = x.shape[0]
        x = _outproj(x, oa.reshape(m, -1), ob.reshape(m, -1), oc.reshape(m, -1), w_out16, g_post, li, min(tm, 512))
        x = _ffn(x, g_fpre, w_up16, w_dn16, g_fpost, li, tm, 512)
        return _ple(x, pe, w_ple16, w_plg16, li, min(tm, 512))

    xp = x_prompt.reshape(mp, D_MODEL)
    xs = x_sample.reshape(ms, D_MODEL)
    kp_rows, vp_rows, ks_rows, vs_rows, sp_states, ss_states, gv_rows = [], [], [], [], [], [], []
    for li in range(DEPTH):
        zp = _inproj(xp, g_pre, w_in16, li, 1024, 1408).reshape(bp, lp, Z_WIDTH)
        oa = _attn_prompt(zp, bias_p, sink_t[li], ATTN_ROWS)
        ob, = _gmlp(zp, w_spatial, b_st, gvn, li, GMLP_CHUNK, 512, False)
        oc, s_p = _gla_prompt(zp, wg16, bg, gon, li, GLA_ROWS)
        xp = dense_tail(xp, oa, ob, oc, pe_p, li, 1024)
        kp_rows.append(zp[:, lp - WINDOW:, OFF_K:OFF_K + KV_WIDTH].reshape(bp, WINDOW, SWA_KV_HEADS, HEAD_DIM))
        vp_rows.append(zp[:, lp - WINDOW:, OFF_V:OFF_V + KV_WIDTH].reshape(bp, WINDOW, SWA_KV_HEADS, HEAD_DIM))
        sp_states.append(s_p.reshape(bp, GLA_HEADS, GLA_DK, GLA_DV))

        zs = _inproj(xs, g_pre, w_in16, li, ms, 1408).reshape(bs, ls, Z_WIDTH)
        oa = _attn_sample(zs, cache_k, cache_v, bias_s, sinks_flat[li * SWA_Q_HEADS:(li + 1) * SWA_Q_HEADS], li)
        ob, vn_s = _gmlp(zs, w_s_small, b_st_small, gvn, li, ls, ls, True)
        oc, s_s = _gla_sample(zs, wg16, bg, gon, state, li)
        xs = dense_tail(xs, oa, ob, oc, pe_s, li, ms)
        ks_rows.append(zs[:, :, OFF_K:OFF_K + KV_WIDTH].reshape(bs, ls, SWA_KV_HEADS, HEAD_DIM))
        vs_rows.append(zs[:, :, OFF_V:OFF_V + KV_WIDTH].reshape(bs, ls, SWA_KV_HEADS, HEAD_DIM))
        ss_states.append(s_s.reshape(bs, GLA_HEADS, GLA_DK, GLA_DV).astype(state_gla.dtype))
        gv_rows.append(vn_s)

    return (xp.reshape(bp, lp, D_MODEL), xs.reshape(bs, ls, D_MODEL),
            jnp.stack(kp_rows), jnp.stack(vp_rows), jnp.stack(ks_rows), jnp.stack(vs_rows),
            jnp.stack(sp_states), jnp.stack(ss_states), jnp.stack(gv_rows))
```

```python
import functools
import math

import jax
import jax.numpy as jnp
from jax import lax
from jax.experimental import pallas as pl
from jax.experimental.pallas import tpu as pltpu

F32 = jnp.float32
BF16 = jnp.bfloat16

D_MODEL = 2048
DEPTH = 4
CHUNK = 64
HEAD_DIM = 64
SWA_Q_HEADS = 16
SWA_KV_HEADS = 4
SWA_GROUP = SWA_Q_HEADS // SWA_KV_HEADS
SWA_WIDTH = 1024
KV_WIDTH = 256
WINDOW = 128
REL_BUCKETS = 32
REL_MAX_DIST = 128
GMLP_GROUPS = 4
GMLP_WIDTH = 512
GMLP_CHUNK = 128
GLA_HEADS = 4
GLA_DK = 64
GLA_DV = 128
GLA_WIDTH = 512
GLA_GATE_RANK = 16
GLA_GATE_TEMP = 16.0
D_FF = 4 * D_MODEL
PLE_DIM = 256
NORM_EPS = 1e-6
LANES = 128
HALF = LANES // 2
BF16_ROWS = 16

OFF_Q, OFF_K, OFF_V = 0, 1024, 1280
OFF_BU, OFF_BV = 1536, 2048
OFF_CQ, OFF_CK, OFF_CV, OFF_CO = 2560, 2816, 3072, 3584
Z_WIDTH = 4096
CG_PAD = LANES

LOG2E = math.log2(math.e)
MASK_VALUE = -1e30
VMEM_LIMIT = 52 * 1024 * 1024
FFN_VMEM_LIMIT = 60 * 1024 * 1024

FFN_ROWS = 1024
FFN_CHUNK = 1024
INPROJ_COLS = 2048
DENSE_ROWS = 512
OUTPROJ_SUB = 4
PLE_SUB = 2
ATTN_ROWS = 512
GLA_ROWS = 512
GLA_SUPER = 2 * CHUNK
GLA_SUB = 16


def _cparams(sem, vmem=VMEM_LIMIT):
    return pltpu.CompilerParams(dimension_semantics=sem, vmem_limit_bytes=vmem)


def _rms(x, g):
    return (x * lax.rsqrt(jnp.mean(x * x, axis=-1, keepdims=True) + NORM_EPS)) * g


def _nt(a, b):
    return lax.dot_general(a, b, (((1,), (1,)), ((), ())), preferred_element_type=F32)


def _tn(a, b):
    return lax.dot_general(a, b, (((0,), (0,)), ((), ())), preferred_element_type=F32)


def _prenorm_kernel(x_ref, g_ref, h_ref):
    h_ref[...] = _rms(x_ref[...], g_ref[...]).astype(BF16)


def _prenorm(x, g, li, tm):
    m = x.shape[0]
    return pl.pallas_call(
        _prenorm_kernel,
        grid=(m // tm,),
        in_specs=[pl.BlockSpec((tm, D_MODEL), lambda i: (i, 0)),
                  pl.BlockSpec((None, 1, D_MODEL), lambda i: (li, 0, 0))],
        out_specs=pl.BlockSpec((tm, D_MODEL), lambda i: (i, 0)),
        out_shape=jax.ShapeDtypeStruct((m, D_MODEL), BF16),
        compiler_params=_cparams(("parallel",)),
        name="prenorm",
    )(x, g)


def _inproj_kernel(h_ref, w_ref, wcg_ref, z_ref, zcg_ref):
    z_ref[...] = jnp.dot(h_ref[...], w_ref[...], preferred_element_type=F32)

    @pl.when(pl.program_id(1) == 0)
    def _():
        zcg_ref[...] = jnp.dot(h_ref[...], wcg_ref[...], preferred_element_type=F32)


def _inproj(h, w, wcg, li, tm, tn):
    m = h.shape[0]
    return pl.pallas_call(
        _inproj_kernel,
        grid=(m // tm, Z_WIDTH // tn),
        in_specs=[pl.BlockSpec((tm, D_MODEL), lambda i, j: (i, 0)),
                  pl.BlockSpec((None, D_MODEL, tn), lambda i, j: (li, 0, j)),
                  pl.BlockSpec((None, D_MODEL, CG_PAD), lambda i, j: (li, 0, 0))],
        out_specs=[pl.BlockSpec((tm, tn), lambda i, j: (i, j)),
                   pl.BlockSpec((tm, CG_PAD), lambda i, j: (i, 0))],
        out_shape=[jax.ShapeDtypeStruct((m, Z_WIDTH), F32), jax.ShapeDtypeStruct((m, CG_PAD), F32)],
        compiler_params=_cparams(("parallel", "arbitrary")),
        name="inproj",
    )(h, w, wcg)


def _attn_prompt_kernel(q_ref, kp_ref, kc_ref, vp_ref, vc_ref, bias_ref, sink_ref, o_ref, *, nsub):
    first = pl.program_id(1) == 0
    kfull = jnp.concatenate([kp_ref[...], kc_ref[...]], axis=0)
    vfull = jnp.concatenate([vp_ref[...], vc_ref[...]], axis=0)
    rows = kfull.shape[0]
    lo_k = lax.broadcasted_iota(jnp.int32, (rows, LANES), 1) < HALF
    lo_q = lax.broadcasted_iota(jnp.int32, (WINDOW, LANES), 1) < HALF
    kdup, v_t = [], []
    for t in range(KV_WIDTH // LANES):
        kt = kfull[:, t * LANES:(t + 1) * LANES]
        kt_r = pltpu.roll(kt, HALF, 1)
        kdup.append(jnp.where(lo_k, kt, kt_r).astype(BF16))
        kdup.append(jnp.where(lo_k, kt_r, kt).astype(BF16))
        vt_t = vfull[:, t * LANES:(t + 1) * LANES].T
        v_t.append(vt_t[:HALF].astype(BF16))
        v_t.append(vt_t[HALF:].astype(BF16))
    ones = jnp.ones((BF16_ROWS, 2 * WINDOW), BF16)

    def scores(j, kv):
        qa = q_ref[j * WINDOW:(j + 1) * WINDOW, (2 * kv) * LANES:(2 * kv + 1) * LANES] * (HEAD_DIM ** -0.5 * LOG2E)
        qb = q_ref[j * WINDOW:(j + 1) * WINDOW, (2 * kv + 1) * LANES:(2 * kv + 2) * LANES] * (HEAD_DIM ** -0.5 * LOG2E)
        qm = jnp.concatenate([jnp.where(lo_q, qa, 0.0), jnp.where(lo_q, 0.0, qa),
                              jnp.where(lo_q, qb, 0.0), jnp.where(lo_q, 0.0, qb)], axis=0).astype(BF16)
        return _nt(kdup[kv][j * WINDOW:(j + 2) * WINDOW], qm)

    steps = [(j, kv) for j in range(nsub) for kv in range(SWA_KV_HEADS)]
    s_next = scores(*steps[0])
    for n, (j, kv) in enumerate(steps):
        s_t = s_next
        if n + 1 < len(steps):
            s_next = scores(*steps[n + 1])
        keys = slice(j * WINDOW, (j + 2) * WINDOW)
        var = jnp.where(first, 0, 1) if j == 0 else 1
        sink = sink_ref[kv]
        e_cols, m_cols = [], []
        for hq in range(SWA_GROUP):
            cols = slice(hq * LANES, (hq + 1) * LANES)
            s_h = s_t[:, cols] + bias_ref[var, kv, :, cols]
            m_h = jnp.maximum(jnp.max(s_h, axis=0, keepdims=True), sink[:, cols])
            e_cols.append(jnp.exp2(s_h - m_h).astype(BF16))
            m_cols.append(m_h)
        e = jnp.concatenate(e_cols, axis=1)
        m = jnp.concatenate(m_cols, axis=1)
        lhs = jnp.concatenate([v_t[kv][:, keys], ones], axis=0)
        r = jnp.dot(lhs, e, preferred_element_type=F32)
        denom = r[HALF:HALF + 1] + jnp.exp2(sink - m)
        o_t = r[:HALF] * (1.0 / denom)
        for qt in range(2):
            pair = jnp.concatenate([o_t[:, (2 * qt) * LANES:(2 * qt + 1) * LANES],
                                    o_t[:, (2 * qt + 1) * LANES:(2 * qt + 2) * LANES]], axis=0)
            col = (2 * kv + qt) * LANES
            o_ref[j * WINDOW:(j + 1) * WINDOW, col:col + LANES] = pair.T.astype(o_ref.dtype)


def _attn_prompt(z, bias_t, sink_t, tl):
    b, l, _ = z.shape
    kcol, vcol = OFF_K // KV_WIDTH, OFF_V // KV_WIDTH
    per = tl // WINDOW
    prev = lambda col: pl.BlockSpec((None, WINDOW, KV_WIDTH),
                                    lambda bi, i: (bi, jnp.maximum(i * per - 1, 0), col))
    cur = lambda col: pl.BlockSpec((None, tl, KV_WIDTH), lambda bi, i: (bi, i, col))
    return pl.pallas_call(
        functools.partial(_attn_prompt_kernel, nsub=per),
        grid=(b, l // tl),
        in_specs=[pl.BlockSpec((None, tl, SWA_WIDTH), lambda bi, i: (bi, i, 0)),
                  prev(kcol), cur(kcol), prev(vcol), cur(vcol),
                  pl.BlockSpec(bias_t.shape, lambda bi, i: (0, 0, 0, 0)),
                  pl.BlockSpec(sink_t.shape, lambda bi, i: (0, 0, 0))],
        out_specs=pl.BlockSpec((None, tl, SWA_WIDTH), lambda bi, i: (bi, i, 0)),
        out_shape=jax.ShapeDtypeStruct((b, l, SWA_WIDTH), BF16),
        compiler_params=_cparams(("parallel", "arbitrary")),
        name="swa_prompt",
    )(z, z, z, z, z, bias_t, sink_t)


def _attn_sample_kernel(sink_ref, q_ref, kp_ref, kc_ref, vp_ref, vc_ref, bias_ref, o_ref):
    q = (q_ref[...] * (HEAD_DIM ** -0.5)).astype(BF16)
    k = jnp.concatenate([kp_ref[...], kc_ref[...]], axis=0)
    v = jnp.concatenate([vp_ref[...], vc_ref[...]], axis=0)
    lk = k.shape[0]
    lo = lax.broadcasted_iota(jnp.int32, (lk, LANES), 1) < HALF
    qlane = lax.broadcasted_iota(jnp.int32, (q.shape[0], LANES), 1)
    for t in range(KV_WIDTH // LANES):
        kt = k[:, t * LANES:(t + 1) * LANES]
        vt = v[:, t * LANES:(t + 1) * LANES]
        kt_r = pltpu.roll(kt, HALF, 1)
        vt_r = pltpu.roll(vt, HALF, 1)
        for half in range(2):
            kv = 2 * t + half
            k_in_lo, k_in_hi = (kt, kt_r) if half == 0 else (kt_r, kt)
            v_in_lo, v_in_hi = (vt, vt_r) if half == 0 else (vt_r, vt)
            k_half = (jnp.where(lo, k_in_lo, 0.0).astype(BF16), jnp.where(lo, 0.0, k_in_hi).astype(BF16))
            v_half = (jnp.where(lo, v_in_lo, 0.0).astype(BF16), jnp.where(lo, 0.0, v_in_hi).astype(BF16))
            for qt in (2 * kv, 2 * kv + 1):
                q2 = q[:, qt * LANES:(qt + 1) * LANES]
                acc = None
                inv = []
                for hh in range(2):
                    h = 2 * qt + hh
                    s = _nt(q2, k_half[hh]) + bias_ref[h]
                    sink = sink_ref[h]
                    m = jnp.maximum(jnp.max(s, axis=-1, keepdims=True), sink)
                    e = jnp.exp(s - m)
                    denom = jnp.sum(e, axis=-1, keepdims=True) + jnp.exp(sink - m)
                    inv.append(1.0 / denom)
                    pv = jnp.dot(e.astype(BF16), v_half[hh], preferred_element_type=F32)
                    acc = pv if acc is None else acc + pv
                o2 = acc * jnp.where(qlane < HALF, inv[0], inv[1])
                o_ref[:, qt * LANES:(qt + 1) * LANES] = o2.astype(o_ref.dtype)


def _attn_sample(z, cache_k, cache_v, bias, sinks, li):
    b, tq, _ = z.shape
    kcol, vcol = OFF_K // KV_WIDTH, OFF_V // KV_WIDTH
    cache = pl.BlockSpec((None, None, WINDOW, KV_WIDTH), lambda bi: (li, bi, 0, 0))
    return pl.pallas_call(
        _attn_sample_kernel,
        grid=(b,),
        in_specs=[pl.BlockSpec(memory_space=pltpu.SMEM),
                  pl.BlockSpec((None, tq, SWA_WIDTH), lambda bi: (bi, 0, 0)),
                  cache,
                  pl.BlockSpec((None, tq, KV_WIDTH), lambda bi: (bi, 0, kcol)),
                  cache,
                  pl.BlockSpec((None, tq, KV_WIDTH), lambda bi: (bi, 0, vcol)),
                  pl.BlockSpec(bias.shape, lambda bi: (0, 0, 0))],
        out_specs=pl.BlockSpec((None, tq, SWA_WIDTH), lambda bi: (bi, 0, 0)),
        out_shape=jax.ShapeDtypeStruct((b, tq, SWA_WIDTH), BF16),
        compiler_params=_cparams(("parallel",)),
        name="swa_sample",
    )(sinks, z, cache_k, z, cache_v, z, bias)


def _gmlp_kernel(bu_ref, bv_ref, w_ref, bs_ref, gv_ref, o_ref, *maybe_vn_ref, c, n_chunks):
    row = lax.broadcasted_iota(jnp.int32, (c, c), 0)
    col = lax.broadcasted_iota(jnp.int32, (c, c), 1)
    causal = (row // CHUNK) >= (col // CHUNK)
    v = jax.nn.gelu(bv_ref[...])
    vc = v - jnp.mean(v, axis=-1, keepdims=True)
    vn = (vc * lax.rsqrt(jnp.mean(vc * vc, axis=-1, keepdims=True) + NORM_EPS)) * gv_ref[...]
    if maybe_vn_ref:
        maybe_vn_ref[0][...] = vn
    u = jax.nn.gelu(bu_ref[...])
    vn16 = vn.astype(BF16)
    for g in range(GMLP_GROUPS):
        wg = jnp.where(causal, w_ref[g], 0.0).astype(BF16)
        bias = bs_ref[:, g:g + 1]
        cols = slice(g * LANES, (g + 1) * LANES)
        for ci in range(n_chunks):
            rows = slice(ci * c, (ci + 1) * c)
            sv = jnp.dot(wg, vn16[rows, cols], preferred_element_type=F32) + bias
            o_ref[rows, cols] = (u[rows, cols] * sv).astype(o_ref.dtype)


def _gmlp(z, w_s, b_st, g_vnorm, li, c, tl, want_vn):
    b, l, _ = z.shape
    ucol, vcol = OFF_BU // GMLP_WIDTH, OFF_BV // GMLP_WIDTH
    out_shape = [jax.ShapeDtypeStruct((b, l, GMLP_WIDTH), BF16)]
    out_specs = [pl.BlockSpec((None, tl, GMLP_WIDTH), lambda bi, i: (bi, i, 0))]
    if want_vn:
        out_shape.append(jax.ShapeDtypeStruct((b, l, GMLP_WIDTH), F32))
        out_specs.append(pl.BlockSpec((None, tl, GMLP_WIDTH), lambda bi, i: (bi, i, 0)))
    return pl.pallas_call(
        functools.partial(_gmlp_kernel, c=c, n_chunks=tl // c),
        grid=(b, l // tl),
        in_specs=[pl.BlockSpec((None, tl, GMLP_WIDTH), lambda bi, i: (bi, i, ucol)),
                  pl.BlockSpec((None, tl, GMLP_WIDTH), lambda bi, i: (bi, i, vcol)),
                  pl.BlockSpec((None, GMLP_GROUPS, c, c), lambda bi, i: (li, 0, 0, 0)),
                  pl.BlockSpec((None, c, GMLP_GROUPS), lambda bi, i: (li, 0, 0)),
                  pl.BlockSpec((None, 1, GMLP_WIDTH), lambda bi, i: (li, 0, 0))],
        out_specs=out_specs,
        out_shape=out_shape,
        compiler_params=_cparams(("parallel", "parallel")),
        name="gmlp",
    )(z, z, w_s, b_st, g_vnorm)


def _log_sigmoid(x):
    return jnp.minimum(x, 0.0) - jnp.log1p(jnp.exp(-jnp.abs(x)))


def _cumsum_rows(x):
    n = x.shape[0]
    row = lax.broadcasted_iota(jnp.int32, x.shape, 0)
    shift = 1
    while shift < n:
        x = x + jnp.where(row >= shift, pltpu.roll(x, shift, 0), 0.0)
        shift *= 2
    return x


def _gla_specs(tl, li):
    cq, ck = OFF_CQ // KV_WIDTH, OFF_CK // KV_WIDTH
    cv, co = OFF_CV // GLA_WIDTH, OFF_CO // GLA_WIDTH
    zrow = lambda width, col: pl.BlockSpec((None, tl, width), lambda bi, i: (bi, i, col))
    return [zrow(KV_WIDTH, cq), zrow(KV_WIDTH, ck), zrow(GLA_WIDTH, cv), zrow(GLA_WIDTH, co), zrow(CG_PAD, 0),
            pl.BlockSpec((None, CG_PAD, KV_WIDTH), lambda bi, i: (li, 0, 0)),
            pl.BlockSpec((None, 1, KV_WIDTH), lambda bi, i: (li, 0, 0)),
            pl.BlockSpec((None, 1, GLA_WIDTH), lambda bi, i: (li, 0, 0))]


def _gla_sample_kernel(cq_ref, ck_ref, cv_ref, co_ref, cg_ref, wg_ref, bg_ref, gon_ref, s0_ref,
                       o_ref, sfin_ref):
    c = cq_ref.shape[0]
    lane = lax.broadcasted_iota(jnp.int32, (c, LANES), 1)
    lo = lane < HALF
    row = lax.broadcasted_iota(jnp.int32, (c, c), 0)
    col = lax.broadcasted_iota(jnp.int32, (c, c), 1)
    tril = row >= col
    gate = jnp.dot(cg_ref[...].astype(BF16), wg_ref[...], preferred_element_type=F32) + bg_ref[...]
    la = _log_sigmoid(gate) / GLA_GATE_TEMP
    bcum = _cumsum_rows(la)
    blast = bcum[c - 1:c, :]
    q = cq_ref[...] * (GLA_DK ** -0.5)
    k = ck_ref[...]
    qd = q * jnp.exp(bcum)
    ref = bcum[0:1, :]
    qr = q * jnp.exp(bcum - ref)
    kd = k * jnp.exp(ref - bcum)
    kl = k * jnp.exp(blast - bcum)
    dec = jnp.exp(blast)
    for j in range(2):
        tile = slice(j * LANES, (j + 1) * LANES)
        qd_t, qr_t, kd_t, kl_t = qd[:, tile], qr[:, tile], kd[:, tile], kl[:, tile]
        st = s0_ref[j].T
        st16 = st.astype(BF16)
        st_new = st * dec[:, tile]
        for hh in range(2):
            h = 2 * j + hh
            keep = lo if hh == 0 else jnp.logical_not(lo)
            hc = slice(h * GLA_DV, (h + 1) * GLA_DV)
            v_h = cv_ref[:, hc].astype(BF16)
            att = _nt(qr_t.astype(BF16), jnp.where(keep, kd_t, 0.0).astype(BF16))
            att = jnp.where(tril, att, 0.0)
            o_h = jnp.dot(att.astype(BF16), v_h, preferred_element_type=F32)
            o_h = o_h + _nt(jnp.where(keep, qd_t, 0.0).astype(BF16), st16)
            st_new = st_new + _tn(v_h, jnp.where(keep, kl_t, 0.0).astype(BF16))
            o_n = _rms(o_h, gon_ref[:, hc])
            o_ref[:, hc] = (o_n * jax.nn.silu(co_ref[:, hc])).astype(o_ref.dtype)
        sfin_ref[j] = st_new.T


def _gla_sample(z, zcg, w_gate, b_gate, g_onorm, state, li):
    b, tl, _ = z.shape
    st_spec = pl.BlockSpec((None, 2, LANES, GLA_DV), lambda bi, i: (bi, 0, 0, 0))
    return pl.pallas_call(
        _gla_sample_kernel,
        grid=(b, 1),
        in_specs=_gla_specs(tl, li) + [pl.BlockSpec((None, None, 2, LANES, GLA_DV),
                                                    lambda bi, i: (li, bi, 0, 0, 0))],
        out_specs=[pl.BlockSpec((None, tl, GLA_WIDTH), lambda bi, i: (bi, i, 0)), st_spec],
        out_shape=[jax.ShapeDtypeStruct((b, tl, GLA_WIDTH), BF16),
                   jax.ShapeDtypeStruct((b, 2, LANES, GLA_DV), F32)],
        compiler_params=_cparams(("parallel", "arbitrary")),
        name="gla_sample",
    )(z, z, z, z, zcg, w_gate, b_gate, g_onorm, state)


def _gla_prompt_kernel(cq_ref, ck_ref, cv_ref, co_ref, cg_ref, wg_ref, bg_ref, gon_ref,
                       o_ref, sfin_ref, st_ref):
    tl = cq_ref.shape[0]
    n_chunks = tl // CHUNK

    @pl.when(pl.program_id(1) == 0)
    def _():
        st_ref[...] = jnp.zeros_like(st_ref)

    gate = jnp.dot(cg_ref[...].astype(BF16), wg_ref[...], preferred_element_type=F32) + bg_ref[...]
    la = _log_sigmoid(gate) / GLA_GATE_TEMP
    la_hi = la.astype(BF16)
    la_lo = (la - la_hi.astype(F32)).astype(BF16)
    row = lax.broadcasted_iota(jnp.int32, (GLA_SUPER, GLA_SUPER), 0)
    col = lax.broadcasted_iota(jnp.int32, (GLA_SUPER, GLA_SUPER), 1)
    tri = ((row // CHUNK) == (col // CHUNK)) & (row >= col)
    tri16 = jnp.where(tri, 1.0, 0.0).astype(BF16)
    bcum = jnp.concatenate(
        [jnp.dot(tri16, la_hi[s * GLA_SUPER:(s + 1) * GLA_SUPER], preferred_element_type=F32)
         + jnp.dot(tri16, la_lo[s * GLA_SUPER:(s + 1) * GLA_SUPER], preferred_element_type=F32)
         for s in range(tl // GLA_SUPER)], axis=0)
    blast = [bcum[(ci + 1) * CHUNK - 1:(ci + 1) * CHUNK, :] for ci in range(n_chunks)]
    blast_b = jnp.concatenate([jnp.broadcast_to(bl, (CHUNK, KV_WIDTH)) for bl in blast], axis=0)
    k = ck_ref[...]
    q = cq_ref[...] * (GLA_DK ** -0.5)
    qd = q * jnp.exp(bcum)
    kl = k * jnp.exp(blast_b - bcum)
    ref = jnp.concatenate([jnp.broadcast_to(bcum[r0:r0 + 1], (GLA_SUB, KV_WIDTH))
                           for r0 in range(0, tl, GLA_SUB)], axis=0)
    q_same = q * jnp.exp(bcum - ref)
    k_same = k * jnp.exp(ref - bcum)
    sub_idx = (lax.broadcasted_iota(jnp.int32, (tl, KV_WIDTH), 0) % CHUNK) // GLA_SUB
    q_cross, k_cross = [], []
    for jj in range(CHUNK // GLA_SUB - 1):
        r = jnp.concatenate([jnp.broadcast_to(bcum[ci * CHUNK + (jj + 1) * GLA_SUB:ci * CHUNK + (jj + 1) * GLA_SUB + 1],
                                              (CHUNK, KV_WIDTH)) for ci in range(n_chunks)], axis=0)
        q_cross.append(q * jnp.exp(jnp.where(sub_idx > jj, bcum - r, MASK_VALUE)))
        k_cross.append(k * jnp.exp(jnp.where(sub_idx == jj, r - bcum, MASK_VALUE)))
    same_sub = (row // GLA_SUB) == (col // GLA_SUB)
    lo = lax.broadcasted_iota(jnp.int32, (tl, LANES), 1) < HALF

    for j in range(2):
        tile = slice(j * LANES, (j + 1) * LANES)
        qd_t, kl_t = qd[:, tile], kl[:, tile]
        hcs = [slice((2 * j + hh) * GLA_DV, (2 * j + hh + 1) * GLA_DV) for hh in range(2)]
        keeps = [lo, jnp.logical_not(lo)]
        v = [cv_ref[:, hc].astype(BF16) for hc in hcs]
        qm = [jnp.where(kp, qd_t, 0.0).astype(BF16) for kp in keeps]
        klm = [jnp.where(kp, kl_t, 0.0).astype(BF16) for kp in keeps]
        q_same16 = q_same[:, tile].astype(BF16)
        k_same16 = [jnp.where(kp, k_same[:, tile], 0.0).astype(BF16) for kp in keeps]
        q_cross16 = jnp.concatenate([qc[:, tile] for qc in q_cross], axis=1).astype(BF16)
        k_cross16 = [jnp.concatenate([jnp.where(kp, kc[:, tile], 0.0) for kc in k_cross], axis=1).astype(BF16)
                     for kp in keeps]
        chunk_rows = [slice(ci * CHUNK, (ci + 1) * CHUNK) for ci in range(n_chunks)]
        stack = lambda pair, rows: jnp.concatenate([pair[0][rows], pair[1][rows]], axis=0)
        upd = [_tn(stack(v, rows), stack(klm, rows)) for rows in chunk_rows]
        st = st_ref[j]
        st16 = []
        for ci in range(n_chunks):
            st16.append(st.astype(BF16))
            st = st * jnp.exp(blast[ci][:, tile]) + upd[ci]
        st_ref[j] = st
        inter = [_nt(stack(qm, rows), st16[ci]) for ci, rows in enumerate(chunk_rows)]
        intra = [[], []]
        for s in range(tl // GLA_SUPER):
            rows = slice(s * GLA_SUPER, (s + 1) * GLA_SUPER)
            att_same = _nt(q_same16[rows], jnp.concatenate([k_same16[0][rows], k_same16[1][rows]], axis=0))
            att_cross = _nt(q_cross16[rows], jnp.concatenate([k_cross16[0][rows], k_cross16[1][rows]], axis=0))
            for hh in range(2):
                cols = slice(hh * GLA_SUPER, (hh + 1) * GLA_SUPER)
                a_h = jnp.where(tri, jnp.where(same_sub, att_same[:, cols], att_cross[:, cols]), 0.0).astype(BF16)
                intra[hh].append(jnp.dot(a_h, v[hh][rows], preferred_element_type=F32))
        for hh in range(2):
            o_h = jnp.concatenate(intra[hh], axis=0) + jnp.concatenate(
                [blk[hh * CHUNK:(hh + 1) * CHUNK] for blk in inter], axis=0)
            o_n = _rms(o_h, gon_ref[:, hcs[hh]])
            o_ref[:, hcs[hh]] = (o_n * jax.nn.silu(co_ref[:, hcs[hh]])).astype(o_ref.dtype)

    @pl.when(pl.program_id(1) == pl.num_programs(1) - 1)
    def _():
        for j in range(2):
            sfin_ref[j] = st_ref[j].T


def _gla_prompt(z, zcg, w_gate, b_gate, g_onorm, li, tl):
    b, l, _ = z.shape
    return pl.pallas_call(
        _gla_prompt_kernel,
        grid=(b, l // tl),
        in_specs=_gla_specs(tl, li),
        out_specs=[pl.BlockSpec((None, tl, GLA_WIDTH), lambda bi, i: (bi, i, 0)),
                   pl.BlockSpec((None, 2, LANES, GLA_DV), lambda bi, i: (bi, 0, 0, 0))],
        out_shape=[jax.ShapeDtypeStruct((b, l, GLA_WIDTH), BF16),
                   jax.ShapeDtypeStruct((b, 2, LANES, GLA_DV), F32)],
        scratch_shapes=[pltpu.VMEM((2, GLA_DV, LANES), F32)],
        compiler_params=_cparams(("parallel", "arbitrary")),
        name="gla_prompt",
    )(z, z, z, z, zcg, w_gate, b_gate, g_onorm)


def _pipelined(nsub, matmul_stage, vector_stage):
    nxt = matmul_stage(0)
    for r in range(nsub):
        cur = nxt
        if r + 1 < nsub:
            nxt = matmul_stage(r + 1)
        vector_stage(r, cur)


def _outproj_kernel(x_ref, oa_ref, ob_ref, oc_ref, w_ref, g_ref, g2_ref, x1_ref, h2_ref, *, nsub):
    sub = x_ref.shape[0] // nsub

    def matmul_stage(r):
        rows = slice(r * sub, (r + 1) * sub)
        mix = jnp.dot(oa_ref[rows, :], w_ref[0:SWA_WIDTH, :], preferred_element_type=F32)
        mix = mix + jnp.dot(ob_ref[rows, :], w_ref[SWA_WIDTH:SWA_WIDTH + GMLP_WIDTH, :], preferred_element_type=F32)
        return mix + jnp.dot(oc_ref[rows, :], w_ref[SWA_WIDTH + GMLP_WIDTH:, :], preferred_element_type=F32)

    def vector_stage(r, mix):
        rows = slice(r * sub, (r + 1) * sub)
        x1 = x_ref[rows, :] + _rms(mix, g_ref[...])
        x1_ref[rows, :] = x1
        h2_ref[rows, :] = _rms(x1, g2_ref[...]).astype(BF16)

    _pipelined(nsub, matmul_stage, vector_stage)


def _outproj(x, oa, ob, oc, w, g, g2, li, tm, nsub):
    m = x.shape[0]
    row = lambda width: pl.BlockSpec((tm, width), lambda i: (i, 0))
    gain = pl.BlockSpec((None, 1, D_MODEL), lambda i: (li, 0, 0))
    return pl.pallas_call(
        functools.partial(_outproj_kernel, nsub=nsub),
        grid=(m // tm,),
        in_specs=[row(D_MODEL), row(SWA_WIDTH), row(GMLP_WIDTH), row(GLA_WIDTH),
                  pl.BlockSpec((None, D_MODEL, D_MODEL), lambda i: (li, 0, 0)), gain, gain],
        out_specs=[row(D_MODEL), row(D_MODEL)],
        out_shape=[jax.ShapeDtypeStruct((m, D_MODEL), F32), jax.ShapeDtypeStruct((m, D_MODEL), BF16)],
        compiler_params=_cparams(("parallel",)),
        name="outproj",
    )(x, oa, ob, oc, w, g, g2)


def _ffn_kernel(h_ref, wup_ref, wdn_ref, f_ref):
    @pl.when(pl.program_id(1) == 0)
    def _():
        f_ref[...] = jnp.zeros_like(f_ref)

    u = jnp.maximum(jnp.dot(h_ref[...], wup_ref[...], preferred_element_type=F32), 0.0)
    f_ref[...] += jnp.dot((u * u).astype(BF16), wdn_ref[...], preferred_element_type=F32)


def _ffn(h, wup, wdn, li, tm, tk):
    m = h.shape[0]
    return pl.pallas_call(
        _ffn_kernel,
        grid=(m // tm, D_FF // tk),
        in_specs=[pl.BlockSpec((tm, D_MODEL), lambda i, k: (i, 0)),
                  pl.BlockSpec((None, D_MODEL, tk), lambda i, k: (li, 0, k)),
                  pl.BlockSpec((None, tk, D_MODEL), lambda i, k: (li, k, 0))],
        out_specs=pl.BlockSpec((tm, D_MODEL), lambda i, k: (i, 0)),
        out_shape=jax.ShapeDtypeStruct((m, D_MODEL), F32),
        compiler_params=_cparams(("parallel", "arbitrary"), FFN_VMEM_LIMIT),
        name="ffn",
    )(h, wup, wdn)


def _ple_kernel(x1_ref, f_ref, pe_ref, gpost_ref, wp_ref, wg_ref, gnext_ref, x3_ref, *maybe_hn_ref, nsub):
    sub = x1_ref.shape[0] // nsub

    def matmul_stage(r):
        rows = slice(r * sub, (r + 1) * sub)
        x2 = x1_ref[rows, :] + _rms(f_ref[rows, :], gpost_ref[...])
        gate = jnp.dot(x2.astype(BF16), wg_ref[...], preferred_element_type=F32)
        proj = jnp.dot(pe_ref[rows, :].astype(BF16), wp_ref[...], preferred_element_type=F32)
        return x2, gate, proj

    def vector_stage(r, staged):
        rows = slice(r * sub, (r + 1) * sub)
        x2, gate, proj = staged
        x3 = x2 + proj * jax.nn.sigmoid(gate)
        x3_ref[rows, :] = x3
        if maybe_hn_ref:
            maybe_hn_ref[0][rows, :] = _rms(x3, gnext_ref[...]).astype(BF16)

    _pipelined(nsub, matmul_stage, vector_stage)


def _ple(x1, f, pe, gpost, wp, wg, gnext, li, tm, nsub, want_h):
    m = x1.shape[0]
    row = lambda: pl.BlockSpec((tm, D_MODEL), lambda i: (i, 0))
    ln = min(li + 1, DEPTH - 1)
    out_specs = [row()]
    out_shape = [jax.ShapeDtypeStruct((m, D_MODEL), F32)]
    if want_h:
        out_specs.append(row())
        out_shape.append(jax.ShapeDtypeStruct((m, D_MODEL), BF16))
    return pl.pallas_call(
        functools.partial(_ple_kernel, nsub=nsub),
        grid=(m // tm,),
        in_specs=[row(), row(),
                  pl.BlockSpec((None, tm, PLE_DIM), lambda i: (li, i, 0)),
                  pl.BlockSpec((None, 1, D_MODEL), lambda i: (li, 0, 0)),
                  pl.BlockSpec((None, PLE_DIM, D_MODEL), lambda i: (li, 0, 0)),
                  pl.BlockSpec((None, D_MODEL, D_MODEL), lambda i: (li, 0, 0)),
                  pl.BlockSpec((None, 1, D_MODEL), lambda i: (ln, 0, 0))],
        out_specs=out_specs,
        out_shape=out_shape,
        compiler_params=_cparams(("parallel",)),
        name="ple",
    )(x1, f, pe, gpost, wp, wg, gnext)


def _t5_bucket(rel):
    half = REL_BUCKETS // 2
    max_exact = half // 2
    n = -rel
    ret = jnp.where(n < 0, half, 0)
    n = jnp.abs(n)
    nf = jnp.maximum(n, 1).astype(jnp.float32)
    large = max_exact + (jnp.log(nf / max_exact) / math.log(REL_MAX_DIST / max_exact)
                         * (half - max_exact)).astype(jnp.int32)
    large = jnp.minimum(large, half - 1)
    return ret + jnp.where(n < max_exact, n, large)


def _bias_kernel(tab_ref, bkt_p_ref, valid_p_ref, bkt_s_ref, out_p_ref, out_s_ref):
    bkt_p = bkt_p_ref[...]
    bkt_s = bkt_s_ref[...]
    for h in range(SWA_Q_HEADS):
        def pick(b, accs):
            val = tab_ref[b, h]
            return (jnp.where(bkt_p == b, val, accs[0]), jnp.where(bkt_s == b, val, accs[1]))

        acc_p, acc_s = lax.fori_loop(0, REL_BUCKETS, pick,
                                     (jnp.zeros(bkt_p.shape, F32), jnp.zeros(bkt_s.shape, F32)))
        kv, hq = divmod(h, SWA_GROUP)
        for var in range(2):
            out_p_ref[var, kv, :, hq * LANES:(hq + 1) * LANES] = jnp.where(valid_p_ref[var] != 0, acc_p * LOG2E, MASK_VALUE)
        out_s_ref[h] = acc_s


def _bias_tables(rel_bias, dec_seq):
    qpos = jnp.arange(WINDOW)[None, :]
    kpos = jnp.arange(-WINDOW, WINDOW)[:, None]
    bkt_p = _t5_bucket(kpos - qpos).astype(jnp.int32)
    kchunk = jnp.floor_divide(kpos, CHUNK)
    qchunk = qpos // CHUNK
    valid = (kchunk >= qchunk - WINDOW // CHUNK) & (kchunk <= qchunk)
    valid_p = jnp.stack([valid & (kpos >= 0), valid]).astype(jnp.int32)
    bkt_s = _t5_bucket(jnp.arange(-WINDOW, dec_seq)[None, :] - jnp.arange(dec_seq)[:, None]).astype(jnp.int32)
    full = lambda a: pl.BlockSpec(a.shape, lambda: (0,) * a.ndim)
    out_shape = [jax.ShapeDtypeStruct((2, SWA_KV_HEADS, 2 * WINDOW, SWA_GROUP * LANES), F32),
                 jax.ShapeDtypeStruct((SWA_Q_HEADS, dec_seq, WINDOW + dec_seq), F32)]
    return pl.pallas_call(
        _bias_kernel,
        in_specs=[pl.BlockSpec(memory_space=pltpu.SMEM), full(bkt_p), full(valid_p), full(bkt_s)],
        out_specs=[full(s) for s in out_shape],
        out_shape=out_shape,
        name="rel_bias",
    )(rel_bias, bkt_p, valid_p, bkt_s)


def _prep_w_in(w):
    real = OFF_CO + GLA_GATE_RANK
    pad = jnp.zeros(w.shape[:2] + (CG_PAD - GLA_GATE_RANK,), w.dtype)
    main = jnp.concatenate([w[..., :OFF_CO], w[..., real:]], axis=-1).astype(BF16)
    return main, jnp.concatenate([w[..., OFF_CO:real], pad], axis=-1).astype(BF16)


def kernel(x_prompt, x_sample, cache_swa_k, cache_swa_v, state_gla, p_prompt, p_sample, w_in, w_gate, b_gate, rel_bias, attn_sinks, w_spatial, b_spatial, g_gmlp_vnorm, g_gla_onorm, w_out, g_mix_pre, g_mix_post, g_ffn_pre, g_ffn_post, w_up, w_down, w_ple, w_ple_gate):
    bp, lp, _ = x_prompt.shape
    bs, ls, _ = x_sample.shape
    mp, ms = bp * lp, bs * ls
    row3 = lambda a: a.reshape(DEPTH, 1, -1)

    bias_p, bias_s = _bias_tables(rel_bias, ls)
    sink_t = jnp.repeat(attn_sinks * LOG2E, LANES, axis=-1).reshape(DEPTH, SWA_KV_HEADS, 1, SWA_GROUP * LANES)
    sinks_flat = attn_sinks.reshape(-1)

    w_in16, w_cg16 = _prep_w_in(w_in)
    w_out16, w_up16, w_dn16 = w_out.astype(BF16), w_up.astype(BF16), w_down.astype(BF16)
    w_ple16, w_plg16 = w_ple.astype(BF16), w_ple_gate.astype(BF16)
    wg16 = jnp.concatenate([w_gate, jnp.zeros((DEPTH, CG_PAD - GLA_GATE_RANK, KV_WIDTH), F32)], axis=1).astype(BF16)
    g_pre, g_post, g_fpre, g_fpost = row3(g_mix_pre), row3(g_mix_post), row3(g_ffn_pre), row3(g_ffn_post)
    bg, gon, gvn = row3(b_gate), row3(g_gla_onorm), row3(g_gmlp_vnorm)
    b_st = jnp.swapaxes(b_spatial, 1, 2)
    w_s_small, b_st_small = w_spatial[:, :, :ls, :ls], b_st[:, :ls, :]
    pe_p = p_prompt.reshape(DEPTH, mp, PLE_DIM)
    pe_s = p_sample.reshape(DEPTH, ms, PLE_DIM)
    cache_k = cache_swa_k.reshape(DEPTH, bs, WINDOW, KV_WIDTH)
    cache_v = cache_swa_v.reshape(DEPTH, bs, WINDOW, KV_WIDTH)
    state = state_gla.reshape(DEPTH, bs, 2, LANES, GLA_DV)

    def dense_tail(x, oa, ob, oc, pe, li, tm, pipelined):
        m = x.shape[0]
        ts = min(tm, DENSE_ROWS)
        x1, h2 = _outproj(x, oa.reshape(m, -1), ob.reshape(m, -1), oc.reshape(m, -1), w_out16, g_post, g_fpre,
                          li, ts, OUTPROJ_SUB if pipelined else 1)
        f = _ffn(h2, w_up16, w_dn16, li, tm, FFN_CHUNK)
        out = _ple(x1, f, pe, g_fpost, w_ple16, w_plg16, g_pre, li, ts, PLE_SUB if pipelined else 1, li + 1 < DEPTH)
        return out if li + 1 < DEPTH else (out[0], None)

    xp = x_prompt.reshape(mp, D_MODEL)
    xs = x_sample.reshape(ms, D_MODEL)
    hp = _prenorm(xp, g_pre, 0, FFN_ROWS)
    hs = _prenorm(xs, g_pre, 0, ms)
    kp_rows, vp_rows, ks_rows, vs_rows, sp_states, ss_states, gv_rows = [], [], [], [], [], [], []
    for li in range(DEPTH):
        zp, zcg = _inproj(hp, w_in16, w_cg16, li, FFN_ROWS, INPROJ_COLS)
        zp, zcg = zp.reshape(bp, lp, Z_WIDTH), zcg.reshape(bp, lp, CG_PAD)
        oa = _attn_prompt(zp, bias_p, sink_t[li], ATTN_ROWS)
        ob, = _gmlp(zp, w_spatial, b_st, gvn, li, GMLP_CHUNK, 512, False)
        oc, s_p = _gla_prompt(zp, zcg, wg16, bg, gon, li, GLA_ROWS)
        xp, hp = dense_tail(xp, oa, ob, oc, pe_p, li, FFN_ROWS, True)
        kp_rows.append(zp[:, lp - WINDOW:, OFF_K:OFF_K + KV_WIDTH].reshape(bp, WINDOW, SWA_KV_HEADS, HEAD_DIM))
        vp_rows.append(zp[:, lp - WINDOW:, OFF_V:OFF_V + KV_WIDTH].reshape(bp, WINDOW, SWA_KV_HEADS, HEAD_DIM))
        sp_states.append(s_p.reshape(bp, GLA_HEADS, GLA_DK, GLA_DV))

        zs, zcg = _inproj(hs, w_in16, w_cg16, li, ms, INPROJ_COLS)
        zs, zcg = zs.reshape(bs, ls, Z_WIDTH), zcg.reshape(bs, ls, CG_PAD)
        oa = _attn_sample(zs, cache_k, cache_v, bias_s, sinks_flat[li * SWA_Q_HEADS:(li + 1) * SWA_Q_HEADS], li)
        ob, vn_s = _gmlp(zs, w_s_small, b_st_small, gvn, li, ls, ls, True)
        oc, s_s = _gla_sample(zs, zcg, wg16, bg, gon, state, li)
        xs, hs = dense_tail(xs, oa, ob, oc, pe_s, li, ms, False)
        ks_rows.append(zs[:, :, OFF_K:OFF_K + KV_WIDTH].reshape(bs, ls, SWA_KV_HEADS, HEAD_DIM))
        vs_rows.append(zs[:, :, OFF_V:OFF_V + KV_WIDTH].reshape(bs, ls, SWA_KV_HEADS, HEAD_DIM))
        ss_states.append(s_s.reshape(bs, GLA_HEADS, GLA_DK, GLA_DV).astype(state_gla.dtype))
        gv_rows.append(vn_s)

    return (xp.reshape(bp, lp, D_MODEL), xs.reshape(bs, ls, D_MODEL),
            jnp.stack(kp_rows), jnp.stack(vp_rows), jnp.stack(ks_rows), jnp.stack(vs_rows),
            jnp.stack(sp_states), jnp.stack(ss_states), jnp.stack(gv_rows))
```

```python
import functools
import math

import jax
import jax.numpy as jnp
from jax import lax
from jax.experimental import pallas as pl
from jax.experimental.pallas import tpu as pltpu

F32 = jnp.float32
BF16 = jnp.bfloat16

D_MODEL = 2048
DEPTH = 4
CHUNK = 64
HEAD_DIM = 64
SWA_Q_HEADS = 16
SWA_KV_HEADS = 4
SWA_GROUP = SWA_Q_HEADS // SWA_KV_HEADS
SWA_WIDTH = 1024
KV_WIDTH = 256
WINDOW = 128
REL_BUCKETS = 32
REL_MAX_DIST = 128
GMLP_GROUPS = 4
GMLP_WIDTH = 512
GMLP_CHUNK = 128
GLA_HEADS = 4
GLA_DK = 64
GLA_DV = 128
GLA_WIDTH = 512
GLA_GATE_RANK = 16
GLA_GATE_TEMP = 16.0
D_FF = 4 * D_MODEL
PLE_DIM = 256
NORM_EPS = 1e-6
LANES = 128
HALF = LANES // 2
BF16_ROWS = 16

OFF_Q, OFF_K, OFF_V = 0, 1024, 1280
OFF_BU, OFF_BV = 1536, 2048
OFF_CQ, OFF_CK, OFF_CV = 2560, 2816, 3072
Z_WIDTH = 3584
CG_PAD = LANES

LOG2E = math.log2(math.e)
MASK_VALUE = -1e30
VMEM_LIMIT = 52 * 1024 * 1024
FFN_VMEM_LIMIT = 60 * 1024 * 1024

FFN_ROWS = 1024
FFN_CHUNK = 1024
FFN_FIRST_CHUNK = 512
INPROJ_COLS = 1792
DENSE_ROWS = 512
OUTPROJ_SUB = 4
PLE_SUB = 2
ATTN_ROWS = 512
GLA_ROWS = 512
GLA_SUPER = 2 * CHUNK
GLA_SUB = 16


def _cparams(sem, vmem=VMEM_LIMIT):
    return pltpu.CompilerParams(dimension_semantics=sem, vmem_limit_bytes=vmem)


def _rms(x, g):
    return (x * lax.rsqrt(jnp.mean(x * x, axis=-1, keepdims=True) + NORM_EPS)) * g


def _nt(a, b):
    return lax.dot_general(a, b, (((1,), (1,)), ((), ())), preferred_element_type=F32)


def _tn(a, b):
    return lax.dot_general(a, b, (((0,), (0,)), ((), ())), preferred_element_type=F32)


def _prenorm_kernel(x_ref, g_ref, h_ref):
    h_ref[...] = _rms(x_ref[...], g_ref[...]).astype(BF16)


def _prenorm(x, g, li, tm):
    m = x.shape[0]
    return pl.pallas_call(
        _prenorm_kernel,
        grid=(m // tm,),
        in_specs=[pl.BlockSpec((tm, D_MODEL), lambda i: (i, 0)),
                  pl.BlockSpec((None, 1, D_MODEL), lambda i: (li, 0, 0))],
        out_specs=pl.BlockSpec((tm, D_MODEL), lambda i: (i, 0)),
        out_shape=jax.ShapeDtypeStruct((m, D_MODEL), BF16),
        compiler_params=_cparams(("parallel",)),
        name="prenorm",
    )(x, g)


def _inproj_kernel(h_ref, w_ref, wco_ref, wcg_ref, z_ref, zco_ref, zcg_ref):
    z_ref[...] = jnp.dot(h_ref[...], w_ref[...], preferred_element_type=F32)

    @pl.when(pl.program_id(1) == 0)
    def _():
        zco_ref[...] = jnp.dot(h_ref[...], wco_ref[...], preferred_element_type=F32)
        zcg_ref[...] = jnp.dot(h_ref[...], wcg_ref[...], preferred_element_type=F32)


def _inproj(h, w, wco, wcg, li, tm, tn):
    m = h.shape[0]
    extra = lambda width: pl.BlockSpec((None, D_MODEL, width), lambda i, j: (li, 0, 0))
    return pl.pallas_call(
        _inproj_kernel,
        grid=(m // tm, Z_WIDTH // tn),
        in_specs=[pl.BlockSpec((tm, D_MODEL), lambda i, j: (i, 0)),
                  pl.BlockSpec((None, D_MODEL, tn), lambda i, j: (li, 0, j)),
                  extra(GLA_WIDTH), extra(CG_PAD)],
        out_specs=[pl.BlockSpec((tm, tn), lambda i, j: (i, j)),
                   pl.BlockSpec((tm, GLA_WIDTH), lambda i, j: (i, 0)),
                   pl.BlockSpec((tm, CG_PAD), lambda i, j: (i, 0))],
        out_shape=[jax.ShapeDtypeStruct((m, Z_WIDTH), F32), jax.ShapeDtypeStruct((m, GLA_WIDTH), F32),
                   jax.ShapeDtypeStruct((m, CG_PAD), F32)],
        compiler_params=_cparams(("parallel", "arbitrary")),
        name="inproj",
    )(h, w, wco, wcg)


def _attn_prompt_kernel(q_ref, kp_ref, kc_ref, vp_ref, vc_ref, bias_ref, sink_ref, o_ref, *, nsub):
    first = pl.program_id(1) == 0
    kfull = jnp.concatenate([kp_ref[...], kc_ref[...]], axis=0)
    vfull = jnp.concatenate([vp_ref[...], vc_ref[...]], axis=0)
    rows = kfull.shape[0]
    lo_k = lax.broadcasted_iota(jnp.int32, (rows, LANES), 1) < HALF
    lo_q = lax.broadcasted_iota(jnp.int32, (WINDOW, LANES), 1) < HALF
    kdup, v_t = [], []
    for t in range(KV_WIDTH // LANES):
        kt = kfull[:, t * LANES:(t + 1) * LANES]
        kt_r = pltpu.roll(kt, HALF, 1)
        kdup.append(jnp.where(lo_k, kt, kt_r).astype(BF16))
        kdup.append(jnp.where(lo_k, kt_r, kt).astype(BF16))
        vt_t = vfull[:, t * LANES:(t + 1) * LANES].T
        v_t.append(vt_t[:HALF].astype(BF16))
        v_t.append(vt_t[HALF:].astype(BF16))
    ones = jnp.ones((BF16_ROWS, 2 * WINDOW), BF16)

    def scores(j, kv):
        qa = q_ref[j * WINDOW:(j + 1) * WINDOW, (2 * kv) * LANES:(2 * kv + 1) * LANES] * (HEAD_DIM ** -0.5 * LOG2E)
        qb = q_ref[j * WINDOW:(j + 1) * WINDOW, (2 * kv + 1) * LANES:(2 * kv + 2) * LANES] * (HEAD_DIM ** -0.5 * LOG2E)
        qm = jnp.concatenate([jnp.where(lo_q, qa, 0.0), jnp.where(lo_q, 0.0, qa),
                              jnp.where(lo_q, qb, 0.0), jnp.where(lo_q, 0.0, qb)], axis=0).astype(BF16)
        return _nt(kdup[kv][j * WINDOW:(j + 2) * WINDOW], qm)

    steps = [(j, kv) for j in range(nsub) for kv in range(SWA_KV_HEADS)]
    s_next = scores(*steps[0])
    for n, (j, kv) in enumerate(steps):
        s_t = s_next
        if n + 1 < len(steps):
            s_next = scores(*steps[n + 1])
        keys = slice(j * WINDOW, (j + 2) * WINDOW)
        var = jnp.where(first, 0, 1) if j == 0 else 1
        sink = sink_ref[kv]
        e_cols, m_cols = [], []
        for hq in range(SWA_GROUP):
            cols = slice(hq * LANES, (hq + 1) * LANES)
            s_h = s_t[:, cols] + bias_ref[var, kv, :, cols]
            m_h = jnp.maximum(jnp.max(s_h, axis=0, keepdims=True), sink[:, cols])
            e_cols.append(jnp.exp2(s_h - m_h).astype(BF16))
            m_cols.append(m_h)
        e = jnp.concatenate(e_cols, axis=1)
        m = jnp.concatenate(m_cols, axis=1)
        lhs = jnp.concatenate([v_t[kv][:, keys], ones], axis=0)
        r = jnp.dot(lhs, e, preferred_element_type=F32)
        denom = r[HALF:HALF + 1] + jnp.exp2(sink - m)
        o_t = r[:HALF] * (1.0 / denom)
        for qt in range(2):
            pair = jnp.concatenate([o_t[:, (2 * qt) * LANES:(2 * qt + 1) * LANES],
                                    o_t[:, (2 * qt + 1) * LANES:(2 * qt + 2) * LANES]], axis=0)
            col = (2 * kv + qt) * LANES
            o_ref[j * WINDOW:(j + 1) * WINDOW, col:col + LANES] = pair.T.astype(o_ref.dtype)


def _attn_prompt(z, bias_t, sink_t, tl):
    b, l, _ = z.shape
    kcol, vcol = OFF_K // KV_WIDTH, OFF_V // KV_WIDTH
    per = tl // WINDOW
    prev = lambda col: pl.BlockSpec((None, WINDOW, KV_WIDTH),
                                    lambda bi, i: (bi, jnp.maximum(i * per - 1, 0), col))
    cur = lambda col: pl.BlockSpec((None, tl, KV_WIDTH), lambda bi, i: (bi, i, col))
    return pl.pallas_call(
        functools.partial(_attn_prompt_kernel, nsub=per),
        grid=(b, l // tl),
        in_specs=[pl.BlockSpec((None, tl, SWA_WIDTH), lambda bi, i: (bi, i, 0)),
                  prev(kcol), cur(kcol), prev(vcol), cur(vcol),
                  pl.BlockSpec(bias_t.shape, lambda bi, i: (0, 0, 0, 0)),
                  pl.BlockSpec(sink_t.shape, lambda bi, i: (0, 0, 0))],
        out_specs=pl.BlockSpec((None, tl, SWA_WIDTH), lambda bi, i: (bi, i, 0)),
        out_shape=jax.ShapeDtypeStruct((b, l, SWA_WIDTH), BF16),
        compiler_params=_cparams(("parallel", "arbitrary")),
        name="swa_prompt",
    )(z, z, z, z, z, bias_t, sink_t)


def _attn_sample_kernel(sink_ref, q_ref, kp_ref, kc_ref, vp_ref, vc_ref, bias_ref, o_ref):
    q = (q_ref[...] * (HEAD_DIM ** -0.5)).astype(BF16)
    k = jnp.concatenate([kp_ref[...], kc_ref[...]], axis=0)
    v = jnp.concatenate([vp_ref[...], vc_ref[...]], axis=0)
    lk = k.shape[0]
    lo = lax.broadcasted_iota(jnp.int32, (lk, LANES), 1) < HALF
    qlane = lax.broadcasted_iota(jnp.int32, (q.shape[0], LANES), 1)
    for t in range(KV_WIDTH // LANES):
        kt = k[:, t * LANES:(t + 1) * LANES]
        vt = v[:, t * LANES:(t + 1) * LANES]
        kt_r = pltpu.roll(kt, HALF, 1)
        vt_r = pltpu.roll(vt, HALF, 1)
        for half in range(2):
            kv = 2 * t + half
            k_in_lo, k_in_hi = (kt, kt_r) if half == 0 else (kt_r, kt)
            v_in_lo, v_in_hi = (vt, vt_r) if half == 0 else (vt_r, vt)
            k_half = (jnp.where(lo, k_in_lo, 0.0).astype(BF16), jnp.where(lo, 0.0, k_in_hi).astype(BF16))
            v_half = (jnp.where(lo, v_in_lo, 0.0).astype(BF16), jnp.where(lo, 0.0, v_in_hi).astype(BF16))
            for qt in (2 * kv, 2 * kv + 1):
                q2 = q[:, qt * LANES:(qt + 1) * LANES]
                acc = None
                inv = []
                for hh in range(2):
                    h = 2 * qt + hh
                    s = _nt(q2, k_half[hh]) + bias_ref[h]
                    sink = sink_ref[h]
                    m = jnp.maximum(jnp.max(s, axis=-1, keepdims=True), sink)
                    e = jnp.exp(s - m)
                    denom = jnp.sum(e, axis=-1, keepdims=True) + jnp.exp(sink - m)
                    inv.append(1.0 / denom)
                    pv = jnp.dot(e.astype(BF16), v_half[hh], preferred_element_type=F32)
                    acc = pv if acc is None else acc + pv
                o2 = acc * jnp.where(qlane < HALF, inv[0], inv[1])
                o_ref[:, qt * LANES:(qt + 1) * LANES] = o2.astype(o_ref.dtype)


def _attn_sample(z, cache_k, cache_v, bias, sinks, li):
    b, tq, _ = z.shape
    kcol, vcol = OFF_K // KV_WIDTH, OFF_V // KV_WIDTH
    cache = pl.BlockSpec((None, None, WINDOW, KV_WIDTH), lambda bi: (li, bi, 0, 0))
    return pl.pallas_call(
        _attn_sample_kernel,
        grid=(b,),
        in_specs=[pl.BlockSpec(memory_space=pltpu.SMEM),
                  pl.BlockSpec((None, tq, SWA_WIDTH), lambda bi: (bi, 0, 0)),
                  cache,
                  pl.BlockSpec((None, tq, KV_WIDTH), lambda bi: (bi, 0, kcol)),
                  cache,
                  pl.BlockSpec((None, tq, KV_WIDTH), lambda bi: (bi, 0, vcol)),
                  pl.BlockSpec(bias.shape, lambda bi: (0, 0, 0))],
        out_specs=pl.BlockSpec((None, tq, SWA_WIDTH), lambda bi: (bi, 0, 0)),
        out_shape=jax.ShapeDtypeStruct((b, tq, SWA_WIDTH), BF16),
        compiler_params=_cparams(("parallel",)),
        name="swa_sample",
    )(sinks, z, cache_k, z, cache_v, z, bias)


def _gmlp_kernel(bu_ref, bv_ref, w_ref, bs_ref, gv_ref, o_ref, *maybe_vn_ref, c, n_chunks):
    row = lax.broadcasted_iota(jnp.int32, (c, c), 0)
    col = lax.broadcasted_iota(jnp.int32, (c, c), 1)
    causal = (row // CHUNK) >= (col // CHUNK)
    v = jax.nn.gelu(bv_ref[...])
    vc = v - jnp.mean(v, axis=-1, keepdims=True)
    vn = (vc * lax.rsqrt(jnp.mean(vc * vc, axis=-1, keepdims=True) + NORM_EPS)) * gv_ref[...]
    if maybe_vn_ref:
        maybe_vn_ref[0][...] = vn
    u = jax.nn.gelu(bu_ref[...])
    vn16 = vn.astype(BF16)
    for g in range(GMLP_GROUPS):
        wg = jnp.where(causal, w_ref[g], 0.0).astype(BF16)
        bias = bs_ref[:, g:g + 1]
        cols = slice(g * LANES, (g + 1) * LANES)
        for ci in range(n_chunks):
            rows = slice(ci * c, (ci + 1) * c)
            sv = jnp.dot(wg, vn16[rows, cols], preferred_element_type=F32) + bias
            o_ref[rows, cols] = (u[rows, cols] * sv).astype(o_ref.dtype)


def _gmlp(z, w_s, b_st, g_vnorm, li, c, tl, want_vn):
    b, l, _ = z.shape
    ucol, vcol = OFF_BU // GMLP_WIDTH, OFF_BV // GMLP_WIDTH
    out_shape = [jax.ShapeDtypeStruct((b, l, GMLP_WIDTH), BF16)]
    out_specs = [pl.BlockSpec((None, tl, GMLP_WIDTH), lambda bi, i: (bi, i, 0))]
    if want_vn:
        out_shape.append(jax.ShapeDtypeStruct((b, l, GMLP_WIDTH), F32))
        out_specs.append(pl.BlockSpec((None, tl, GMLP_WIDTH), lambda bi, i: (bi, i, 0)))
    return pl.pallas_call(
        functools.partial(_gmlp_kernel, c=c, n_chunks=tl // c),
        grid=(b, l // tl),
        in_specs=[pl.BlockSpec((None, tl, GMLP_WIDTH), lambda bi, i: (bi, i, ucol)),
                  pl.BlockSpec((None, tl, GMLP_WIDTH), lambda bi, i: (bi, i, vcol)),
                  pl.BlockSpec((None, GMLP_GROUPS, c, c), lambda bi, i: (li, 0, 0, 0)),
                  pl.BlockSpec((None, c, GMLP_GROUPS), lambda bi, i: (li, 0, 0)),
                  pl.BlockSpec((None, 1, GMLP_WIDTH), lambda bi, i: (li, 0, 0))],
        out_specs=out_specs,
        out_shape=out_shape,
        compiler_params=_cparams(("parallel", "parallel")),
        name="gmlp",
    )(z, z, w_s, b_st, g_vnorm)


def _log_sigmoid(x):
    return jnp.minimum(x, 0.0) - jnp.log1p(jnp.exp(-jnp.abs(x)))


def _cumsum_rows(x):
    n = x.shape[0]
    row = lax.broadcasted_iota(jnp.int32, x.shape, 0)
    shift = 1
    while shift < n:
        x = x + jnp.where(row >= shift, pltpu.roll(x, shift, 0), 0.0)
        shift *= 2
    return x


def _gla_specs(tl, li):
    cq, ck, cv = OFF_CQ // KV_WIDTH, OFF_CK // KV_WIDTH, OFF_CV // GLA_WIDTH
    zrow = lambda width, col: pl.BlockSpec((None, tl, width), lambda bi, i: (bi, i, col))
    return [zrow(KV_WIDTH, cq), zrow(KV_WIDTH, ck), zrow(GLA_WIDTH, cv), zrow(GLA_WIDTH, 0), zrow(CG_PAD, 0),
            pl.BlockSpec((None, CG_PAD, KV_WIDTH), lambda bi, i: (li, 0, 0)),
            pl.BlockSpec((None, 1, KV_WIDTH), lambda bi, i: (li, 0, 0)),
            pl.BlockSpec((None, 1, GLA_WIDTH), lambda bi, i: (li, 0, 0))]


def _gla_sample_kernel(cq_ref, ck_ref, cv_ref, co_ref, cg_ref, wg_ref, bg_ref, gon_ref, s0_ref,
                       o_ref, sfin_ref):
    c = cq_ref.shape[0]
    lane = lax.broadcasted_iota(jnp.int32, (c, LANES), 1)
    lo = lane < HALF
    row = lax.broadcasted_iota(jnp.int32, (c, c), 0)
    col = lax.broadcasted_iota(jnp.int32, (c, c), 1)
    tril = row >= col
    gate = jnp.dot(cg_ref[...].astype(BF16), wg_ref[...], preferred_element_type=F32) + bg_ref[...]
    la = _log_sigmoid(gate) / GLA_GATE_TEMP
    bcum = _cumsum_rows(la)
    blast = bcum[c - 1:c, :]
    q = cq_ref[...] * (GLA_DK ** -0.5)
    k = ck_ref[...]
    qd = q * jnp.exp(bcum)
    ref = bcum[0:1, :]
    qr = q * jnp.exp(bcum - ref)
    kd = k * jnp.exp(ref - bcum)
    kl = k * jnp.exp(blast - bcum)
    dec = jnp.exp(blast)
    for j in range(2):
        tile = slice(j * LANES, (j + 1) * LANES)
        qd_t, qr_t, kd_t, kl_t = qd[:, tile], qr[:, tile], kd[:, tile], kl[:, tile]
        st = s0_ref[j].T
        st16 = st.astype(BF16)
        st_new = st * dec[:, tile]
        for hh in range(2):
            h = 2 * j + hh
            keep = lo if hh == 0 else jnp.logical_not(lo)
            hc = slice(h * GLA_DV, (h + 1) * GLA_DV)
            v_h = cv_ref[:, hc].astype(BF16)
            att = _nt(qr_t.astype(BF16), jnp.where(keep, kd_t, 0.0).astype(BF16))
            att = jnp.where(tril, att, 0.0)
            o_h = jnp.dot(att.astype(BF16), v_h, preferred_element_type=F32)
            o_h = o_h + _nt(jnp.where(keep, qd_t, 0.0).astype(BF16), st16)
            st_new = st_new + _tn(v_h, jnp.where(keep, kl_t, 0.0).astype(BF16))
            o_n = _rms(o_h, gon_ref[:, hc])
            o_ref[:, hc] = (o_n * jax.nn.silu(co_ref[:, hc])).astype(o_ref.dtype)
        sfin_ref[j] = st_new.T


def _gla_sample(z, zco, zcg, w_gate, b_gate, g_onorm, state, li):
    b, tl, _ = z.shape
    st_spec = pl.BlockSpec((None, 2, LANES, GLA_DV), lambda bi, i: (bi, 0, 0, 0))
    return pl.pallas_call(
        _gla_sample_kernel,
        grid=(b, 1),
        in_specs=_gla_specs(tl, li) + [pl.BlockSpec((None, None, 2, LANES, GLA_DV),
                                                    lambda bi, i: (li, bi, 0, 0, 0))],
        out_specs=[pl.BlockSpec((None, tl, GLA_WIDTH), lambda bi, i: (bi, i, 0)), st_spec],
        out_shape=[jax.ShapeDtypeStruct((b, tl, GLA_WIDTH), BF16),
                   jax.ShapeDtypeStruct((b, 2, LANES, GLA_DV), F32)],
        compiler_params=_cparams(("parallel", "arbitrary")),
        name="gla_sample",
    )(z, z, z, zco, zcg, w_gate, b_gate, g_onorm, state)


def _gla_prompt_kernel(cq_ref, ck_ref, cv_ref, co_ref, cg_ref, wg_ref, bg_ref, gon_ref,
                       o_ref, sfin_ref, st_ref):
    tl = cq_ref.shape[0]
    n_chunks = tl // CHUNK

    @pl.when(pl.program_id(1) == 0)
    def _():
        st_ref[...] = jnp.zeros_like(st_ref)

    gate = jnp.dot(cg_ref[...].astype(BF16), wg_ref[...], preferred_element_type=F32) + bg_ref[...]
    la = _log_sigmoid(gate) / GLA_GATE_TEMP
    la_hi = la.astype(BF16)
    la_lo = (la - la_hi.astype(F32)).astype(BF16)
    row = lax.broadcasted_iota(jnp.int32, (GLA_SUPER, GLA_SUPER), 0)
    col = lax.broadcasted_iota(jnp.int32, (GLA_SUPER, GLA_SUPER), 1)
    tri = ((row // CHUNK) == (col // CHUNK)) & (row >= col)
    tri16 = jnp.where(tri, 1.0, 0.0).astype(BF16)
    bcum = jnp.concatenate(
        [jnp.dot(tri16, la_hi[s * GLA_SUPER:(s + 1) * GLA_SUPER], preferred_element_type=F32)
         + jnp.dot(tri16, la_lo[s * GLA_SUPER:(s + 1) * GLA_SUPER], preferred_element_type=F32)
         for s in range(tl // GLA_SUPER)], axis=0)
    blast = [bcum[(ci + 1) * CHUNK - 1:(ci + 1) * CHUNK, :] for ci in range(n_chunks)]
    blast_b = jnp.concatenate([jnp.broadcast_to(bl, (CHUNK, KV_WIDTH)) for bl in blast], axis=0)
    k = ck_ref[...]
    q = cq_ref[...] * (GLA_DK ** -0.5)
    qd = q * jnp.exp(bcum)
    kl = k * jnp.exp(blast_b - bcum)
    ref = jnp.concatenate([jnp.broadcast_to(bcum[r0:r0 + 1], (GLA_SUB, KV_WIDTH))
                           for r0 in range(0, tl, GLA_SUB)], axis=0)
    q_same = q * jnp.exp(bcum - ref)
    k_same = k * jnp.exp(ref - bcum)
    sub_idx = (lax.broadcasted_iota(jnp.int32, (tl, KV_WIDTH), 0) % CHUNK) // GLA_SUB
    q_cross, k_cross = [], []
    for jj in range(CHUNK // GLA_SUB - 1):
        r = jnp.concatenate([jnp.broadcast_to(bcum[ci * CHUNK + (jj + 1) * GLA_SUB:ci * CHUNK + (jj + 1) * GLA_SUB + 1],
                                              (CHUNK, KV_WIDTH)) for ci in range(n_chunks)], axis=0)
        q_cross.append(q * jnp.exp(jnp.where(sub_idx > jj, bcum - r, MASK_VALUE)))
        k_cross.append(k * jnp.exp(jnp.where(sub_idx == jj, r - bcum, MASK_VALUE)))
    same_sub = (row // GLA_SUB) == (col // GLA_SUB)
    lo = lax.broadcasted_iota(jnp.int32, (tl, LANES), 1) < HALF

    for j in range(2):
        tile = slice(j * LANES, (j + 1) * LANES)
        qd_t, kl_t = qd[:, tile], kl[:, tile]
        hcs = [slice((2 * j + hh) * GLA_DV, (2 * j + hh + 1) * GLA_DV) for hh in range(2)]
        keeps = [lo, jnp.logical_not(lo)]
        v = [cv_ref[:, hc].astype(BF16) for hc in hcs]
        qm = [jnp.where(kp, qd_t, 0.0).astype(BF16) for kp in keeps]
        klm = [jnp.where(kp, kl_t, 0.0).astype(BF16) for kp in keeps]
        q_same16 = q_same[:, tile].astype(BF16)
        k_same16 = [jnp.where(kp, k_same[:, tile], 0.0).astype(BF16) for kp in keeps]
        q_cross16 = jnp.concatenate([qc[:, tile] for qc in q_cross], axis=1).astype(BF16)
        k_cross16 = [jnp.concatenate([jnp.where(kp, kc[:, tile], 0.0) for kc in k_cross], axis=1).astype(BF16)
                     for kp in keeps]
        chunk_rows = [slice(ci * CHUNK, (ci + 1) * CHUNK) for ci in range(n_chunks)]
        stack = lambda pair, rows: jnp.concatenate([pair[0][rows], pair[1][rows]], axis=0)
        upd = [_tn(stack(v, rows), stack(klm, rows)) for rows in chunk_rows]
        st = st_ref[j]
        st16 = []
        for ci in range(n_chunks):
            st16.append(st.astype(BF16))
            st = st * jnp.exp(blast[ci][:, tile]) + upd[ci]
        st_ref[j] = st
        inter = [_nt(stack(qm, rows), st16[ci]) for ci, rows in enumerate(chunk_rows)]
        intra = [[], []]
        for s in range(tl // GLA_SUPER):
            rows = slice(s * GLA_SUPER, (s + 1) * GLA_SUPER)
            att_same = _nt(q_same16[rows], jnp.concatenate([k_same16[0][rows], k_same16[1][rows]], axis=0))
            att_cross = _nt(q_cross16[rows], jnp.concatenate([k_cross16[0][rows], k_cross16[1][rows]], axis=0))
            for hh in range(2):
                cols = slice(hh * GLA_SUPER, (hh + 1) * GLA_SUPER)
                a_h = jnp.where(tri, jnp.where(same_sub, att_same[:, cols], att_cross[:, cols]), 0.0).astype(BF16)
                intra[hh].append(jnp.dot(a_h, v[hh][rows], preferred_element_type=F32))
        for hh in range(2):
            o_h = jnp.concatenate(intra[hh], axis=0) + jnp.concatenate(
                [blk[hh * CHUNK:(hh + 1) * CHUNK] for blk in inter], axis=0)
            o_n = _rms(o_h, gon_ref[:, hcs[hh]])
            o_ref[:, hcs[hh]] = (o_n * jax.nn.silu(co_ref[:, hcs[hh]])).astype(o_ref.dtype)

    @pl.when(pl.program_id(1) == pl.num_programs(1) - 1)
    def _():
        for j in range(2):
            sfin_ref[j] = st_ref[j].T


def _gla_prompt(z, zco, zcg, w_gate, b_gate, g_onorm, li, tl):
    b, l, _ = z.shape
    return pl.pallas_call(
        _gla_prompt_kernel,
        grid=(b, l // tl),
        in_specs=_gla_specs(tl, li),
        out_specs=[pl.BlockSpec((None, tl, GLA_WIDTH), lambda bi, i: (bi, i, 0)),
                   pl.BlockSpec((None, 2, LANES, GLA_DV), lambda bi, i: (bi, 0, 0, 0))],
        out_shape=[jax.ShapeDtypeStruct((b, l, GLA_WIDTH), BF16),
                   jax.ShapeDtypeStruct((b, 2, LANES, GLA_DV), F32)],
        scratch_shapes=[pltpu.VMEM((2, GLA_DV, LANES), F32)],
        compiler_params=_cparams(("parallel", "arbitrary")),
        name="gla_prompt",
    )(z, z, z, zco, zcg, w_gate, b_gate, g_onorm)


def _pipelined(nsub, matmul_stage, vector_stage):
    nxt = matmul_stage(0)
    for r in range(nsub):
        cur = nxt
        if r + 1 < nsub:
            nxt = matmul_stage(r + 1)
        vector_stage(r, cur)


def _outproj_kernel(x_ref, oa_ref, ob_ref, oc_ref, w_ref, g_ref, g2_ref, x1_ref, h2_ref, *, nsub):
    sub = x_ref.shape[0] // nsub

    def matmul_stage(r):
        rows = slice(r * sub, (r + 1) * sub)
        mix = jnp.dot(oa_ref[rows, :], w_ref[0:SWA_WIDTH, :], preferred_element_type=F32)
        mix = mix + jnp.dot(ob_ref[rows, :], w_ref[SWA_WIDTH:SWA_WIDTH + GMLP_WIDTH, :], preferred_element_type=F32)
        return mix + jnp.dot(oc_ref[rows, :], w_ref[SWA_WIDTH + GMLP_WIDTH:, :], preferred_element_type=F32)

    def vector_stage(r, mix):
        rows = slice(r * sub, (r + 1) * sub)
        x1 = x_ref[rows, :] + _rms(mix, g_ref[...])
        x1_ref[rows, :] = x1
        h2_ref[rows, :] = _rms(x1, g2_ref[...]).astype(BF16)

    _pipelined(nsub, matmul_stage, vector_stage)


def _outproj(x, oa, ob, oc, w, g, g2, li, tm, nsub):
    m = x.shape[0]
    row = lambda width: pl.BlockSpec((tm, width), lambda i: (i, 0))
    gain = pl.BlockSpec((None, 1, D_MODEL), lambda i: (li, 0, 0))
    return pl.pallas_call(
        functools.partial(_outproj_kernel, nsub=nsub),
        grid=(m // tm,),
        in_specs=[row(D_MODEL), row(SWA_WIDTH), row(GMLP_WIDTH), row(GLA_WIDTH),
                  pl.BlockSpec((None, D_MODEL, D_MODEL), lambda i: (li, 0, 0)), gain, gain],
        out_specs=[row(D_MODEL), row(D_MODEL)],
        out_shape=[jax.ShapeDtypeStruct((m, D_MODEL), F32), jax.ShapeDtypeStruct((m, D_MODEL), BF16)],
        compiler_params=_cparams(("parallel",)),
        name="outproj",
    )(x, oa, ob, oc, w, g, g2)


def _ffn_zero_first(f_ref):
    @pl.when(pl.program_id(1) == 0)
    def _():
        f_ref[...] = jnp.zeros_like(f_ref)


def _ffn_kernel(h_ref, wup_ref, wdn_ref, f_ref):
    _ffn_zero_first(f_ref)
    u = jnp.maximum(jnp.dot(h_ref[...], wup_ref[...], preferred_element_type=F32), 0.0)
    f_ref[...] += jnp.dot((u * u).astype(BF16), wdn_ref[...], preferred_element_type=F32)


def _ffn_rest_kernel(h_ref, wup_ref, wdn_ref, f_first_ref, f_ref):
    _ffn_kernel(h_ref, wup_ref, wdn_ref, f_ref)


def _ffn_first_kernel(h_ref, wup_ref, wdn_ref, f_ref, wup16_ref, wdn16_ref):
    _ffn_zero_first(f_ref)
    wup16_ref[...] = wup_ref[...].astype(BF16)
    wdn16_ref[...] = wdn_ref[...].astype(BF16)
    u = jnp.maximum(jnp.dot(h_ref[...], wup16_ref[...], preferred_element_type=F32), 0.0)
    f_ref[...] += jnp.dot((u * u).astype(BF16), wdn16_ref[...], preferred_element_type=F32)


def _ffn_specs(tm, tk, row_of):
    return [pl.BlockSpec((tm, D_MODEL), lambda i, k: (row_of(i), 0)),
            pl.BlockSpec((None, D_MODEL, tk), lambda i, k: (0, 0, k)),
            pl.BlockSpec((None, tk, D_MODEL), lambda i, k: (0, k, 0))]


def _ffn_prompt(h, w_up, w_down, li, tm, tk_first, tk):
    m = h.shape[0]
    f_shape = jax.ShapeDtypeStruct((m, D_MODEL), F32)
    f, wup16, wdn16 = pl.pallas_call(
        _ffn_first_kernel,
        grid=(1, D_FF // tk_first),
        in_specs=[pl.BlockSpec((tm, D_MODEL), lambda i, k: (0, 0)),
                  pl.BlockSpec((None, D_MODEL, tk_first), lambda i, k: (li, 0, k)),
                  pl.BlockSpec((None, tk_first, D_MODEL), lambda i, k: (li, k, 0))],
        out_specs=_ffn_specs(tm, tk_first, lambda i: 0),
        out_shape=[f_shape, jax.ShapeDtypeStruct((1, D_MODEL, D_FF), BF16),
                   jax.ShapeDtypeStruct((1, D_FF, D_MODEL), BF16)],
        compiler_params=_cparams(("arbitrary", "arbitrary"), FFN_VMEM_LIMIT),
        name="ffn_first",
    )(h, w_up, w_down)
    f = pl.pallas_call(
        _ffn_rest_kernel,
        grid=(m // tm - 1, D_FF // tk),
        in_specs=_ffn_specs(tm, tk, lambda i: i + 1) + [pl.BlockSpec(memory_space=pl.ANY)],
        out_specs=pl.BlockSpec((tm, D_MODEL), lambda i, k: (i + 1, 0)),
        out_shape=f_shape,
        input_output_aliases={3: 0},
        compiler_params=_cparams(("parallel", "arbitrary"), FFN_VMEM_LIMIT),
        name="ffn_rest",
    )(h, wup16, wdn16, f)
    return f, wup16, wdn16


def _ffn_sample(h, wup16, wdn16, tk):
    m = h.shape[0]
    return pl.pallas_call(
        _ffn_kernel,
        grid=(1, D_FF // tk),
        in_specs=_ffn_specs(m, tk, lambda i: i),
        out_specs=pl.BlockSpec((m, D_MODEL), lambda i, k: (i, 0)),
        out_shape=jax.ShapeDtypeStruct((m, D_MODEL), F32),
        compiler_params=_cparams(("parallel", "arbitrary"), FFN_VMEM_LIMIT),
        name="ffn_sample",
    )(h, wup16, wdn16)


def _ple_kernel(x1_ref, f_ref, pe_ref, gpost_ref, wp_ref, wg_ref, gnext_ref, x3_ref, *maybe_hn_ref, nsub):
    sub = x1_ref.shape[0] // nsub

    def matmul_stage(r):
        rows = slice(r * sub, (r + 1) * sub)
        x2 = x1_ref[rows, :] + _rms(f_ref[rows, :], gpost_ref[...])
        gate = jnp.dot(x2.astype(BF16), wg_ref[...], preferred_element_type=F32)
        proj = jnp.dot(pe_ref[rows, :].astype(BF16), wp_ref[...], preferred_element_type=F32)
        return x2, gate, proj

    def vector_stage(r, staged):
        rows = slice(r * sub, (r + 1) * sub)
        x2, gate, proj = staged
        x3 = x2 + proj * jax.nn.sigmoid(gate)
        x3_ref[rows, :] = x3
        if maybe_hn_ref:
            maybe_hn_ref[0][rows, :] = _rms(x3, gnext_ref[...]).astype(BF16)

    _pipelined(nsub, matmul_stage, vector_stage)


def _ple(x1, f, pe, gpost, wp, wg, gnext, li, tm, nsub, want_h):
    m = x1.shape[0]
    row = lambda: pl.BlockSpec((tm, D_MODEL), lambda i: (i, 0))
    ln = min(li + 1, DEPTH - 1)
    out_specs = [row()]
    out_shape = [jax.ShapeDtypeStruct((m, D_MODEL), F32)]
    if want_h:
        out_specs.append(row())
        out_shape.append(jax.ShapeDtypeStruct((m, D_MODEL), BF16))
    return pl.pallas_call(
        functools.partial(_ple_kernel, nsub=nsub),
        grid=(m // tm,),
        in_specs=[row(), row(),
                  pl.BlockSpec((None, tm, PLE_DIM), lambda i: (li, i, 0)),
                  pl.BlockSpec((None, 1, D_MODEL), lambda i: (li, 0, 0)),
                  pl.BlockSpec((None, PLE_DIM, D_MODEL), lambda i: (li, 0, 0)),
                  pl.BlockSpec((None, D_MODEL, D_MODEL), lambda i: (li, 0, 0)),
                  pl.BlockSpec((None, 1, D_MODEL), lambda i: (ln, 0, 0))],
        out_specs=out_specs,
        out_shape=out_shape,
        compiler_params=_cparams(("parallel",)),
        name="ple",
    )(x1, f, pe, gpost, wp, wg, gnext)


def _t5_bucket(rel):
    half = REL_BUCKETS // 2
    max_exact = half // 2
    n = -rel
    ret = jnp.where(n < 0, half, 0)
    n = jnp.abs(n)
    nf = jnp.maximum(n, 1).astype(jnp.float32)
    large = max_exact + (jnp.log(nf / max_exact) / math.log(REL_MAX_DIST / max_exact)
                         * (half - max_exact)).astype(jnp.int32)
    large = jnp.minimum(large, half - 1)
    return ret + jnp.where(n < max_exact, n, large)


def _bias_kernel(tab_ref, bkt_p_ref, valid_p_ref, bkt_s_ref, out_p_ref, out_s_ref):
    bkt_p = bkt_p_ref[...]
    bkt_s = bkt_s_ref[...]
    for h in range(SWA_Q_HEADS):
        def pick(b, accs):
            val = tab_ref[b, h]
            return (jnp.where(bkt_p == b, val, accs[0]), jnp.where(bkt_s == b, val, accs[1]))

        acc_p, acc_s = lax.fori_loop(0, REL_BUCKETS, pick,
                                     (jnp.zeros(bkt_p.shape, F32), jnp.zeros(bkt_s.shape, F32)))
        kv, hq = divmod(h, SWA_GROUP)
        for var in range(2):
            out_p_ref[var, kv, :, hq * LANES:(hq + 1) * LANES] = jnp.where(valid_p_ref[var] != 0, acc_p * LOG2E, MASK_VALUE)
        out_s_ref[h] = acc_s


def _bias_tables(rel_bias, dec_seq):
    qpos = jnp.arange(WINDOW)[None, :]
    kpos = jnp.arange(-WINDOW, WINDOW)[:, None]
    bkt_p = _t5_bucket(kpos - qpos).astype(jnp.int32)
    kchunk = jnp.floor_divide(kpos, CHUNK)
    qchunk = qpos // CHUNK
    valid = (kchunk >= qchunk - WINDOW // CHUNK) & (kchunk <= qchunk)
    valid_p = jnp.stack([valid & (kpos >= 0), valid]).astype(jnp.int32)
    bkt_s = _t5_bucket(jnp.arange(-WINDOW, dec_seq)[None, :] - jnp.arange(dec_seq)[:, None]).astype(jnp.int32)
    full = lambda a: pl.BlockSpec(a.shape, lambda: (0,) * a.ndim)
    out_shape = [jax.ShapeDtypeStruct((2, SWA_KV_HEADS, 2 * WINDOW, SWA_GROUP * LANES), F32),
                 jax.ShapeDtypeStruct((SWA_Q_HEADS, dec_seq, WINDOW + dec_seq), F32)]
    return pl.pallas_call(
        _bias_kernel,
        in_specs=[pl.BlockSpec(memory_space=pltpu.SMEM), full(bkt_p), full(valid_p), full(bkt_s)],
        out_specs=[full(s) for s in out_shape],
        out_shape=out_shape,
        name="rel_bias",
    )(rel_bias, bkt_p, valid_p, bkt_s)


def _prep_w_in(w):
    co = Z_WIDTH + GLA_GATE_RANK
    pad = jnp.zeros(w.shape[:2] + (CG_PAD - GLA_GATE_RANK,), w.dtype)
    return (w[..., :Z_WIDTH].astype(BF16), w[..., co:].astype(BF16),
            jnp.concatenate([w[..., Z_WIDTH:co], pad], axis=-1).astype(BF16))


def kernel(x_prompt, x_sample, cache_swa_k, cache_swa_v, state_gla, p_prompt, p_sample, w_in, w_gate, b_gate, rel_bias, attn_sinks, w_spatial, b_spatial, g_gmlp_vnorm, g_gla_onorm, w_out, g_mix_pre, g_mix_post, g_ffn_pre, g_ffn_post, w_up, w_down, w_ple, w_ple_gate):
    bp, lp, _ = x_prompt.shape
    bs, ls, _ = x_sample.shape
    mp, ms = bp * lp, bs * ls
    row3 = lambda a: a.reshape(DEPTH, 1, -1)

    bias_p, bias_s = _bias_tables(rel_bias, ls)
    sink_t = jnp.repeat(attn_sinks * LOG2E, LANES, axis=-1).reshape(DEPTH, SWA_KV_HEADS, 1, SWA_GROUP * LANES)
    sinks_flat = attn_sinks.reshape(-1)

    w_in16, w_co16, w_cg16 = _prep_w_in(w_in)
    w_out16, w_ple16, w_plg16 = w_out.astype(BF16), w_ple.astype(BF16), w_ple_gate.astype(BF16)
    wg16 = jnp.concatenate([w_gate, jnp.zeros((DEPTH, CG_PAD - GLA_GATE_RANK, KV_WIDTH), F32)], axis=1).astype(BF16)
    g_pre, g_post, g_fpre, g_fpost = row3(g_mix_pre), row3(g_mix_post), row3(g_ffn_pre), row3(g_ffn_post)
    bg, gon, gvn = row3(b_gate), row3(g_gla_onorm), row3(g_gmlp_vnorm)
    b_st = jnp.swapaxes(b_spatial, 1, 2)
    w_s_small, b_st_small = w_spatial[:, :, :ls, :ls], b_st[:, :ls, :]
    pe_p = p_prompt.reshape(DEPTH, mp, PLE_DIM)
    pe_s = p_sample.reshape(DEPTH, ms, PLE_DIM)
    cache_k = cache_swa_k.reshape(DEPTH, bs, WINDOW, KV_WIDTH)
    cache_v = cache_swa_v.reshape(DEPTH, bs, WINDOW, KV_WIDTH)
    state = state_gla.reshape(DEPTH, bs, 2, LANES, GLA_DV)

    def mix_out(x, oa, ob, oc, li, ts, nsub):
        m = x.shape[0]
        return _outproj(x, oa.reshape(m, -1), ob.reshape(m, -1), oc.reshape(m, -1), w_out16, g_post, g_fpre,
                        li, ts, nsub)

    def layer_out(x1, f, pe, li, ts, nsub):
        out = _ple(x1, f, pe, g_fpost, w_ple16, w_plg16, g_pre, li, ts, nsub, li + 1 < DEPTH)
        return out if li + 1 < DEPTH else (out[0], None)

    xp = x_prompt.reshape(mp, D_MODEL)
    xs = x_sample.reshape(ms, D_MODEL)
    hp = _prenorm(xp, g_pre, 0, FFN_ROWS)
    hs = _prenorm(xs, g_pre, 0, ms)
    kp_rows, vp_rows, ks_rows, vs_rows, sp_states, ss_states, gv_rows = [], [], [], [], [], [], []
    for li in range(DEPTH):
        zp, zco, zcg = _inproj(hp, w_in16, w_co16, w_cg16, li, FFN_ROWS, INPROJ_COLS)
        zp, zco, zcg = zp.reshape(bp, lp, -1), zco.reshape(bp, lp, -1), zcg.reshape(bp, lp, -1)
        oa = _attn_prompt(zp, bias_p, sink_t[li], ATTN_ROWS)
        ob, = _gmlp(zp, w_spatial, b_st, gvn, li, GMLP_CHUNK, 512, False)
        oc, s_p = _gla_prompt(zp, zco, zcg, wg16, bg, gon, li, GLA_ROWS)
        x1, h2 = mix_out(xp, oa, ob, oc, li, DENSE_ROWS, OUTPROJ_SUB)
        f, w_up16, w_dn16 = _ffn_prompt(h2, w_up, w_down, li, FFN_ROWS, FFN_FIRST_CHUNK, FFN_CHUNK)
        xp, hp = layer_out(x1, f, pe_p, li, DENSE_ROWS, PLE_SUB)
        kp_rows.append(zp[:, lp - WINDOW:, OFF_K:OFF_K + KV_WIDTH].reshape(bp, WINDOW, SWA_KV_HEADS, HEAD_DIM))
        vp_rows.append(zp[:, lp - WINDOW:, OFF_V:OFF_V + KV_WIDTH].reshape(bp, WINDOW, SWA_KV_HEADS, HEAD_DIM))
        sp_states.append(s_p.reshape(bp, GLA_HEADS, GLA_DK, GLA_DV))

        zs, zco, zcg = _inproj(hs, w_in16, w_co16, w_cg16, li, ms, INPROJ_COLS)
        zs, zco, zcg = zs.reshape(bs, ls, -1), zco.reshape(bs, ls, -1), zcg.reshape(bs, ls, -1)
        oa = _attn_sample(zs, cache_k, cache_v, bias_s, sinks_flat[li * SWA_Q_HEADS:(li + 1) * SWA_Q_HEADS], li)
        ob, vn_s = _gmlp(zs, w_s_small, b_st_small, gvn, li, ls, ls, True)
        oc, s_s = _gla_sample(zs, zco, zcg, wg16, bg, gon, state, li)
        x1, h2 = mix_out(xs, oa, ob, oc, li, ms, 1)
        f = _ffn_sample(h2, w_up16, w_dn16, FFN_CHUNK)
        xs, hs = layer_out(x1, f, pe_s, li, ms, 1)
        ks_rows.append(zs[:, :, OFF_K:OFF_K + KV_WIDTH].reshape(bs, ls, SWA_KV_HEADS, HEAD_DIM))
        vs_rows.append(zs[:, :, OFF_V:OFF_V + KV_WIDTH].reshape(bs, ls, SWA_KV_HEADS, HEAD_DIM))
        ss_states.append(s_s.reshape(bs, GLA_HEADS, GLA_DK, GLA_DV).astype(state_gla.dtype))
        gv_rows.append(vn_s)

    return (xp.reshape(bp, lp, D_MODEL), xs.reshape(bs, ls, D_MODEL),
            jnp.stack(kp_rows), jnp.stack(vp_rows), jnp.stack(ks_rows), jnp.stack(vs_rows),
            jnp.stack(sp_states), jnp.stack(ss_states), jnp.stack(gv_rows))
```

```python
import functools
import math

import jax
import jax.numpy as jnp
from jax import lax
from jax.experimental import pallas as pl
from jax.experimental.pallas import tpu as pltpu

F32 = jnp.float32
BF16 = jnp.bfloat16

D_MODEL = 2048
DEPTH = 4
CHUNK = 64
HEAD_DIM = 64
SWA_Q_HEADS = 16
SWA_KV_HEADS = 4
SWA_GROUP = SWA_Q_HEADS // SWA_KV_HEADS
SWA_WIDTH = 1024
KV_WIDTH = 256
WINDOW = 128
REL_BUCKETS = 32
REL_MAX_DIST = 128
GMLP_GROUPS = 4
GMLP_WIDTH = 512
GMLP_CHUNK = 128
GLA_HEADS = 4
GLA_DK = 64
GLA_DV = 128
GLA_WIDTH = 512
GLA_GATE_RANK = 16
GLA_GATE_TEMP = 16.0
D_FF = 4 * D_MODEL
PLE_DIM = 256
NORM_EPS = 1e-6
LANES = 128
HALF = LANES // 2
BF16_ROWS = 16

OFF_Q, OFF_K, OFF_V = 0, 1024, 1280
OFF_BU, OFF_BV = 1536, 2048
OFF_CQ, OFF_CK, OFF_CV = 2560, 2816, 3072
Z_WIDTH = 3584
CG_PAD = LANES

LOG2E = math.log2(math.e)
MASK_VALUE = -1e30
VMEM_LIMIT = 52 * 1024 * 1024
FFN_VMEM_LIMIT = 60 * 1024 * 1024

FFN_ROWS = 1024
FFN_CHUNK = 1024
FFN_FIRST_CHUNK = 512
W_IN_CAST_ROWS = 256
INPROJ_COLS = 1792
DENSE_ROWS = 512
OUTPROJ_SUB = 4
PLE_SUB = 2
ATTN_ROWS = 512
ATTN_LOOKAHEAD = 2
GLA_ROWS = 512
GLA_SUPER = 2 * CHUNK
GLA_SUB = 16


def _cparams(sem, vmem=VMEM_LIMIT):
    return pltpu.CompilerParams(dimension_semantics=sem, vmem_limit_bytes=vmem)


def _rms(x, g):
    return (x * lax.rsqrt(jnp.mean(x * x, axis=-1, keepdims=True) + NORM_EPS)) * g


def _nt(a, b):
    return lax.dot_general(a, b, (((1,), (1,)), ((), ())), preferred_element_type=F32)


def _tn(a, b):
    return lax.dot_general(a, b, (((0,), (0,)), ((), ())), preferred_element_type=F32)


def _prenorm_kernel(x_ref, g_ref, h_ref):
    h_ref[...] = _rms(x_ref[...], g_ref[...]).astype(BF16)


def _prenorm(x, g, li, tm):
    m = x.shape[0]
    return pl.pallas_call(
        _prenorm_kernel,
        grid=(m // tm,),
        in_specs=[pl.BlockSpec((tm, D_MODEL), lambda i: (i, 0)),
                  pl.BlockSpec((None, 1, D_MODEL), lambda i: (li, 0, 0))],
        out_specs=pl.BlockSpec((tm, D_MODEL), lambda i: (i, 0)),
        out_shape=jax.ShapeDtypeStruct((m, D_MODEL), BF16),
        compiler_params=_cparams(("parallel",)),
        name="prenorm",
    )(x, g)


def _inproj_kernel(h_ref, w_ref, wco_ref, wcg_ref, z_ref, zco_ref, zcg_ref):
    z_ref[...] = jnp.dot(h_ref[...], w_ref[...], preferred_element_type=F32)

    @pl.when(pl.program_id(1) == 0)
    def _():
        zco_ref[...] = jnp.dot(h_ref[...], wco_ref[...], preferred_element_type=F32)
        zcg_ref[...] = jnp.dot(h_ref[...], wcg_ref[...], preferred_element_type=F32)


def _inproj(h, w, wco, wcg, li, tm, tn):
    m = h.shape[0]
    extra = lambda width: pl.BlockSpec((None, D_MODEL, width), lambda i, j: (li, 0, 0))
    return pl.pallas_call(
        _inproj_kernel,
        grid=(m // tm, Z_WIDTH // tn),
        in_specs=[pl.BlockSpec((tm, D_MODEL), lambda i, j: (i, 0)),
                  pl.BlockSpec((None, D_MODEL, tn), lambda i, j: (li, 0, j)),
                  extra(GLA_WIDTH), extra(CG_PAD)],
        out_specs=[pl.BlockSpec((tm, tn), lambda i, j: (i, j)),
                   pl.BlockSpec((tm, GLA_WIDTH), lambda i, j: (i, 0)),
                   pl.BlockSpec((tm, CG_PAD), lambda i, j: (i, 0))],
        out_shape=[jax.ShapeDtypeStruct((m, Z_WIDTH), F32), jax.ShapeDtypeStruct((m, GLA_WIDTH), F32),
                   jax.ShapeDtypeStruct((m, CG_PAD), F32)],
        compiler_params=_cparams(("parallel", "arbitrary")),
        name="inproj",
    )(h, w, wco, wcg)


def _attn_prompt_kernel(q_ref, kp_ref, kc_ref, vp_ref, vc_ref, bias_ref, sink_ref, o_ref, *, nsub):
    first = pl.program_id(1) == 0
    kfull = jnp.concatenate([kp_ref[...], kc_ref[...]], axis=0)
    vfull = jnp.concatenate([vp_ref[...], vc_ref[...]], axis=0)
    rows = kfull.shape[0]
    lo_k = lax.broadcasted_iota(jnp.int32, (rows, LANES), 1) < HALF
    lo_q = lax.broadcasted_iota(jnp.int32, (WINDOW, LANES), 1) < HALF
    kdup, v_t = [], []
    for t in range(KV_WIDTH // LANES):
        kt = kfull[:, t * LANES:(t + 1) * LANES]
        kt_r = pltpu.roll(kt, HALF, 1)
        kdup.append(jnp.where(lo_k, kt, kt_r).astype(BF16))
        kdup.append(jnp.where(lo_k, kt_r, kt).astype(BF16))
        vt_t = vfull[:, t * LANES:(t + 1) * LANES].T
        v_t.append(vt_t[:HALF].astype(BF16))
        v_t.append(vt_t[HALF:].astype(BF16))
    ones = jnp.ones((BF16_ROWS, 2 * WINDOW), BF16)

    def scores(j, kv):
        qa = q_ref[j * WINDOW:(j + 1) * WINDOW, (2 * kv) * LANES:(2 * kv + 1) * LANES] * (HEAD_DIM ** -0.5 * LOG2E)
        qb = q_ref[j * WINDOW:(j + 1) * WINDOW, (2 * kv + 1) * LANES:(2 * kv + 2) * LANES] * (HEAD_DIM ** -0.5 * LOG2E)
        qm = jnp.concatenate([jnp.where(lo_q, qa, 0.0), jnp.where(lo_q, 0.0, qa),
                              jnp.where(lo_q, qb, 0.0), jnp.where(lo_q, 0.0, qb)], axis=0).astype(BF16)
        return _nt(kdup[kv][j * WINDOW:(j + 2) * WINDOW], qm)

    steps = [(j, kv) for j in range(nsub) for kv in range(SWA_KV_HEADS)]
    ahead = [scores(*steps[n]) for n in range(ATTN_LOOKAHEAD)]
    for n, (j, kv) in enumerate(steps):
        s_t = ahead.pop(0)
        if n + ATTN_LOOKAHEAD < len(steps):
            ahead.append(scores(*steps[n + ATTN_LOOKAHEAD]))
        keys = slice(j * WINDOW, (j + 2) * WINDOW)
        var = jnp.where(first, 0, 1) if j == 0 else 1
        sink = sink_ref[kv]
        e_cols, m_cols = [], []
        for hq in range(SWA_GROUP):
            cols = slice(hq * LANES, (hq + 1) * LANES)
            s_h = s_t[:, cols] + bias_ref[var, kv, :, cols]
            m_h = jnp.maximum(jnp.max(s_h, axis=0, keepdims=True), sink[:, cols])
            e_cols.append(jnp.exp2(s_h - m_h).astype(BF16))
            m_cols.append(m_h)
        e = jnp.concatenate(e_cols, axis=1)
        m = jnp.concatenate(m_cols, axis=1)
        lhs = jnp.concatenate([v_t[kv][:, keys], ones], axis=0)
        r = jnp.dot(lhs, e, preferred_element_type=F32)
        denom = r[HALF:HALF + 1] + jnp.exp2(sink - m)
        o_t = r[:HALF] * (1.0 / denom)
        for qt in range(2):
            pair = jnp.concatenate([o_t[:, (2 * qt) * LANES:(2 * qt + 1) * LANES],
                                    o_t[:, (2 * qt + 1) * LANES:(2 * qt + 2) * LANES]], axis=0)
            col = (2 * kv + qt) * LANES
            o_ref[j * WINDOW:(j + 1) * WINDOW, col:col + LANES] = pair.T.astype(o_ref.dtype)


def _attn_prompt(z, bias_t, sink_t, tl):
    b, l, _ = z.shape
    kcol, vcol = OFF_K // KV_WIDTH, OFF_V // KV_WIDTH
    per = tl // WINDOW
    prev = lambda col: pl.BlockSpec((None, WINDOW, KV_WIDTH),
                                    lambda bi, i: (bi, jnp.maximum(i * per - 1, 0), col))
    cur = lambda col: pl.BlockSpec((None, tl, KV_WIDTH), lambda bi, i: (bi, i, col))
    return pl.pallas_call(
        functools.partial(_attn_prompt_kernel, nsub=per),
        grid=(b, l // tl),
        in_specs=[pl.BlockSpec((None, tl, SWA_WIDTH), lambda bi, i: (bi, i, 0)),
                  prev(kcol), cur(kcol), prev(vcol), cur(vcol),
                  pl.BlockSpec(bias_t.shape, lambda bi, i: (0, 0, 0, 0)),
                  pl.BlockSpec(sink_t.shape, lambda bi, i: (0, 0, 0))],
        out_specs=pl.BlockSpec((None, tl, SWA_WIDTH), lambda bi, i: (bi, i, 0)),
        out_shape=jax.ShapeDtypeStruct((b, l, SWA_WIDTH), BF16),
        compiler_params=_cparams(("parallel", "arbitrary")),
        name="swa_prompt",
    )(z, z, z, z, z, bias_t, sink_t)


def _attn_sample_kernel(sink_ref, q_ref, kp_ref, kc_ref, vp_ref, vc_ref, bias_ref, o_ref):
    q = (q_ref[...] * (HEAD_DIM ** -0.5)).astype(BF16)
    k = jnp.concatenate([kp_ref[...], kc_ref[...]], axis=0)
    v = jnp.concatenate([vp_ref[...], vc_ref[...]], axis=0)
    lk = k.shape[0]
    lo = lax.broadcasted_iota(jnp.int32, (lk, LANES), 1) < HALF
    qlane = lax.broadcasted_iota(jnp.int32, (q.shape[0], LANES), 1)
    for t in range(KV_WIDTH // LANES):
        kt = k[:, t * LANES:(t + 1) * LANES]
        vt = v[:, t * LANES:(t + 1) * LANES]
        kt_r = pltpu.roll(kt, HALF, 1)
        vt_r = pltpu.roll(vt, HALF, 1)
        for half in range(2):
            kv = 2 * t + half
            k_in_lo, k_in_hi = (kt, kt_r) if half == 0 else (kt_r, kt)
            v_in_lo, v_in_hi = (vt, vt_r) if half == 0 else (vt_r, vt)
            k_half = (jnp.where(lo, k_in_lo, 0.0).astype(BF16), jnp.where(lo, 0.0, k_in_hi).astype(BF16))
            v_half = (jnp.where(lo, v_in_lo, 0.0).astype(BF16), jnp.where(lo, 0.0, v_in_hi).astype(BF16))
            for qt in (2 * kv, 2 * kv + 1):
                q2 = q[:, qt * LANES:(qt + 1) * LANES]
                acc = None
                inv = []
                for hh in range(2):
                    h = 2 * qt + hh
                    s = _nt(q2, k_half[hh]) + bias_ref[h]
                    sink = sink_ref[h]
                    m = jnp.maximum(jnp.max(s, axis=-1, keepdims=True), sink)
                    e = jnp.exp(s - m)
                    denom = jnp.sum(e, axis=-1, keepdims=True) + jnp.exp(sink - m)
                    inv.append(1.0 / denom)
                    pv = jnp.dot(e.astype(BF16), v_half[hh], preferred_element_type=F32)
                    acc = pv if acc is None else acc + pv
                o2 = acc * jnp.where(qlane < HALF, inv[0], inv[1])
                o_ref[:, qt * LANES:(qt + 1) * LANES] = o2.astype(o_ref.dtype)


def _attn_sample(z, cache_k, cache_v, bias, sinks, li):
    b, tq, _ = z.shape
    kcol, vcol = OFF_K // KV_WIDTH, OFF_V // KV_WIDTH
    cache = pl.BlockSpec((None, None, WINDOW, KV_WIDTH), lambda bi: (li, bi, 0, 0))
    return pl.pallas_call(
        _attn_sample_kernel,
        grid=(b,),
        in_specs=[pl.BlockSpec(memory_space=pltpu.SMEM),
                  pl.BlockSpec((None, tq, SWA_WIDTH), lambda bi: (bi, 0, 0)),
                  cache,
                  pl.BlockSpec((None, tq, KV_WIDTH), lambda bi: (bi, 0, kcol)),
                  cache,
                  pl.BlockSpec((None, tq, KV_WIDTH), lambda bi: (bi, 0, vcol)),
                  pl.BlockSpec(bias.shape, lambda bi: (0, 0, 0))],
        out_specs=pl.BlockSpec((None, tq, SWA_WIDTH), lambda bi: (bi, 0, 0)),
        out_shape=jax.ShapeDtypeStruct((b, tq, SWA_WIDTH), BF16),
        compiler_params=_cparams(("parallel",)),
        name="swa_sample",
    )(sinks, z, cache_k, z, cache_v, z, bias)


def _gmlp_kernel(bu_ref, bv_ref, w_ref, bs_ref, gv_ref, o_ref, *maybe_vn_ref, c, n_chunks):
    row = lax.broadcasted_iota(jnp.int32, (c, c), 0)
    col = lax.broadcasted_iota(jnp.int32, (c, c), 1)
    causal = (row // CHUNK) >= (col // CHUNK)
    v = jax.nn.gelu(bv_ref[...])
    vc = v - jnp.mean(v, axis=-1, keepdims=True)
    vn = (vc * lax.rsqrt(jnp.mean(vc * vc, axis=-1, keepdims=True) + NORM_EPS)) * gv_ref[...]
    if maybe_vn_ref:
        maybe_vn_ref[0][...] = vn
    u = jax.nn.gelu(bu_ref[...])
    vn16 = vn.astype(BF16)
    for g in range(GMLP_GROUPS):
        wg = jnp.where(causal, w_ref[g], 0.0).astype(BF16)
        bias = bs_ref[:, g:g + 1]
        cols = slice(g * LANES, (g + 1) * LANES)
        for ci in range(n_chunks):
            rows = slice(ci * c, (ci + 1) * c)
            sv = jnp.dot(wg, vn16[rows, cols], preferred_element_type=F32) + bias
            o_ref[rows, cols] = (u[rows, cols] * sv).astype(o_ref.dtype)


def _gmlp(z, w_s, b_st, g_vnorm, li, c, tl, want_vn):
    b, l, _ = z.shape
    ucol, vcol = OFF_BU // GMLP_WIDTH, OFF_BV // GMLP_WIDTH
    out_shape = [jax.ShapeDtypeStruct((b, l, GMLP_WIDTH), BF16)]
    out_specs = [pl.BlockSpec((None, tl, GMLP_WIDTH), lambda bi, i: (bi, i, 0))]
    if want_vn:
        out_shape.append(jax.ShapeDtypeStruct((b, l, GMLP_WIDTH), F32))
        out_specs.append(pl.BlockSpec((None, tl, GMLP_WIDTH), lambda bi, i: (bi, i, 0)))
    return pl.pallas_call(
        functools.partial(_gmlp_kernel, c=c, n_chunks=tl // c),
        grid=(b, l // tl),
        in_specs=[pl.BlockSpec((None, tl, GMLP_WIDTH), lambda bi, i: (bi, i, ucol)),
                  pl.BlockSpec((None, tl, GMLP_WIDTH), lambda bi, i: (bi, i, vcol)),
                  pl.BlockSpec((None, GMLP_GROUPS, c, c), lambda bi, i: (li, 0, 0, 0)),
                  pl.BlockSpec((None, c, GMLP_GROUPS), lambda bi, i: (li, 0, 0)),
                  pl.BlockSpec((None, 1, GMLP_WIDTH), lambda bi, i: (li, 0, 0))],
        out_specs=out_specs,
        out_shape=out_shape,
        compiler_params=_cparams(("parallel", "parallel")),
        name="gmlp",
    )(z, z, w_s, b_st, g_vnorm)


def _log_sigmoid(x):
    return jnp.minimum(x, 0.0) - jnp.log1p(jnp.exp(-jnp.abs(x)))


def _cumsum_rows(x):
    n = x.shape[0]
    row = lax.broadcasted_iota(jnp.int32, x.shape, 0)
    shift = 1
    while shift < n:
        x = x + jnp.where(row >= shift, pltpu.roll(x, shift, 0), 0.0)
        shift *= 2
    return x


def _gla_specs(tl, li):
    cq, ck, cv = OFF_CQ // KV_WIDTH, OFF_CK // KV_WIDTH, OFF_CV // GLA_WIDTH
    zrow = lambda width, col: pl.BlockSpec((None, tl, width), lambda bi, i: (bi, i, col))
    return [zrow(KV_WIDTH, cq), zrow(KV_WIDTH, ck), zrow(GLA_WIDTH, cv), zrow(GLA_WIDTH, 0), zrow(CG_PAD, 0),
            pl.BlockSpec((None, CG_PAD, KV_WIDTH), lambda bi, i: (li, 0, 0)),
            pl.BlockSpec((None, 1, KV_WIDTH), lambda bi, i: (li, 0, 0)),
            pl.BlockSpec((None, 1, GLA_WIDTH), lambda bi, i: (li, 0, 0))]


def _gla_sample_kernel(cq_ref, ck_ref, cv_ref, co_ref, cg_ref, wg_ref, bg_ref, gon_ref, s0_ref,
                       o_ref, sfin_ref):
    c = cq_ref.shape[0]
    lane = lax.broadcasted_iota(jnp.int32, (c, LANES), 1)
    lo = lane < HALF
    row = lax.broadcasted_iota(jnp.int32, (c, c), 0)
    col = lax.broadcasted_iota(jnp.int32, (c, c), 1)
    tril = row >= col
    gate = jnp.dot(cg_ref[...].astype(BF16), wg_ref[...], preferred_element_type=F32) + bg_ref[...]
    la = _log_sigmoid(gate) / GLA_GATE_TEMP
    bcum = _cumsum_rows(la)
    blast = bcum[c - 1:c, :]
    q = cq_ref[...] * (GLA_DK ** -0.5)
    k = ck_ref[...]
    qd = q * jnp.exp(bcum)
    ref = bcum[0:1, :]
    qr = q * jnp.exp(bcum - ref)
    kd = k * jnp.exp(ref - bcum)
    kl = k * jnp.exp(blast - bcum)
    dec = jnp.exp(blast)
    for j in range(2):
        tile = slice(j * LANES, (j + 1) * LANES)
        qd_t, qr_t, kd_t, kl_t = qd[:, tile], qr[:, tile], kd[:, tile], kl[:, tile]
        st = s0_ref[j].T
        st16 = st.astype(BF16)
        st_new = st * dec[:, tile]
        for hh in range(2):
            h = 2 * j + hh
            keep = lo if hh == 0 else jnp.logical_not(lo)
            hc = slice(h * GLA_DV, (h + 1) * GLA_DV)
            v_h = cv_ref[:, hc].astype(BF16)
            att = _nt(qr_t.astype(BF16), jnp.where(keep, kd_t, 0.0).astype(BF16))
            att = jnp.where(tril, att, 0.0)
            o_h = jnp.dot(att.astype(BF16), v_h, preferred_element_type=F32)
            o_h = o_h + _nt(jnp.where(keep, qd_t, 0.0).astype(BF16), st16)
            st_new = st_new + _tn(v_h, jnp.where(keep, kl_t, 0.0).astype(BF16))
            o_n = _rms(o_h, gon_ref[:, hc])
            o_ref[:, hc] = (o_n * jax.nn.silu(co_ref[:, hc])).astype(o_ref.dtype)
        sfin_ref[j] = st_new.T


def _gla_sample(z, zco, zcg, w_gate, b_gate, g_onorm, state, li):
    b, tl, _ = z.shape
    st_spec = pl.BlockSpec((None, 2, LANES, GLA_DV), lambda bi, i: (bi, 0, 0, 0))
    return pl.pallas_call(
        _gla_sample_kernel,
        grid=(b, 1),
        in_specs=_gla_specs(tl, li) + [pl.BlockSpec((None, None, 2, LANES, GLA_DV),
                                                    lambda bi, i: (li, bi, 0, 0, 0))],
        out_specs=[pl.BlockSpec((None, tl, GLA_WIDTH), lambda bi, i: (bi, i, 0)), st_spec],
        out_shape=[jax.ShapeDtypeStruct((b, tl, GLA_WIDTH), BF16),
                   jax.ShapeDtypeStruct((b, 2, LANES, GLA_DV), F32)],
        compiler_params=_cparams(("parallel", "arbitrary")),
        name="gla_sample",
    )(z, z, z, zco, zcg, w_gate, b_gate, g_onorm, state)


def _gla_prompt_kernel(cq_ref, ck_ref, cv_ref, co_ref, cg_ref, wg_ref, bg_ref, gon_ref,
                       o_ref, sfin_ref, st_ref):
    tl = cq_ref.shape[0]
    n_chunks = tl // CHUNK

    @pl.when(pl.program_id(1) == 0)
    def _():
        st_ref[...] = jnp.zeros_like(st_ref)

    gate = jnp.dot(cg_ref[...].astype(BF16), wg_ref[...], preferred_element_type=F32) + bg_ref[...]
    la = _log_sigmoid(gate) / GLA_GATE_TEMP
    la_hi = la.astype(BF16)
    la_lo = (la - la_hi.astype(F32)).astype(BF16)
    row = lax.broadcasted_iota(jnp.int32, (GLA_SUPER, GLA_SUPER), 0)
    col = lax.broadcasted_iota(jnp.int32, (GLA_SUPER, GLA_SUPER), 1)
    tri = ((row // CHUNK) == (col // CHUNK)) & (row >= col)
    tri16 = jnp.where(tri, 1.0, 0.0).astype(BF16)
    bcum = jnp.concatenate(
        [jnp.dot(tri16, la_hi[s * GLA_SUPER:(s + 1) * GLA_SUPER], preferred_element_type=F32)
         + jnp.dot(tri16, la_lo[s * GLA_SUPER:(s + 1) * GLA_SUPER], preferred_element_type=F32)
         for s in range(tl // GLA_SUPER)], axis=0)
    blast = [bcum[(ci + 1) * CHUNK - 1:(ci + 1) * CHUNK, :] for ci in range(n_chunks)]
    blast_b = jnp.concatenate([jnp.broadcast_to(bl, (CHUNK, KV_WIDTH)) for bl in blast], axis=0)
    k = ck_ref[...]
    q = cq_ref[...] * (GLA_DK ** -0.5)
    qd = q * jnp.exp(bcum)
    kl = k * jnp.exp(blast_b - bcum)
    ref = jnp.concatenate([jnp.broadcast_to(bcum[r0:r0 + 1], (GLA_SUB, KV_WIDTH))
                           for r0 in range(0, tl, GLA_SUB)], axis=0)
    q_same = q * jnp.exp(bcum - ref)
    k_same = k * jnp.exp(ref - bcum)
    sub_idx = (lax.broadcasted_iota(jnp.int32, (tl, KV_WIDTH), 0) % CHUNK) // GLA_SUB
    q_cross, k_cross = [], []
    for jj in range(CHUNK // GLA_SUB - 1):
        r = jnp.concatenate([jnp.broadcast_to(bcum[ci * CHUNK + (jj + 1) * GLA_SUB:ci * CHUNK + (jj + 1) * GLA_SUB + 1],
                                              (CHUNK, KV_WIDTH)) for ci in range(n_chunks)], axis=0)
        q_cross.append(q * jnp.exp(jnp.where(sub_idx > jj, bcum - r, MASK_VALUE)))
        k_cross.append(k * jnp.exp(jnp.where(sub_idx == jj, r - bcum, MASK_VALUE)))
    same_sub = (row // GLA_SUB) == (col // GLA_SUB)
    lo = lax.broadcasted_iota(jnp.int32, (tl, LANES), 1) < HALF

    keeps = [lo, jnp.logical_not(lo)]
    chunk_rows = [slice(ci * CHUNK, (ci + 1) * CHUNK) for ci in range(n_chunks)]
    super_rows = [slice(s * GLA_SUPER, (s + 1) * GLA_SUPER) for s in range(tl // GLA_SUPER)]
    stack = lambda pair, rows: jnp.concatenate([pair[0][rows], pair[1][rows]], axis=0)

    pairs = []
    for j in range(2):
        tile = slice(j * LANES, (j + 1) * LANES)
        hcs = [slice((2 * j + hh) * GLA_DV, (2 * j + hh + 1) * GLA_DV) for hh in range(2)]
        v = [cv_ref[:, hc].astype(BF16) for hc in hcs]
        qm = [jnp.where(kp, qd[:, tile], 0.0).astype(BF16) for kp in keeps]
        klm = [jnp.where(kp, kl[:, tile], 0.0).astype(BF16) for kp in keeps]
        q_same16 = q_same[:, tile].astype(BF16)
        k_same16 = [jnp.where(kp, k_same[:, tile], 0.0).astype(BF16) for kp in keeps]
        q_cross16 = jnp.concatenate([qc[:, tile] for qc in q_cross], axis=1).astype(BF16)
        k_cross16 = [jnp.concatenate([jnp.where(kp, kc[:, tile], 0.0) for kc in k_cross], axis=1).astype(BF16)
                     for kp in keeps]
        upd = [_tn(stack(v, rows), stack(klm, rows)) for rows in chunk_rows]
        att = [(_nt(q_same16[rows], stack(k_same16, rows)), _nt(q_cross16[rows], stack(k_cross16, rows)))
               for rows in super_rows]
        pairs.append(dict(tile=tile, hcs=hcs, v=v, qm=qm, upd=upd, att=att))
    for j, p in enumerate(pairs):
        st = st_ref[j]
        st16 = []
        for ci in range(n_chunks):
            st16.append(st.astype(BF16))
            st = st * jnp.exp(blast[ci][:, p["tile"]]) + p["upd"][ci]
        st_ref[j] = st
        p["inter"] = [_nt(stack(p["qm"], rows), st16[ci]) for ci, rows in enumerate(chunk_rows)]
    for p in pairs:
        intra = [[], []]
        for rows, (att_same, att_cross) in zip(super_rows, p["att"]):
            for hh in range(2):
                cols = slice(hh * GLA_SUPER, (hh + 1) * GLA_SUPER)
                a_h = jnp.where(tri, jnp.where(same_sub, att_same[:, cols], att_cross[:, cols]), 0.0).astype(BF16)
                intra[hh].append(jnp.dot(a_h, p["v"][hh][rows], preferred_element_type=F32))
        for hh in range(2):
            hc = p["hcs"][hh]
            o_h = jnp.concatenate(intra[hh], axis=0) + jnp.concatenate(
                [blk[hh * CHUNK:(hh + 1) * CHUNK] for blk in p["inter"]], axis=0)
            o_n = _rms(o_h, gon_ref[:, hc])
            o_ref[:, hc] = (o_n * jax.nn.silu(co_ref[:, hc])).astype(o_ref.dtype)

    @pl.when(pl.program_id(1) == pl.num_programs(1) - 1)
    def _():
        for j in range(2):
            sfin_ref[j] = st_ref[j].T


def _gla_prompt(z, zco, zcg, w_gate, b_gate, g_onorm, li, tl):
    b, l, _ = z.shape
    return pl.pallas_call(
        _gla_prompt_kernel,
        grid=(b, l // tl),
        in_specs=_gla_specs(tl, li),
        out_specs=[pl.BlockSpec((None, tl, GLA_WIDTH), lambda bi, i: (bi, i, 0)),
                   pl.BlockSpec((None, 2, LANES, GLA_DV), lambda bi, i: (bi, 0, 0, 0))],
        out_shape=[jax.ShapeDtypeStruct((b, l, GLA_WIDTH), BF16),
                   jax.ShapeDtypeStruct((b, 2, LANES, GLA_DV), F32)],
        scratch_shapes=[pltpu.VMEM((2, GLA_DV, LANES), F32)],
        compiler_params=_cparams(("parallel", "arbitrary")),
        name="gla_prompt",
    )(z, z, z, zco, zcg, w_gate, b_gate, g_onorm)


def _pipelined(nsub, matmul_stage, vector_stage):
    nxt = matmul_stage(0)
    for r in range(nsub):
        cur = nxt
        if r + 1 < nsub:
            nxt = matmul_stage(r + 1)
        vector_stage(r, cur)


def _outproj_kernel(x_ref, oa_ref, ob_ref, oc_ref, w_ref, g_ref, g2_ref, x1_ref, h2_ref, *, nsub):
    sub = x_ref.shape[0] // nsub

    def matmul_stage(r):
        rows = slice(r * sub, (r + 1) * sub)
        mix = jnp.dot(oa_ref[rows, :], w_ref[0:SWA_WIDTH, :], preferred_element_type=F32)
        mix = mix + jnp.dot(ob_ref[rows, :], w_ref[SWA_WIDTH:SWA_WIDTH + GMLP_WIDTH, :], preferred_element_type=F32)
        return mix + jnp.dot(oc_ref[rows, :], w_ref[SWA_WIDTH + GMLP_WIDTH:, :], preferred_element_type=F32)

    def vector_stage(r, mix):
        rows = slice(r * sub, (r + 1) * sub)
        x1 = x_ref[rows, :] + _rms(mix, g_ref[...])
        x1_ref[rows, :] = x1
        h2_ref[rows, :] = _rms(x1, g2_ref[...]).astype(BF16)

    _pipelined(nsub, matmul_stage, vector_stage)


def _outproj(x, oa, ob, oc, w, g, g2, li, tm, nsub):
    m = x.shape[0]
    row = lambda width: pl.BlockSpec((tm, width), lambda i: (i, 0))
    gain = pl.BlockSpec((None, 1, D_MODEL), lambda i: (li, 0, 0))
    return pl.pallas_call(
        functools.partial(_outproj_kernel, nsub=nsub),
        grid=(m // tm,),
        in_specs=[row(D_MODEL), row(SWA_WIDTH), row(GMLP_WIDTH), row(GLA_WIDTH),
                  pl.BlockSpec((None, D_MODEL, D_MODEL), lambda i: (li, 0, 0)), gain, gain],
        out_specs=[row(D_MODEL), row(D_MODEL)],
        out_shape=[jax.ShapeDtypeStruct((m, D_MODEL), F32), jax.ShapeDtypeStruct((m, D_MODEL), BF16)],
        compiler_params=_cparams(("parallel",)),
        name="outproj",
    )(x, oa, ob, oc, w, g, g2)


def _ffn_zero_first(f_ref):
    @pl.when(pl.program_id(1) == 0)
    def _():
        f_ref[...] = jnp.zeros_like(f_ref)


def _ffn_kernel(h_ref, wup_ref, wdn_ref, f_ref):
    _ffn_zero_first(f_ref)
    u = jnp.maximum(jnp.dot(h_ref[...], wup_ref[...], preferred_element_type=F32), 0.0)
    f_ref[...] += jnp.dot((u * u).astype(BF16), wdn_ref[...], preferred_element_type=F32)


def _ffn_rest_kernel(h_ref, wup_ref, wdn_ref, f_first_ref, f_ref):
    _ffn_kernel(h_ref, wup_ref, wdn_ref, f_ref)


def _ffn_first_kernel(h_ref, wup_ref, wdn_ref, f_ref, wup16_ref, wdn16_ref):
    _ffn_zero_first(f_ref)
    wup16_ref[...] = wup_ref[...].astype(BF16)
    wdn16_ref[...] = wdn_ref[...].astype(BF16)
    u = jnp.maximum(jnp.dot(h_ref[...], wup16_ref[...], preferred_element_type=F32), 0.0)
    f_ref[...] += jnp.dot((u * u).astype(BF16), wdn16_ref[...], preferred_element_type=F32)


def _ffn_specs(tm, tk, row_of):
    return [pl.BlockSpec((tm, D_MODEL), lambda i, k: (row_of(i), 0)),
            pl.BlockSpec((None, D_MODEL, tk), lambda i, k: (0, 0, k)),
            pl.BlockSpec((None, tk, D_MODEL), lambda i, k: (0, k, 0))]


def _ffn_prompt(h, w_up, w_down, li, tm, tk_first, tk):
    m = h.shape[0]
    f_shape = jax.ShapeDtypeStruct((m, D_MODEL), F32)
    f, wup16, wdn16 = pl.pallas_call(
        _ffn_first_kernel,
        grid=(1, D_FF // tk_first),
        in_specs=[pl.BlockSpec((tm, D_MODEL), lambda i, k: (0, 0)),
                  pl.BlockSpec((None, D_MODEL, tk_first), lambda i, k: (li, 0, k)),
                  pl.BlockSpec((None, tk_first, D_MODEL), lambda i, k: (li, k, 0))],
        out_specs=_ffn_specs(tm, tk_first, lambda i: 0),
        out_shape=[f_shape, jax.ShapeDtypeStruct((1, D_MODEL, D_FF), BF16),
                   jax.ShapeDtypeStruct((1, D_FF, D_MODEL), BF16)],
        compiler_params=_cparams(("arbitrary", "arbitrary"), FFN_VMEM_LIMIT),
        name="ffn_first",
    )(h, w_up, w_down)
    f = pl.pallas_call(
        _ffn_rest_kernel,
        grid=(m // tm - 1, D_FF // tk),
        in_specs=_ffn_specs(tm, tk, lambda i: i + 1) + [pl.BlockSpec(memory_space=pl.ANY)],
        out_specs=pl.BlockSpec((tm, D_MODEL), lambda i, k: (i + 1, 0)),
        out_shape=f_shape,
        input_output_aliases={3: 0},
        compiler_params=_cparams(("parallel", "arbitrary"), FFN_VMEM_LIMIT),
        name="ffn_rest",
    )(h, wup16, wdn16, f)
    return f, wup16, wdn16


def _ffn_sample(h, wup16, wdn16, tk):
    m = h.shape[0]
    return pl.pallas_call(
        _ffn_kernel,
        grid=(1, D_FF // tk),
        in_specs=_ffn_specs(m, tk, lambda i: i),
        out_specs=pl.BlockSpec((m, D_MODEL), lambda i, k: (i, 0)),
        out_shape=jax.ShapeDtypeStruct((m, D_MODEL), F32),
        compiler_params=_cparams(("parallel", "arbitrary"), FFN_VMEM_LIMIT),
        name="ffn_sample",
    )(h, wup16, wdn16)


def _ple_kernel(x1_ref, f_ref, pe_ref, gpost_ref, wp_ref, wg_ref, gnext_ref, x3_ref, *maybe_hn_ref, nsub):
    sub = x1_ref.shape[0] // nsub

    def matmul_stage(r):
        rows = slice(r * sub, (r + 1) * sub)
        proj = jnp.dot(pe_ref[rows, :].astype(BF16), wp_ref[...], preferred_element_type=F32)
        x2 = x1_ref[rows, :] + _rms(f_ref[rows, :], gpost_ref[...])
        gate = jnp.dot(x2.astype(BF16), wg_ref[...], preferred_element_type=F32)
        return x2, gate, proj

    def vector_stage(r, staged):
        rows = slice(r * sub, (r + 1) * sub)
        x2, gate, proj = staged
        x3 = x2 + proj * jax.nn.sigmoid(gate)
        x3_ref[rows, :] = x3
        if maybe_hn_ref:
            maybe_hn_ref[0][rows, :] = _rms(x3, gnext_ref[...]).astype(BF16)

    _pipelined(nsub, matmul_stage, vector_stage)


def _ple(x1, f, pe, gpost, wp, wg, gnext, li, tm, nsub, want_h):
    m = x1.shape[0]
    row = lambda: pl.BlockSpec((tm, D_MODEL), lambda i: (i, 0))
    ln = min(li + 1, DEPTH - 1)
    out_specs = [row()]
    out_shape = [jax.ShapeDtypeStruct((m, D_MODEL), F32)]
    if want_h:
        out_specs.append(row())
        out_shape.append(jax.ShapeDtypeStruct((m, D_MODEL), BF16))
    return pl.pallas_call(
        functools.partial(_ple_kernel, nsub=nsub),
        grid=(m // tm,),
        in_specs=[row(), row(),
                  pl.BlockSpec((None, tm, PLE_DIM), lambda i: (li, i, 0)),
                  pl.BlockSpec((None, 1, D_MODEL), lambda i: (li, 0, 0)),
                  pl.BlockSpec((None, PLE_DIM, D_MODEL), lambda i: (li, 0, 0)),
                  pl.BlockSpec((None, D_MODEL, D_MODEL), lambda i: (li, 0, 0)),
                  pl.BlockSpec((None, 1, D_MODEL), lambda i: (ln, 0, 0))],
        out_specs=out_specs,
        out_shape=out_shape,
        compiler_params=_cparams(("parallel",)),
        name="ple",
    )(x1, f, pe, gpost, wp, wg, gnext)


def _t5_bucket(rel):
    half = REL_BUCKETS // 2
    max_exact = half // 2
    n = -rel
    ret = jnp.where(n < 0, half, 0)
    n = jnp.abs(n)
    nf = jnp.maximum(n, 1).astype(jnp.float32)
    large = max_exact + (jnp.log(nf / max_exact) / math.log(REL_MAX_DIST / max_exact)
                         * (half - max_exact)).astype(jnp.int32)
    large = jnp.minimum(large, half - 1)
    return ret + jnp.where(n < max_exact, n, large)


def _bias_kernel(tab_ref, bkt_p_ref, valid_p_ref, bkt_s_ref, out_p_ref, out_s_ref):
    bkt_p = bkt_p_ref[...]
    bkt_s = bkt_s_ref[...]
    for h in range(SWA_Q_HEADS):
        def pick(b, accs):
            val = tab_ref[b, h]
            return (jnp.where(bkt_p == b, val, accs[0]), jnp.where(bkt_s == b, val, accs[1]))

        acc_p, acc_s = lax.fori_loop(0, REL_BUCKETS, pick,
                                     (jnp.zeros(bkt_p.shape, F32), jnp.zeros(bkt_s.shape, F32)))
        kv, hq = divmod(h, SWA_GROUP)
        for var in range(2):
            out_p_ref[var, kv, :, hq * LANES:(hq + 1) * LANES] = jnp.where(valid_p_ref[var] != 0, acc_p * LOG2E, MASK_VALUE)
        out_s_ref[h] = acc_s


def _bias_tables(rel_bias, dec_seq):
    qpos = jnp.arange(WINDOW)[None, :]
    kpos = jnp.arange(-WINDOW, WINDOW)[:, None]
    bkt_p = _t5_bucket(kpos - qpos).astype(jnp.int32)
    kchunk = jnp.floor_divide(kpos, CHUNK)
    qchunk = qpos // CHUNK
    valid = (kchunk >= qchunk - WINDOW // CHUNK) & (kchunk <= qchunk)
    valid_p = jnp.stack([valid & (kpos >= 0), valid]).astype(jnp.int32)
    bkt_s = _t5_bucket(jnp.arange(-WINDOW, dec_seq)[None, :] - jnp.arange(dec_seq)[:, None]).astype(jnp.int32)
    full = lambda a: pl.BlockSpec(a.shape, lambda: (0,) * a.ndim)
    out_shape = [jax.ShapeDtypeStruct((2, SWA_KV_HEADS, 2 * WINDOW, SWA_GROUP * LANES), F32),
                 jax.ShapeDtypeStruct((SWA_Q_HEADS, dec_seq, WINDOW + dec_seq), F32)]
    return pl.pallas_call(
        _bias_kernel,
        in_specs=[pl.BlockSpec(memory_space=pltpu.SMEM), full(bkt_p), full(valid_p), full(bkt_s)],
        out_specs=[full(s) for s in out_shape],
        out_shape=out_shape,
        name="rel_bias",
    )(rel_bias, bkt_p, valid_p, bkt_s)


def _w_in_kernel(w_ref, main_ref, co_ref, cg_ref):
    co = Z_WIDTH + GLA_GATE_RANK
    main_ref[...] = w_ref[:, :Z_WIDTH].astype(BF16)
    lane = lax.broadcasted_iota(jnp.int32, cg_ref.shape, 1)
    cg_ref[...] = jnp.where(lane < GLA_GATE_RANK, w_ref[:, Z_WIDTH:Z_WIDTH + CG_PAD], 0.0).astype(BF16)
    co_ref[...] = w_ref[:, co:co + GLA_WIDTH].astype(BF16)


def _prep_w_in(w, rows):
    depth, d, width = w.shape
    blk = lambda cols: pl.BlockSpec((None, rows, cols), lambda li, i: (li, i, 0))
    return pl.pallas_call(
        _w_in_kernel,
        grid=(depth, d // rows),
        in_specs=[blk(width)],
        out_specs=[blk(Z_WIDTH), blk(GLA_WIDTH), blk(CG_PAD)],
        out_shape=[jax.ShapeDtypeStruct((depth, d, Z_WIDTH), BF16), jax.ShapeDtypeStruct((depth, d, GLA_WIDTH), BF16),
                   jax.ShapeDtypeStruct((depth, d, CG_PAD), BF16)],
        compiler_params=_cparams(("parallel", "parallel")),
        name="w_in_cast",
    )(w)


def kernel(x_prompt, x_sample, cache_swa_k, cache_swa_v, state_gla, p_prompt, p_sample, w_in, w_gate, b_gate, rel_bias, attn_sinks, w_spatial, b_spatial, g_gmlp_vnorm, g_gla_onorm, w_out, g_mix_pre, g_mix_post, g_ffn_pre, g_ffn_post, w_up, w_down, w_ple, w_ple_gate):
    bp, lp, _ = x_prompt.shape
    bs, ls, _ = x_sample.shape
    mp, ms = bp * lp, bs * ls
    row3 = lambda a: a.reshape(DEPTH, 1, -1)

    bias_p, bias_s = _bias_tables(rel_bias, ls)
    sink_t = jnp.repeat(attn_sinks * LOG2E, LANES, axis=-1).reshape(DEPTH, SWA_KV_HEADS, 1, SWA_GROUP * LANES)
    sinks_flat = attn_sinks.reshape(-1)

    w_in16, w_co16, w_cg16 = _prep_w_in(w_in, W_IN_CAST_ROWS)
    w_out16, w_ple16, w_plg16 = w_out.astype(BF16), w_ple.astype(BF16), w_ple_gate.astype(BF16)
    wg16 = jnp.concatenate([w_gate, jnp.zeros((DEPTH, CG_PAD - GLA_GATE_RANK, KV_WIDTH), F32)], axis=1).astype(BF16)
    g_pre, g_post, g_fpre, g_fpost = row3(g_mix_pre), row3(g_mix_post), row3(g_ffn_pre), row3(g_ffn_post)
    bg, gon, gvn = row3(b_gate), row3(g_gla_onorm), row3(g_gmlp_vnorm)
    b_st = jnp.swapaxes(b_spatial, 1, 2)
    w_s_small, b_st_small = w_spatial[:, :, :ls, :ls], b_st[:, :ls, :]
    pe_p = p_prompt.reshape(DEPTH, mp, PLE_DIM)
    pe_s = p_sample.reshape(DEPTH, ms, PLE_DIM)
    cache_k = cache_swa_k.reshape(DEPTH, bs, WINDOW, KV_WIDTH)
    cache_v = cache_swa_v.reshape(DEPTH, bs, WINDOW, KV_WIDTH)
    state = state_gla.reshape(DEPTH, bs, 2, LANES, GLA_DV)

    def mix_out(x, oa, ob, oc, li, ts, nsub):
        m = x.shape[0]
        return _outproj(x, oa.reshape(m, -1), ob.reshape(m, -1), oc.reshape(m, -1), w_out16, g_post, g_fpre,
                        li, ts, nsub)

    def layer_out(x1, f, pe, li, ts, nsub):
        out = _ple(x1, f, pe, g_fpost, w_ple16, w_plg16, g_pre, li, ts, nsub, li + 1 < DEPTH)
        return out if li + 1 < DEPTH else (out[0], None)

    xp = x_prompt.reshape(mp, D_MODEL)
    xs = x_sample.reshape(ms, D_MODEL)
    hp = _prenorm(xp, g_pre, 0, FFN_ROWS)
    hs = _prenorm(xs, g_pre, 0, ms)
    kp_rows, vp_rows, ks_rows, vs_rows, sp_states, ss_states, gv_rows = [], [], [], [], [], [], []
    for li in range(DEPTH):
        zp, zco, zcg = _inproj(hp, w_in16, w_co16, w_cg16, li, FFN_ROWS, INPROJ_COLS)
        zp, zco, zcg = zp.reshape(bp, lp, -1), zco.reshape(bp, lp, -1), zcg.reshape(bp, lp, -1)
        oa = _attn_prompt(zp, bias_p, sink_t[li], ATTN_ROWS)
        ob, = _gmlp(zp, w_spatial, b_st, gvn, li, GMLP_CHUNK, 512, False)
        oc, s_p = _gla_prompt(zp, zco, zcg, wg16, bg, gon, li, GLA_ROWS)
        x1, h2 = mix_out(xp, oa, ob, oc, li, DENSE_ROWS, OUTPROJ_SUB)
        f, w_up16, w_dn16 = _ffn_prompt(h2, w_up, w_down, li, FFN_ROWS, FFN_FIRST_CHUNK, FFN_CHUNK)
        xp, hp = layer_out(x1, f, pe_p, li, DENSE_ROWS, PLE_SUB)
        kp_rows.append(zp[:, lp - WINDOW:, OFF_K:OFF_K + KV_WIDTH].reshape(bp, WINDOW, SWA_KV_HEADS, HEAD_DIM))
        vp_rows.append(zp[:, lp - WINDOW:, OFF_V:OFF_V + KV_WIDTH].reshape(bp, WINDOW, SWA_KV_HEADS, HEAD_DIM))
        sp_states.append(s_p.reshape(bp, GLA_HEADS, GLA_DK, GLA_DV))

        zs, zco, zcg = _inproj(hs, w_in16, w_co16, w_cg16, li, ms, INPROJ_COLS)
        zs, zco, zcg = zs.reshape(bs, ls, -1), zco.reshape(bs, ls, -1), zcg.reshape(bs, ls, -1)
        oa = _attn_sample(zs, cache_k, cache_v, bias_s, sinks_flat[li * SWA_Q_HEADS:(li + 1) * SWA_Q_HEADS], li)
        ob, vn_s = _gmlp(zs, w_s_small, b_st_small, gvn, li, ls, ls, True)
        oc, s_s = _gla_sample(zs, zco, zcg, wg16, bg, gon, state, li)
        x1, h2 = mix_out(xs, oa, ob, oc, li, ms, 1)
        f = _ffn_sample(h2, w_up16, w_dn16, FFN_CHUNK)
        xs, hs = layer_out(x1, f, pe_s, li, ms, 1)
        ks_rows.append(zs[:, :, OFF_K:OFF_K + KV_WIDTH].reshape(bs, ls, SWA_KV_HEADS, HEAD_DIM))
        vs_rows.append(zs[:, :, OFF_V:OFF_V + KV_WIDTH].reshape(bs, ls, SWA_KV_HEADS, HEAD_DIM))
        ss_states.append(s_s.reshape(bs, GLA_HEADS, GLA_DK, GLA_DV).astype(state_gla.dtype))
        gv_rows.append(vn_s)

    return (xp.reshape(bp, lp, D_MODEL), xs.reshape(bs, ls, D_MODEL),
            jnp.stack(kp_rows), jnp.stack(vp_rows), jnp.stack(ks_rows), jnp.stack(vs_rows),
            jnp.stack(sp_states), jnp.stack(ss_states), jnp.stack(gv_rows))
```

```python
import functools
import math

import jax
import jax.numpy as jnp
from jax import lax
from jax.experimental import pallas as pl
from jax.experimental.pallas import tpu as pltpu

F32 = jnp.float32
BF16 = jnp.bfloat16

D_MODEL = 2048
DEPTH = 4
CHUNK = 64
HEAD_DIM = 64
SWA_Q_HEADS = 16
SWA_KV_HEADS = 4
SWA_GROUP = SWA_Q_HEADS // SWA_KV_HEADS
SWA_WIDTH = 1024
KV_WIDTH = 256
WINDOW = 128
REL_BUCKETS = 32
REL_MAX_DIST = 128
GMLP_GROUPS = 4
GMLP_WIDTH = 512
GMLP_CHUNK = 128
GLA_HEADS = 4
GLA_DK = 64
GLA_DV = 128
GLA_WIDTH = 512
GLA_GATE_RANK = 16
GLA_GATE_TEMP = 16.0
D_FF = 4 * D_MODEL
PLE_DIM = 256
NORM_EPS = 1e-6
LANES = 128
HALF = LANES // 2
BF16_ROWS = 16

OFF_Q, OFF_K, OFF_V = 0, 1024, 1280
OFF_BU, OFF_BV = 1536, 2048
OFF_CQ, OFF_CK, OFF_CV = 2560, 2816, 3072
Z_WIDTH = 3584
CG_PAD = LANES

LOG2E = math.log2(math.e)
MASK_VALUE = -1e30
VMEM_LIMIT = 52 * 1024 * 1024
FFN_VMEM_LIMIT = 60 * 1024 * 1024

FFN_ROWS = 1024
FFN_CHUNK = 1024
FFN_FIRST_CHUNK = 512
INPROJ_COLS = 1792
DENSE_ROWS = 512
OUTPROJ_SUB = 4
PLE_SUB = 2
ATTN_ROWS = 1024
GMLP_ROWS = 1024
ATTN_LOOKAHEAD = 2
GLA_ROWS = 1024
GLA_SUPER = 2 * CHUNK
GLA_SUB = 16


def _cparams(sem, vmem=VMEM_LIMIT):
    return pltpu.CompilerParams(dimension_semantics=sem, vmem_limit_bytes=vmem)


def _rms(x, g):
    return (x * lax.rsqrt(jnp.mean(x * x, axis=-1, keepdims=True) + NORM_EPS)) * g


def _nt(a, b):
    return lax.dot_general(a, b, (((1,), (1,)), ((), ())), preferred_element_type=F32)


def _tn(a, b):
    return lax.dot_general(a, b, (((0,), (0,)), ((), ())), preferred_element_type=F32)


def _prenorm_kernel(x_ref, g_ref, h_ref):
    h_ref[...] = _rms(x_ref[...], g_ref[...]).astype(BF16)


def _prenorm(x, g, li, tm):
    m = x.shape[0]
    return pl.pallas_call(
        _prenorm_kernel,
        grid=(m // tm,),
        in_specs=[pl.BlockSpec((tm, D_MODEL), lambda i: (i, 0)),
                  pl.BlockSpec((None, 1, D_MODEL), lambda i: (li, 0, 0))],
        out_specs=pl.BlockSpec((tm, D_MODEL), lambda i: (i, 0)),
        out_shape=jax.ShapeDtypeStruct((m, D_MODEL), BF16),
        compiler_params=_cparams(("parallel",)),
        name="prenorm",
    )(x, g)


def _inproj_kernel(h_ref, w_ref, wco_ref, wcg_ref, z_ref, zco_ref, zcg_ref):
    z_ref[...] = _nt(h_ref[...], w_ref[...])

    @pl.when(pl.program_id(1) == 0)
    def _():
        zco_ref[...] = _nt(h_ref[...], wco_ref[...])
        zcg_ref[...] = _nt(h_ref[...], wcg_ref[...])


def _inproj(h, w, wco, wcg, li, tm, tn):
    m = h.shape[0]
    extra = lambda width: pl.BlockSpec((None, width, D_MODEL), lambda i, j: (li, 0, 0))
    return pl.pallas_call(
        _inproj_kernel,
        grid=(m // tm, Z_WIDTH // tn),
        in_specs=[pl.BlockSpec((tm, D_MODEL), lambda i, j: (i, 0)),
                  pl.BlockSpec((None, tn, D_MODEL), lambda i, j: (li, j, 0)),
                  extra(GLA_WIDTH), extra(CG_PAD)],
        out_specs=[pl.BlockSpec((tm, tn), lambda i, j: (i, j)),
                   pl.BlockSpec((tm, GLA_WIDTH), lambda i, j: (i, 0)),
                   pl.BlockSpec((tm, CG_PAD), lambda i, j: (i, 0))],
        out_shape=[jax.ShapeDtypeStruct((m, Z_WIDTH), F32), jax.ShapeDtypeStruct((m, GLA_WIDTH), F32),
                   jax.ShapeDtypeStruct((m, CG_PAD), F32)],
        compiler_params=_cparams(("parallel", "arbitrary")),
        name="inproj",
    )(h, w, wco, wcg)


def _attn_prompt_kernel(q_ref, kp_ref, kc_ref, vp_ref, vc_ref, bias_ref, sink_ref, o_ref, *, nsub):
    first = pl.program_id(1) == 0
    kfull = jnp.concatenate([kp_ref[...], kc_ref[...]], axis=0)
    vfull = jnp.concatenate([vp_ref[...], vc_ref[...]], axis=0)
    rows = kfull.shape[0]
    lo_k = lax.broadcasted_iota(jnp.int32, (rows, LANES), 1) < HALF
    lo_q = lax.broadcasted_iota(jnp.int32, (WINDOW, LANES), 1) < HALF
    kdup, v_t = [], []
    for t in range(KV_WIDTH // LANES):
        kt = kfull[:, t * LANES:(t + 1) * LANES]
        kt_r = pltpu.roll(kt, HALF, 1)
        kdup.append(jnp.where(lo_k, kt, kt_r).astype(BF16))
        kdup.append(jnp.where(lo_k, kt_r, kt).astype(BF16))
        vt_t = vfull[:, t * LANES:(t + 1) * LANES].T
        v_t.append(vt_t[:HALF].astype(BF16))
        v_t.append(vt_t[HALF:].astype(BF16))
    ones = jnp.ones((BF16_ROWS, 2 * WINDOW), BF16)

    def scores(j, kv):
        qa = q_ref[j * WINDOW:(j + 1) * WINDOW, (2 * kv) * LANES:(2 * kv + 1) * LANES] * (HEAD_DIM ** -0.5 * LOG2E)
        qb = q_ref[j * WINDOW:(j + 1) * WINDOW, (2 * kv + 1) * LANES:(2 * kv + 2) * LANES] * (HEAD_DIM ** -0.5 * LOG2E)
        qm = jnp.concatenate([jnp.where(lo_q, qa, 0.0), jnp.where(lo_q, 0.0, qa),
                              jnp.where(lo_q, qb, 0.0), jnp.where(lo_q, 0.0, qb)], axis=0).astype(BF16)
        return _nt(kdup[kv][j * WINDOW:(j + 2) * WINDOW], qm)

    steps = [(j, kv) for j in range(nsub) for kv in range(SWA_KV_HEADS)]
    ahead = [scores(*steps[n]) for n in range(ATTN_LOOKAHEAD)]
    for n, (j, kv) in enumerate(steps):
        s_t = ahead.pop(0)
        if n + ATTN_LOOKAHEAD < len(steps):
            ahead.append(scores(*steps[n + ATTN_LOOKAHEAD]))
        keys = slice(j * WINDOW, (j + 2) * WINDOW)
        var = jnp.where(first, 0, 1) if j == 0 else 1
        sink = sink_ref[kv]
        e_cols, m_cols = [], []
        for hq in range(SWA_GROUP):
            cols = slice(hq * LANES, (hq + 1) * LANES)
            s_h = s_t[:, cols] + bias_ref[var, kv, :, cols]
            m_h = jnp.maximum(jnp.max(s_h, axis=0, keepdims=True), sink[:, cols])
            e_cols.append(jnp.exp2(s_h - m_h).astype(BF16))
            m_cols.append(m_h)
        e = jnp.concatenate(e_cols, axis=1)
        m = jnp.concatenate(m_cols, axis=1)
        lhs = jnp.concatenate([v_t[kv][:, keys], ones], axis=0)
        r = jnp.dot(lhs, e, preferred_element_type=F32)
        denom = r[HALF:HALF + 1] + jnp.exp2(sink - m)
        o_t = r[:HALF] * (1.0 / denom)
        for qt in range(2):
            pair = jnp.concatenate([o_t[:, (2 * qt) * LANES:(2 * qt + 1) * LANES],
                                    o_t[:, (2 * qt + 1) * LANES:(2 * qt + 2) * LANES]], axis=0)
            col = (2 * kv + qt) * LANES
            o_ref[j * WINDOW:(j + 1) * WINDOW, col:col + LANES] = pair.T.astype(o_ref.dtype)


def _attn_prompt(z, bias_t, sink_t, tl):
    b, l, _ = z.shape
    kcol, vcol = OFF_K // KV_WIDTH, OFF_V // KV_WIDTH
    per = tl // WINDOW
    prev = lambda col: pl.BlockSpec((None, WINDOW, KV_WIDTH),
                                    lambda bi, i: (bi, jnp.maximum(i * per - 1, 0), col))
    cur = lambda col: pl.BlockSpec((None, tl, KV_WIDTH), lambda bi, i: (bi, i, col))
    return pl.pallas_call(
        functools.partial(_attn_prompt_kernel, nsub=per),
        grid=(b, l // tl),
        in_specs=[pl.BlockSpec((None, tl, SWA_WIDTH), lambda bi, i: (bi, i, 0)),
                  prev(kcol), cur(kcol), prev(vcol), cur(vcol),
                  pl.BlockSpec(bias_t.shape, lambda bi, i: (0, 0, 0, 0)),
                  pl.BlockSpec(sink_t.shape, lambda bi, i: (0, 0, 0))],
        out_specs=pl.BlockSpec((None, tl, SWA_WIDTH), lambda bi, i: (bi, i, 0)),
        out_shape=jax.ShapeDtypeStruct((b, l, SWA_WIDTH), BF16),
        compiler_params=_cparams(("parallel", "arbitrary")),
        name="swa_prompt",
    )(z, z, z, z, z, bias_t, sink_t)


def _attn_sample_kernel(sink_ref, q_ref, kp_ref, kc_ref, vp_ref, vc_ref, bias_ref, o_ref):
    q = (q_ref[...] * (HEAD_DIM ** -0.5)).astype(BF16)
    k = jnp.concatenate([kp_ref[...], kc_ref[...]], axis=0)
    v = jnp.concatenate([vp_ref[...], vc_ref[...]], axis=0)
    lk = k.shape[0]
    lo = lax.broadcasted_iota(jnp.int32, (lk, LANES), 1) < HALF
    qlane = lax.broadcasted_iota(jnp.int32, (q.shape[0], LANES), 1)
    for t in range(KV_WIDTH // LANES):
        kt = k[:, t * LANES:(t + 1) * LANES]
        vt = v[:, t * LANES:(t + 1) * LANES]
        kt_r = pltpu.roll(kt, HALF, 1)
        vt_r = pltpu.roll(vt, HALF, 1)
        for half in range(2):
            kv = 2 * t + half
            k_in_lo, k_in_hi = (kt, kt_r) if half == 0 else (kt_r, kt)
            v_in_lo, v_in_hi = (vt, vt_r) if half == 0 else (vt_r, vt)
            k_half = (jnp.where(lo, k_in_lo, 0.0).astype(BF16), jnp.where(lo, 0.0, k_in_hi).astype(BF16))
            v_half = (jnp.where(lo, v_in_lo, 0.0).astype(BF16), jnp.where(lo, 0.0, v_in_hi).astype(BF16))
            for qt in (2 * kv, 2 * kv + 1):
                q2 = q[:, qt * LANES:(qt + 1) * LANES]
                acc = None
                inv = []
                for hh in range(2):
                    h = 2 * qt + hh
                    s = _nt(q2, k_half[hh]) + bias_ref[h]
                    sink = sink_ref[h]
                    m = jnp.maximum(jnp.max(s, axis=-1, keepdims=True), sink)
                    e = jnp.exp(s - m)
                    denom = jnp.sum(e, axis=-1, keepdims=True) + jnp.exp(sink - m)
                    inv.append(1.0 / denom)
                    pv = jnp.dot(e.astype(BF16), v_half[hh], preferred_element_type=F32)
                    acc = pv if acc is None else acc + pv
                o2 = acc * jnp.where(qlane < HALF, inv[0], inv[1])
                o_ref[:, qt * LANES:(qt + 1) * LANES] = o2.astype(o_ref.dtype)


def _attn_sample(z, cache_k, cache_v, bias, sinks, li):
    b, tq, _ = z.shape
    kcol, vcol = OFF_K // KV_WIDTH, OFF_V // KV_WIDTH
    cache = pl.BlockSpec((None, None, WINDOW, KV_WIDTH), lambda bi: (li, bi, 0, 0))
    return pl.pallas_call(
        _attn_sample_kernel,
        grid=(b,),
        in_specs=[pl.BlockSpec(memory_space=pltpu.SMEM),
                  pl.BlockSpec((None, tq, SWA_WIDTH), lambda bi: (bi, 0, 0)),
                  cache,
                  pl.BlockSpec((None, tq, KV_WIDTH), lambda bi: (bi, 0, kcol)),
                  cache,
                  pl.BlockSpec((None, tq, KV_WIDTH), lambda bi: (bi, 0, vcol)),
                  pl.BlockSpec(bias.shape, lambda bi: (0, 0, 0))],
        out_specs=pl.BlockSpec((None, tq, SWA_WIDTH), lambda bi: (bi, 0, 0)),
        out_shape=jax.ShapeDtypeStruct((b, tq, SWA_WIDTH), BF16),
        compiler_params=_cparams(("parallel",)),
        name="swa_sample",
    )(sinks, z, cache_k, z, cache_v, z, bias)


def _gmlp_kernel(bu_ref, bv_ref, w_ref, bs_ref, gv_ref, o_ref, *maybe_vn_ref, c, n_chunks):
    row = lax.broadcasted_iota(jnp.int32, (c, c), 0)
    col = lax.broadcasted_iota(jnp.int32, (c, c), 1)
    causal = (row // CHUNK) >= (col // CHUNK)
    v = jax.nn.gelu(bv_ref[...])
    vc = v - jnp.mean(v, axis=-1, keepdims=True)
    vn = (vc * lax.rsqrt(jnp.mean(vc * vc, axis=-1, keepdims=True) + NORM_EPS)) * gv_ref[...]
    if maybe_vn_ref:
        maybe_vn_ref[0][...] = vn
    u = jax.nn.gelu(bu_ref[...])
    vn16 = vn.astype(BF16)
    for g in range(GMLP_GROUPS):
        wg = jnp.where(causal, w_ref[g], 0.0).astype(BF16)
        bias = bs_ref[:, g:g + 1]
        cols = slice(g * LANES, (g + 1) * LANES)
        for ci in range(n_chunks):
            rows = slice(ci * c, (ci + 1) * c)
            sv = jnp.dot(wg, vn16[rows, cols], preferred_element_type=F32) + bias
            o_ref[rows, cols] = (u[rows, cols] * sv).astype(o_ref.dtype)


def _gmlp(z, w_s, b_st, g_vnorm, li, c, tl, want_vn):
    b, l, _ = z.shape
    ucol, vcol = OFF_BU // GMLP_WIDTH, OFF_BV // GMLP_WIDTH
    out_shape = [jax.ShapeDtypeStruct((b, l, GMLP_WIDTH), BF16)]
    out_specs = [pl.BlockSpec((None, tl, GMLP_WIDTH), lambda bi, i: (bi, i, 0))]
    if want_vn:
        out_shape.append(jax.ShapeDtypeStruct((b, l, GMLP_WIDTH), F32))
        out_specs.append(pl.BlockSpec((None, tl, GMLP_WIDTH), lambda bi, i: (bi, i, 0)))
    return pl.pallas_call(
        functools.partial(_gmlp_kernel, c=c, n_chunks=tl // c),
        grid=(b, l // tl),
        in_specs=[pl.BlockSpec((None, tl, GMLP_WIDTH), lambda bi, i: (bi, i, ucol)),
                  pl.BlockSpec((None, tl, GMLP_WIDTH), lambda bi, i: (bi, i, vcol)),
                  pl.BlockSpec((None, GMLP_GROUPS, c, c), lambda bi, i: (li, 0, 0, 0)),
                  pl.BlockSpec((None, c, GMLP_GROUPS), lambda bi, i: (li, 0, 0)),
                  pl.BlockSpec((None, 1, GMLP_WIDTH), lambda bi, i: (li, 0, 0))],
        out_specs=out_specs,
        out_shape=out_shape,
        compiler_params=_cparams(("parallel", "parallel")),
        name="gmlp",
    )(z, z, w_s, b_st, g_vnorm)


def _log_sigmoid(x):
    return jnp.minimum(x, 0.0) - jnp.log1p(jnp.exp(-jnp.abs(x)))


def _cumsum_rows(x):
    n = x.shape[0]
    row = lax.broadcasted_iota(jnp.int32, x.shape, 0)
    shift = 1
    while shift < n:
        x = x + jnp.where(row >= shift, pltpu.roll(x, shift, 0), 0.0)
        shift *= 2
    return x


def _gla_specs(tl, li):
    cq, ck, cv = OFF_CQ // KV_WIDTH, OFF_CK // KV_WIDTH, OFF_CV // GLA_WIDTH
    zrow = lambda width, col: pl.BlockSpec((None, tl, width), lambda bi, i: (bi, i, col))
    return [zrow(KV_WIDTH, cq), zrow(KV_WIDTH, ck), zrow(GLA_WIDTH, cv), zrow(GLA_WIDTH, 0), zrow(CG_PAD, 0),
            pl.BlockSpec((None, CG_PAD, KV_WIDTH), lambda bi, i: (li, 0, 0)),
            pl.BlockSpec((None, 1, KV_WIDTH), lambda bi, i: (li, 0, 0)),
            pl.BlockSpec((None, 1, GLA_WIDTH), lambda bi, i: (li, 0, 0))]


def _gla_sample_kernel(cq_ref, ck_ref, cv_ref, co_ref, cg_ref, wg_ref, bg_ref, gon_ref, s0_ref,
                       o_ref, sfin_ref):
    c = cq_ref.shape[0]
    lane = lax.broadcasted_iota(jnp.int32, (c, LANES), 1)
    lo = lane < HALF
    row = lax.broadcasted_iota(jnp.int32, (c, c), 0)
    col = lax.broadcasted_iota(jnp.int32, (c, c), 1)
    tril = row >= col
    gate = jnp.dot(cg_ref[...].astype(BF16), wg_ref[...], preferred_element_type=F32) + bg_ref[...]
    la = _log_sigmoid(gate) / GLA_GATE_TEMP
    bcum = _cumsum_rows(la)
    blast = bcum[c - 1:c, :]
    q = cq_ref[...] * (GLA_DK ** -0.5)
    k = ck_ref[...]
    qd = q * jnp.exp(bcum)
    ref = bcum[0:1, :]
    qr = q * jnp.exp(bcum - ref)
    kd = k * jnp.exp(ref - bcum)
    kl = k * jnp.exp(blast - bcum)
    dec = jnp.exp(blast)
    for j in range(2):
        tile = slice(j * LANES, (j + 1) * LANES)
        qd_t, qr_t, kd_t, kl_t = qd[:, tile], qr[:, tile], kd[:, tile], kl[:, tile]
        st = s0_ref[j].T
        st16 = st.astype(BF16)
        st_new = st * dec[:, tile]
        for hh in range(2):
            h = 2 * j + hh
            keep = lo if hh == 0 else jnp.logical_not(lo)
            hc = slice(h * GLA_DV, (h + 1) * GLA_DV)
            v_h = cv_ref[:, hc].astype(BF16)
            att = _nt(qr_t.astype(BF16), jnp.where(keep, kd_t, 0.0).astype(BF16))
            att = jnp.where(tril, att, 0.0)
            o_h = jnp.dot(att.astype(BF16), v_h, preferred_element_type=F32)
            o_h = o_h + _nt(jnp.where(keep, qd_t, 0.0).astype(BF16), st16)
            st_new = st_new + _tn(v_h, jnp.where(keep, kl_t, 0.0).astype(BF16))
            o_n = _rms(o_h, gon_ref[:, hc])
            o_ref[:, hc] = (o_n * jax.nn.silu(co_ref[:, hc])).astype(o_ref.dtype)
        sfin_ref[j] = st_new.T


def _gla_sample(z, zco, zcg, w_gate, b_gate, g_onorm, state, li):
    b, tl, _ = z.shape
    st_spec = pl.BlockSpec((None, 2, LANES, GLA_DV), lambda bi, i: (bi, 0, 0, 0))
    return pl.pallas_call(
        _gla_sample_kernel,
        grid=(b, 1),
        in_specs=_gla_specs(tl, li) + [pl.BlockSpec((None, None, 2, LANES, GLA_DV),
                                                    lambda bi, i: (li, bi, 0, 0, 0))],
        out_specs=[pl.BlockSpec((None, tl, GLA_WIDTH), lambda bi, i: (bi, i, 0)), st_spec],
        out_shape=[jax.ShapeDtypeStruct((b, tl, GLA_WIDTH), BF16),
                   jax.ShapeDtypeStruct((b, 2, LANES, GLA_DV), F32)],
        compiler_params=_cparams(("parallel", "arbitrary")),
        name="gla_sample",
    )(z, z, z, zco, zcg, w_gate, b_gate, g_onorm, state)


def _gla_prompt_kernel(cq_ref, ck_ref, cv_ref, co_ref, cg_ref, wg_ref, bg_ref, gon_ref,
                       o_ref, sfin_ref, st_ref):
    tl = cq_ref.shape[0]
    n_chunks = tl // CHUNK

    @pl.when(pl.program_id(1) == 0)
    def _():
        st_ref[...] = jnp.zeros_like(st_ref)

    gate = jnp.dot(cg_ref[...].astype(BF16), wg_ref[...], preferred_element_type=F32) + bg_ref[...]
    la = _log_sigmoid(gate) / GLA_GATE_TEMP
    la_hi = la.astype(BF16)
    la_lo = (la - la_hi.astype(F32)).astype(BF16)
    row = lax.broadcasted_iota(jnp.int32, (GLA_SUPER, GLA_SUPER), 0)
    col = lax.broadcasted_iota(jnp.int32, (GLA_SUPER, GLA_SUPER), 1)
    tri = ((row // CHUNK) == (col // CHUNK)) & (row >= col)
    tri16 = jnp.where(tri, 1.0, 0.0).astype(BF16)
    bcum = jnp.concatenate(
        [jnp.dot(tri16, la_hi[s * GLA_SUPER:(s + 1) * GLA_SUPER], preferred_element_type=F32)
         + jnp.dot(tri16, la_lo[s * GLA_SUPER:(s + 1) * GLA_SUPER], preferred_element_type=F32)
         for s in range(tl // GLA_SUPER)], axis=0)
    blast = [bcum[(ci + 1) * CHUNK - 1:(ci + 1) * CHUNK, :] for ci in range(n_chunks)]
    blast_b = jnp.concatenate([jnp.broadcast_to(bl, (CHUNK, KV_WIDTH)) for bl in blast], axis=0)
    k = ck_ref[...]
    q = cq_ref[...] * (GLA_DK ** -0.5)
    qd = q * jnp.exp(bcum)
    kl = k * jnp.exp(blast_b - bcum)
    ref = jnp.concatenate([jnp.broadcast_to(bcum[r0:r0 + 1], (GLA_SUB, KV_WIDTH))
                           for r0 in range(0, tl, GLA_SUB)], axis=0)
    q_same = q * jnp.exp(bcum - ref)
    k_same = k * jnp.exp(ref - bcum)
    sub_idx = (lax.broadcasted_iota(jnp.int32, (tl, KV_WIDTH), 0) % CHUNK) // GLA_SUB
    q_cross, k_cross = [], []
    for jj in range(CHUNK // GLA_SUB - 1):
        r = jnp.concatenate([jnp.broadcast_to(bcum[ci * CHUNK + (jj + 1) * GLA_SUB:ci * CHUNK + (jj + 1) * GLA_SUB + 1],
                                              (CHUNK, KV_WIDTH)) for ci in range(n_chunks)], axis=0)
        q_cross.append(q * jnp.exp(jnp.where(sub_idx > jj, bcum - r, MASK_VALUE)))
        k_cross.append(k * jnp.exp(jnp.where(sub_idx == jj, r - bcum, MASK_VALUE)))
    same_sub = (row // GLA_SUB) == (col // GLA_SUB)
    lo = lax.broadcasted_iota(jnp.int32, (tl, LANES), 1) < HALF

    keeps = [lo, jnp.logical_not(lo)]
    chunk_rows = [slice(ci * CHUNK, (ci + 1) * CHUNK) for ci in range(n_chunks)]
    super_rows = [slice(s * GLA_SUPER, (s + 1) * GLA_SUPER) for s in range(tl // GLA_SUPER)]
    stack = lambda pair, rows: jnp.concatenate([pair[0][rows], pair[1][rows]], axis=0)

    pairs = []
    for j in range(2):
        tile = slice(j * LANES, (j + 1) * LANES)
        hcs = [slice((2 * j + hh) * GLA_DV, (2 * j + hh + 1) * GLA_DV) for hh in range(2)]
        v = [cv_ref[:, hc].astype(BF16) for hc in hcs]
        qm = [jnp.where(kp, qd[:, tile], 0.0).astype(BF16) for kp in keeps]
        klm = [jnp.where(kp, kl[:, tile], 0.0).astype(BF16) for kp in keeps]
        q_same16 = q_same[:, tile].astype(BF16)
        k_same16 = [jnp.where(kp, k_same[:, tile], 0.0).astype(BF16) for kp in keeps]
        q_cross16 = jnp.concatenate([qc[:, tile] for qc in q_cross], axis=1).astype(BF16)
        k_cross16 = [jnp.concatenate([jnp.where(kp, kc[:, tile], 0.0) for kc in k_cross], axis=1).astype(BF16)
                     for kp in keeps]
        upd = [_tn(stack(v, rows), stack(klm, rows)) for rows in chunk_rows]
        att = [(_nt(q_same16[rows], stack(k_same16, rows)), _nt(q_cross16[rows], stack(k_cross16, rows)))
               for rows in super_rows]
        pairs.append(dict(tile=tile, hcs=hcs, v=v, qm=qm, upd=upd, att=att))
    for j, p in enumerate(pairs):
        st = st_ref[j]
        st16 = []
        for ci in range(n_chunks):
            st16.append(st.astype(BF16))
            st = st * jnp.exp(blast[ci][:, p["tile"]]) + p["upd"][ci]
        st_ref[j] = st
        p["inter"] = [_nt(stack(p["qm"], rows), st16[ci]) for ci, rows in enumerate(chunk_rows)]
    for p in pairs:
        intra = [[], []]
        for rows, (att_same, att_cross) in zip(super_rows, p["att"]):
            for hh in range(2):
                cols = slice(hh * GLA_SUPER, (hh + 1) * GLA_SUPER)
                a_h = jnp.where(tri, jnp.where(same_sub, att_same[:, cols], att_cross[:, cols]), 0.0).astype(BF16)
                intra[hh].append(jnp.dot(a_h, p["v"][hh][rows], preferred_element_type=F32))
        for hh in range(2):
            hc = p["hcs"][hh]
            o_h = jnp.concatenate(intra[hh], axis=0) + jnp.concatenate(
                [blk[hh * CHUNK:(hh + 1) * CHUNK] for blk in p["inter"]], axis=0)
            o_n = _rms(o_h, gon_ref[:, hc])
            o_ref[:, hc] = (o_n * jax.nn.silu(co_ref[:, hc])).astype(o_ref.dtype)

    @pl.when(pl.program_id(1) == pl.num_programs(1) - 1)
    def _():
        for j in range(2):
            sfin_ref[j] = st_ref[j].T


def _gla_prompt(z, zco, zcg, w_gate, b_gate, g_onorm, li, tl):
    b, l, _ = z.shape
    return pl.pallas_call(
        _gla_prompt_kernel,
        grid=(b, l // tl),
        in_specs=_gla_specs(tl, li),
        out_specs=[pl.BlockSpec((None, tl, GLA_WIDTH), lambda bi, i: (bi, i, 0)),
                   pl.BlockSpec((None, 2, LANES, GLA_DV), lambda bi, i: (bi, 0, 0, 0))],
        out_shape=[jax.ShapeDtypeStruct((b, l, GLA_WIDTH), BF16),
                   jax.ShapeDtypeStruct((b, 2, LANES, GLA_DV), F32)],
        scratch_shapes=[pltpu.VMEM((2, GLA_DV, LANES), F32)],
        compiler_params=_cparams(("parallel", "arbitrary")),
        name="gla_prompt",
    )(z, z, z, zco, zcg, w_gate, b_gate, g_onorm)


def _pipelined(nsub, matmul_stage, vector_stage):
    nxt = matmul_stage(0)
    for r in range(nsub):
        cur = nxt
        if r + 1 < nsub:
            nxt = matmul_stage(r + 1)
        vector_stage(r, cur)


def _outproj_kernel(x_ref, oa_ref, ob_ref, oc_ref, w_ref, g_ref, g2_ref, x1_ref, h2_ref, *, nsub):
    sub = x_ref.shape[0] // nsub

    def matmul_stage(r):
        rows = slice(r * sub, (r + 1) * sub)
        mix = jnp.dot(oa_ref[rows, :], w_ref[0:SWA_WIDTH, :], preferred_element_type=F32)
        mix = mix + jnp.dot(ob_ref[rows, :], w_ref[SWA_WIDTH:SWA_WIDTH + GMLP_WIDTH, :], preferred_element_type=F32)
        return mix + jnp.dot(oc_ref[rows, :], w_ref[SWA_WIDTH + GMLP_WIDTH:, :], preferred_element_type=F32)

    def vector_stage(r, mix):
        rows = slice(r * sub, (r + 1) * sub)
        x1 = x_ref[rows, :] + _rms(mix, g_ref[...])
        x1_ref[rows, :] = x1
        h2_ref[rows, :] = _rms(x1, g2_ref[...]).astype(BF16)

    _pipelined(nsub, matmul_stage, vector_stage)


def _outproj(x, oa, ob, oc, w, g, g2, li, tm, nsub):
    m = x.shape[0]
    row = lambda width: pl.BlockSpec((tm, width), lambda i: (i, 0))
    gain = pl.BlockSpec((None, 1, D_MODEL), lambda i: (li, 0, 0))
    return pl.pallas_call(
        functools.partial(_outproj_kernel, nsub=nsub),
        grid=(m // tm,),
        in_specs=[row(D_MODEL), row(SWA_WIDTH), row(GMLP_WIDTH), row(GLA_WIDTH),
                  pl.BlockSpec((None, D_MODEL, D_MODEL), lambda i: (li, 0, 0)), gain, gain],
        out_specs=[row(D_MODEL), row(D_MODEL)],
        out_shape=[jax.ShapeDtypeStruct((m, D_MODEL), F32), jax.ShapeDtypeStruct((m, D_MODEL), BF16)],
        compiler_params=_cparams(("parallel",)),
        name="outproj",
    )(x, oa, ob, oc, w, g, g2)


def _ffn_zero_first(f_ref):
    @pl.when(pl.program_id(1) == 0)
    def _():
        f_ref[...] = jnp.zeros_like(f_ref)


def _ffn_kernel(h_ref, wup_ref, wdn_ref, f_ref):
    _ffn_zero_first(f_ref)
    u = jnp.maximum(jnp.dot(h_ref[...], wup_ref[...], preferred_element_type=F32), 0.0)
    f_ref[...] += jnp.dot((u * u).astype(BF16), wdn_ref[...], preferred_element_type=F32)


def _ffn_rest_kernel(h_ref, wup_ref, wdn_ref, f_first_ref, f_ref):
    _ffn_kernel(h_ref, wup_ref, wdn_ref, f_ref)


def _ffn_first_kernel(h_ref, wup_ref, wdn_ref, f_ref, wup16_ref, wdn16_ref):
    _ffn_zero_first(f_ref)
    wup16_ref[...] = wup_ref[...].astype(BF16)
    wdn16_ref[...] = wdn_ref[...].astype(BF16)
    u = jnp.maximum(jnp.dot(h_ref[...], wup16_ref[...], preferred_element_type=F32), 0.0)
    f_ref[...] += jnp.dot((u * u).astype(BF16), wdn16_ref[...], preferred_element_type=F32)


def _ffn_specs(tm, tk, row_of):
    return [pl.BlockSpec((tm, D_MODEL), lambda i, k: (row_of(i), 0)),
            pl.BlockSpec((None, D_MODEL, tk), lambda i, k: (0, 0, k)),
            pl.BlockSpec((None, tk, D_MODEL), lambda i, k: (0, k, 0))]


def _ffn_prompt(h, w_up, w_down, li, tm, tk_first, tk):
    m = h.shape[0]
    f_shape = jax.ShapeDtypeStruct((m, D_MODEL), F32)
    f, wup16, wdn16 = pl.pallas_call(
        _ffn_first_kernel,
        grid=(1, D_FF // tk_first),
        in_specs=[pl.BlockSpec((tm, D_MODEL), lambda i, k: (0, 0)),
                  pl.BlockSpec((None, D_MODEL, tk_first), lambda i, k: (li, 0, k)),
                  pl.BlockSpec((None, tk_first, D_MODEL), lambda i, k: (li, k, 0))],
        out_specs=_ffn_specs(tm, tk_first, lambda i: 0),
        out_shape=[f_shape, jax.ShapeDtypeStruct((1, D_MODEL, D_FF), BF16),
                   jax.ShapeDtypeStruct((1, D_FF, D_MODEL), BF16)],
        compiler_params=_cparams(("arbitrary", "arbitrary"), FFN_VMEM_LIMIT),
        name="ffn_first",
    )(h, w_up, w_down)
    f = pl.pallas_call(
        _ffn_rest_kernel,
        grid=(m // tm - 1, D_FF // tk),
        in_specs=_ffn_specs(tm, tk, lambda i: i + 1) + [pl.BlockSpec(memory_space=pl.ANY)],
        out_specs=pl.BlockSpec((tm, D_MODEL), lambda i, k: (i + 1, 0)),
        out_shape=f_shape,
        input_output_aliases={3: 0},
        compiler_params=_cparams(("parallel", "arbitrary"), FFN_VMEM_LIMIT),
        name="ffn_rest",
    )(h, wup16, wdn16, f)
    return f, wup16, wdn16


def _ffn_sample(h, wup16, wdn16, tk):
    m = h.shape[0]
    return pl.pallas_call(
        _ffn_kernel,
        grid=(1, D_FF // tk),
        in_specs=_ffn_specs(m, tk, lambda i: i),
        out_specs=pl.BlockSpec((m, D_MODEL), lambda i, k: (i, 0)),
        out_shape=jax.ShapeDtypeStruct((m, D_MODEL), F32),
        compiler_params=_cparams(("parallel", "arbitrary"), FFN_VMEM_LIMIT),
        name="ffn_sample",
    )(h, wup16, wdn16)


def _ple_kernel(x1_ref, f_ref, pe_ref, gpost_ref, wp_ref, wg_ref, gnext_ref, x3_ref, *maybe_hn_ref, nsub):
    sub = x1_ref.shape[0] // nsub

    def matmul_stage(r):
        rows = slice(r * sub, (r + 1) * sub)
        proj = jnp.dot(pe_ref[rows, :].astype(BF16), wp_ref[...], preferred_element_type=F32)
        x2 = x1_ref[rows, :] + _rms(f_ref[rows, :], gpost_ref[...])
        gate = jnp.dot(x2.astype(BF16), wg_ref[...], preferred_element_type=F32)
        return x2, gate, proj

    def vector_stage(r, staged):
        rows = slice(r * sub, (r + 1) * sub)
        x2, gate, proj = staged
        x3 = x2 + proj * jax.nn.sigmoid(gate)
        x3_ref[rows, :] = x3
        if maybe_hn_ref:
            maybe_hn_ref[0][rows, :] = _rms(x3, gnext_ref[...]).astype(BF16)

    _pipelined(nsub, matmul_stage, vector_stage)


def _ple(x1, f, pe, gpost, wp, wg, gnext, li, tm, nsub, want_h):
    m = x1.shape[0]
    row = lambda: pl.BlockSpec((tm, D_MODEL), lambda i: (i, 0))
    ln = min(li + 1, DEPTH - 1)
    out_specs = [row()]
    out_shape = [jax.ShapeDtypeStruct((m, D_MODEL), F32)]
    if want_h:
        out_specs.append(row())
        out_shape.append(jax.ShapeDtypeStruct((m, D_MODEL), BF16))
    return pl.pallas_call(
        functools.partial(_ple_kernel, nsub=nsub),
        grid=(m // tm,),
        in_specs=[row(), row(),
                  pl.BlockSpec((None, tm, PLE_DIM), lambda i: (li, i, 0)),
                  pl.BlockSpec((None, 1, D_MODEL), lambda i: (li, 0, 0)),
                  pl.BlockSpec((None, PLE_DIM, D_MODEL), lambda i: (li, 0, 0)),
                  pl.BlockSpec((None, D_MODEL, D_MODEL), lambda i: (li, 0, 0)),
                  pl.BlockSpec((None, 1, D_MODEL), lambda i: (ln, 0, 0))],
        out_specs=out_specs,
        out_shape=out_shape,
        compiler_params=_cparams(("parallel",)),
        name="ple",
    )(x1, f, pe, gpost, wp, wg, gnext)


def _t5_bucket(rel):
    half = REL_BUCKETS // 2
    max_exact = half // 2
    n = -rel
    ret = jnp.where(n < 0, half, 0)
    n = jnp.abs(n)
    nf = jnp.maximum(n, 1).astype(jnp.float32)
    large = max_exact + (jnp.log(nf / max_exact) / math.log(REL_MAX_DIST / max_exact)
                         * (half - max_exact)).astype(jnp.int32)
    large = jnp.minimum(large, half - 1)
    return ret + jnp.where(n < max_exact, n, large)


def _bias_kernel(tab_ref, bkt_p_ref, valid_p_ref, bkt_s_ref, out_p_ref, out_s_ref):
    bkt_p = bkt_p_ref[...]
    bkt_s = bkt_s_ref[...]
    for h in range(SWA_Q_HEADS):
        def pick(b, accs):
            val = tab_ref[b, h]
            return (jnp.where(bkt_p == b, val, accs[0]), jnp.where(bkt_s == b, val, accs[1]))

        acc_p, acc_s = lax.fori_loop(0, REL_BUCKETS, pick,
                                     (jnp.zeros(bkt_p.shape, F32), jnp.zeros(bkt_s.shape, F32)))
        kv, hq = divmod(h, SWA_GROUP)
        for var in range(2):
            out_p_ref[var, kv, :, hq * LANES:(hq + 1) * LANES] = jnp.where(valid_p_ref[var] != 0, acc_p * LOG2E, MASK_VALUE)
        out_s_ref[h] = acc_s


def _bias_tables(rel_bias, dec_seq):
    qpos = jnp.arange(WINDOW)[None, :]
    kpos = jnp.arange(-WINDOW, WINDOW)[:, None]
    bkt_p = _t5_bucket(kpos - qpos).astype(jnp.int32)
    kchunk = jnp.floor_divide(kpos, CHUNK)
    qchunk = qpos // CHUNK
    valid = (kchunk >= qchunk - WINDOW // CHUNK) & (kchunk <= qchunk)
    valid_p = jnp.stack([valid & (kpos >= 0), valid]).astype(jnp.int32)
    bkt_s = _t5_bucket(jnp.arange(-WINDOW, dec_seq)[None, :] - jnp.arange(dec_seq)[:, None]).astype(jnp.int32)
    full = lambda a: pl.BlockSpec(a.shape, lambda: (0,) * a.ndim)
    out_shape = [jax.ShapeDtypeStruct((2, SWA_KV_HEADS, 2 * WINDOW, SWA_GROUP * LANES), F32),
                 jax.ShapeDtypeStruct((SWA_Q_HEADS, dec_seq, WINDOW + dec_seq), F32)]
    return pl.pallas_call(
        _bias_kernel,
        in_specs=[pl.BlockSpec(memory_space=pltpu.SMEM), full(bkt_p), full(valid_p), full(bkt_s)],
        out_specs=[full(s) for s in out_shape],
        out_shape=out_shape,
        name="rel_bias",
    )(rel_bias, bkt_p, valid_p, bkt_s)


def _prep_w_in(w):
    wt = jnp.swapaxes(w, 1, 2).astype(BF16)
    co = Z_WIDTH + GLA_GATE_RANK
    pad = jnp.zeros((w.shape[0], CG_PAD - GLA_GATE_RANK, w.shape[1]), BF16)
    return wt, wt[:, co:], jnp.concatenate([wt[:, Z_WIDTH:co], pad], axis=1)


def kernel(x_prompt, x_sample, cache_swa_k, cache_swa_v, state_gla, p_prompt, p_sample, w_in, w_gate, b_gate, rel_bias, attn_sinks, w_spatial, b_spatial, g_gmlp_vnorm, g_gla_onorm, w_out, g_mix_pre, g_mix_post, g_ffn_pre, g_ffn_post, w_up, w_down, w_ple, w_ple_gate):
    bp, lp, _ = x_prompt.shape
    bs, ls, _ = x_sample.shape
    mp, ms = bp * lp, bs * ls
    row3 = lambda a: a.reshape(DEPTH, 1, -1)

    bias_p, bias_s = _bias_tables(rel_bias, ls)
    sink_t = jnp.repeat(attn_sinks * LOG2E, LANES, axis=-1).reshape(DEPTH, SWA_KV_HEADS, 1, SWA_GROUP * LANES)
    sinks_flat = attn_sinks.reshape(-1)

    w_in16, w_co16, w_cg16 = _prep_w_in(w_in)
    w_out16, w_ple16, w_plg16 = w_out.astype(BF16), w_ple.astype(BF16), w_ple_gate.astype(BF16)
    wg16 = jnp.concatenate([w_gate, jnp.zeros((DEPTH, CG_PAD - GLA_GATE_RANK, KV_WIDTH), F32)], axis=1).astype(BF16)
    g_pre, g_post, g_fpre, g_fpost = row3(g_mix_pre), row3(g_mix_post), row3(g_ffn_pre), row3(g_ffn_post)
    bg, gon, gvn = row3(b_gate), row3(g_gla_onorm), row3(g_gmlp_vnorm)
    b_st = jnp.swapaxes(b_spatial, 1, 2)
    w_s_small, b_st_small = w_spatial[:, :, :ls, :ls], b_st[:, :ls, :]
    pe_p = p_prompt.reshape(DEPTH, mp, PLE_DIM)
    pe_s = p_sample.reshape(DEPTH, ms, PLE_DIM)
    cache_k = cache_swa_k.reshape(DEPTH, bs, WINDOW, KV_WIDTH)
    cache_v = cache_swa_v.reshape(DEPTH, bs, WINDOW, KV_WIDTH)
    state = state_gla.reshape(DEPTH, bs, 2, LANES, GLA_DV)

    def mix_out(x, oa, ob, oc, li, ts, nsub):
        m = x.shape[0]
        return _outproj(x, oa.reshape(m, -1), ob.reshape(m, -1), oc.reshape(m, -1), w_out16, g_post, g_fpre,
                        li, ts, nsub)

    def layer_out(x1, f, pe, li, ts, nsub):
        out = _ple(x1, f, pe, g_fpost, w_ple16, w_plg16, g_pre, li, ts, nsub, li + 1 < DEPTH)
        return out if li + 1 < DEPTH else (out[0], None)

    xp = x_prompt.reshape(mp, D_MODEL)
    xs = x_sample.reshape(ms, D_MODEL)
    hp = _prenorm(xp, g_pre, 0, FFN_ROWS)
    hs = _prenorm(xs, g_pre, 0, ms)
    kp_rows, vp_rows, ks_rows, vs_rows, sp_states, ss_states, gv_rows = [], [], [], [], [], [], []
    for li in range(DEPTH):
        zp, zco, zcg = _inproj(hp, w_in16, w_co16, w_cg16, li, FFN_ROWS, INPROJ_COLS)
        zp, zco, zcg = zp.reshape(bp, lp, -1), zco.reshape(bp, lp, -1), zcg.reshape(bp, lp, -1)
        oa = _attn_prompt(zp, bias_p, sink_t[li], ATTN_ROWS)
        ob, = _gmlp(zp, w_spatial, b_st, gvn, li, GMLP_CHUNK, GMLP_ROWS, False)
        oc, s_p = _gla_prompt(zp, zco, zcg, wg16, bg, gon, li, GLA_ROWS)
        x1, h2 = mix_out(xp, oa, ob, oc, li, DENSE_ROWS, OUTPROJ_SUB)
        f, w_up16, w_dn16 = _ffn_prompt(h2, w_up, w_down, li, FFN_ROWS, FFN_FIRST_CHUNK, FFN_CHUNK)
        xp, hp = layer_out(x1, f, pe_p, li, DENSE_ROWS, PLE_SUB)
        kp_rows.append(zp[:, lp - WINDOW:, OFF_K:OFF_K + KV_WIDTH].reshape(bp, WINDOW, SWA_KV_HEADS, HEAD_DIM))
        vp_rows.append(zp[:, lp - WINDOW:, OFF_V:OFF_V + KV_WIDTH].reshape(bp, WINDOW, SWA_KV_HEADS, HEAD_DIM))
        sp_states.append(s_p.reshape(bp, GLA_HEADS, GLA_DK, GLA_DV))

        zs, zco, zcg = _inproj(hs, w_in16, w_co16, w_cg16, li, ms, INPROJ_COLS)
        zs, zco, zcg = zs.reshape(bs, ls, -1), zco.reshape(bs, ls, -1), zcg.reshape(bs, ls, -1)
        oa = _attn_sample(zs, cache_k, cache_v, bias_s, sinks_flat[li * SWA_Q_HEADS:(li + 1) * SWA_Q_HEADS], li)
        ob, vn_s = _gmlp(zs, w_s_small, b_st_small, gvn, li, ls, ls, True)
        oc, s_s = _gla_sample(zs, zco, zcg, wg16, bg, gon, state, li)
        x1, h2 = mix_out(xs, oa, ob, oc, li, ms, 1)
        f = _ffn_sample(h2, w_up16, w_dn16, FFN_CHUNK)
        xs, hs = layer_out(x1, f, pe_s, li, ms, 1)
        ks_rows.append(zs[:, :, OFF_K:OFF_K + KV_WIDTH].reshape(bs, ls, SWA_KV_HEADS, HEAD_DIM))
        vs_rows.append(zs[:, :, OFF_V:OFF_V + KV_WIDTH].reshape(bs, ls, SWA_KV_HEADS, HEAD_DIM))
        ss_states.append(s_s.reshape(bs, GLA_HEADS, GLA_DK, GLA_DV).astype(state_gla.dtype))
        gv_rows.append(vn_s)

    return (xp.reshape(bp, lp, D_MODEL), xs.reshape(bs, ls, D_MODEL),
            jnp.stack(kp_rows), jnp.stack(vp_rows), jnp.stack(ks_rows), jnp.stack(vs_rows),
            jnp.stack(sp_states), jnp.stack(ss_states), jnp.stack(gv_rows))
```

```python
import functools
import math

import jax
import jax.numpy as jnp
from jax import lax
from jax.experimental import pallas as pl
from jax.experimental.pallas import tpu as pltpu

F32 = jnp.float32
BF16 = jnp.bfloat16

D_MODEL = 2048
DEPTH = 4
CHUNK = 64
HEAD_DIM = 64
SWA_Q_HEADS = 16
SWA_KV_HEADS = 4
SWA_GROUP = SWA_Q_HEADS // SWA_KV_HEADS
SWA_WIDTH = 1024
KV_WIDTH = 256
WINDOW = 128
REL_BUCKETS = 32
REL_MAX_DIST = 128
GMLP_GROUPS = 4
GMLP_WIDTH = 512
GMLP_CHUNK = 128
GLA_HEADS = 4
GLA_DK = 64
GLA_DV = 128
GLA_WIDTH = 512
GLA_GATE_RANK = 16
GLA_GATE_TEMP = 16.0
D_FF = 4 * D_MODEL
PLE_DIM = 256
NORM_EPS = 1e-6
LANES = 128
HALF = LANES // 2
BF16_ROWS = 16

OFF_Q, OFF_K, OFF_V = 0, 1024, 1280
OFF_BU, OFF_BV = 1536, 2048
OFF_CQ, OFF_CK, OFF_CV = 2560, 2816, 3072
Z_WIDTH = 3584
CG_PAD = LANES

LOG2E = math.log2(math.e)
MASK_VALUE = -1e30
VMEM_LIMIT = 52 * 1024 * 1024
FFN_VMEM_LIMIT = 60 * 1024 * 1024

FFN_ROWS = 1024
FFN_CHUNK = 1024
FFN_FIRST_CHUNK = 512
INPROJ_COLS = 1792
DENSE_ROWS = 512
OUTPROJ_SUB = 4
PLE_SUB = 2
ATTN_ROWS = 1024
GMLP_ROWS = 1024
ATTN_LOOKAHEAD = 2
GLA_ROWS = 1024
GLA_SUPER = 2 * CHUNK
GLA_SUB = 16


def _cparams(sem, vmem=VMEM_LIMIT):
    return pltpu.CompilerParams(dimension_semantics=sem, vmem_limit_bytes=vmem)


def _rms(x, g):
    return (x * lax.rsqrt(jnp.mean(x * x, axis=-1, keepdims=True) + NORM_EPS)) * g


def _nt(a, b):
    return lax.dot_general(a, b, (((1,), (1,)), ((), ())), preferred_element_type=F32)


def _tn(a, b):
    return lax.dot_general(a, b, (((0,), (0,)), ((), ())), preferred_element_type=F32)


def _prenorm_kernel(x_ref, g_ref, h_ref):
    h_ref[...] = _rms(x_ref[...], g_ref[...]).astype(BF16)


def _prenorm(x, g, li, tm):
    m = x.shape[0]
    return pl.pallas_call(
        _prenorm_kernel,
        grid=(m // tm,),
        in_specs=[pl.BlockSpec((tm, D_MODEL), lambda i: (i, 0)),
                  pl.BlockSpec((None, 1, D_MODEL), lambda i: (li, 0, 0))],
        out_specs=pl.BlockSpec((tm, D_MODEL), lambda i: (i, 0)),
        out_shape=jax.ShapeDtypeStruct((m, D_MODEL), BF16),
        compiler_params=_cparams(("parallel",)),
        name="prenorm",
    )(x, g)


def _inproj_kernel(h_ref, w_ref, wco_ref, wcg_ref, z_ref, zco_ref, zcg_ref):
    z_ref[...] = _nt(h_ref[...], w_ref[...])

    @pl.when(pl.program_id(1) == 0)
    def _():
        zco_ref[...] = _nt(h_ref[...], wco_ref[...])
        zcg_ref[...] = _nt(h_ref[...], wcg_ref[...])


def _inproj(h, w, wco, wcg, li, tm, tn):
    m = h.shape[0]
    extra = lambda width: pl.BlockSpec((None, width, D_MODEL), lambda i, j: (li, 0, 0))
    return pl.pallas_call(
        _inproj_kernel,
        grid=(m // tm, Z_WIDTH // tn),
        in_specs=[pl.BlockSpec((tm, D_MODEL), lambda i, j: (i, 0)),
                  pl.BlockSpec((None, tn, D_MODEL), lambda i, j: (li, j, 0)),
                  extra(GLA_WIDTH), extra(CG_PAD)],
        out_specs=[pl.BlockSpec((tm, tn), lambda i, j: (i, j)),
                   pl.BlockSpec((tm, GLA_WIDTH), lambda i, j: (i, 0)),
                   pl.BlockSpec((tm, CG_PAD), lambda i, j: (i, 0))],
        out_shape=[jax.ShapeDtypeStruct((m, Z_WIDTH), F32), jax.ShapeDtypeStruct((m, GLA_WIDTH), F32),
                   jax.ShapeDtypeStruct((m, CG_PAD), F32)],
        compiler_params=_cparams(("parallel", "arbitrary")),
        name="inproj",
    )(h, w, wco, wcg)


def _attn_prompt_kernel(q_ref, kp_ref, kc_ref, vp_ref, vc_ref, bias_ref, sink_ref, o_ref, *, nsub):
    first = pl.program_id(1) == 0
    kfull = jnp.concatenate([kp_ref[...], kc_ref[...]], axis=0)
    vfull = jnp.concatenate([vp_ref[...], vc_ref[...]], axis=0)
    rows = kfull.shape[0]
    lo_k = lax.broadcasted_iota(jnp.int32, (rows, LANES), 1) < HALF
    lo_q = lax.broadcasted_iota(jnp.int32, (WINDOW, LANES), 1) < HALF
    kdup, v_t = [], []
    for t in range(KV_WIDTH // LANES):
        kt = kfull[:, t * LANES:(t + 1) * LANES]
        kt_r = pltpu.roll(kt, HALF, 1)
        kdup.append(jnp.where(lo_k, kt, kt_r).astype(BF16))
        kdup.append(jnp.where(lo_k, kt_r, kt).astype(BF16))
        vt_t = vfull[:, t * LANES:(t + 1) * LANES].T
        v_t.append(vt_t[:HALF].astype(BF16))
        v_t.append(vt_t[HALF:].astype(BF16))
    ones = jnp.ones((BF16_ROWS, 2 * WINDOW), BF16)

    def scores(j, kv):
        qa = q_ref[j * WINDOW:(j + 1) * WINDOW, (2 * kv) * LANES:(2 * kv + 1) * LANES] * (HEAD_DIM ** -0.5 * LOG2E)
        qb = q_ref[j * WINDOW:(j + 1) * WINDOW, (2 * kv + 1) * LANES:(2 * kv + 2) * LANES] * (HEAD_DIM ** -0.5 * LOG2E)
        qm = jnp.concatenate([jnp.where(lo_q, qa, 0.0), jnp.where(lo_q, 0.0, qa),
                              jnp.where(lo_q, qb, 0.0), jnp.where(lo_q, 0.0, qb)], axis=0).astype(BF16)
        return _nt(kdup[kv][j * WINDOW:(j + 2) * WINDOW], qm)

    steps = [(j, kv) for j in range(nsub) for kv in range(SWA_KV_HEADS)]
    ahead = [scores(*steps[n]) for n in range(ATTN_LOOKAHEAD)]
    for n, (j, kv) in enumerate(steps):
        s_t = ahead.pop(0)
        if n + ATTN_LOOKAHEAD < len(steps):
            ahead.append(scores(*steps[n + ATTN_LOOKAHEAD]))
        keys = slice(j * WINDOW, (j + 2) * WINDOW)
        var = jnp.where(first, 0, 1) if j == 0 else 1
        sink = sink_ref[kv]
        e_cols, m_cols = [], []
        for hq in range(SWA_GROUP):
            cols = slice(hq * LANES, (hq + 1) * LANES)
            s_h = s_t[:, cols] + bias_ref[var, kv, :, cols]
            m_h = jnp.maximum(jnp.max(s_h, axis=0, keepdims=True), sink[:, cols])
            e_cols.append(jnp.exp2(s_h - m_h).astype(BF16))
            m_cols.append(m_h)
        e = jnp.concatenate(e_cols, axis=1)
        m = jnp.concatenate(m_cols, axis=1)
        lhs = jnp.concatenate([v_t[kv][:, keys], ones], axis=0)
        r = jnp.dot(lhs, e, preferred_element_type=F32)
        denom = r[HALF:HALF + 1] + jnp.exp2(sink - m)
        o_t = r[:HALF] * (1.0 / denom)
        for qt in range(2):
            pair = jnp.concatenate([o_t[:, (2 * qt) * LANES:(2 * qt + 1) * LANES],
                                    o_t[:, (2 * qt + 1) * LANES:(2 * qt + 2) * LANES]], axis=0)
            col = (2 * kv + qt) * LANES
            o_ref[j * WINDOW:(j + 1) * WINDOW, col:col + LANES] = pair.T.astype(o_ref.dtype)


def _attn_prompt(z, bias_t, sink_t, tl):
    b, l, _ = z.shape
    kcol, vcol = OFF_K // KV_WIDTH, OFF_V // KV_WIDTH
    per = tl // WINDOW
    prev = lambda col: pl.BlockSpec((None, WINDOW, KV_WIDTH),
                                    lambda bi, i: (bi, jnp.maximum(i * per - 1, 0), col))
    cur = lambda col: pl.BlockSpec((None, tl, KV_WIDTH), lambda bi, i: (bi, i, col))
    return pl.pallas_call(
        functools.partial(_attn_prompt_kernel, nsub=per),
        grid=(b, l // tl),
        in_specs=[pl.BlockSpec((None, tl, SWA_WIDTH), lambda bi, i: (bi, i, 0)),
                  prev(kcol), cur(kcol), prev(vcol), cur(vcol),
                  pl.BlockSpec(bias_t.shape, lambda bi, i: (0, 0, 0, 0)),
                  pl.BlockSpec(sink_t.shape, lambda bi, i: (0, 0, 0))],
        out_specs=pl.BlockSpec((None, tl, SWA_WIDTH), lambda bi, i: (bi, i, 0)),
        out_shape=jax.ShapeDtypeStruct((b, l, SWA_WIDTH), BF16),
        compiler_params=_cparams(("parallel", "arbitrary")),
        name="swa_prompt",
    )(z, z, z, z, z, bias_t, sink_t)


def _attn_sample_kernel(sink_ref, q_ref, kp_ref, kc_ref, vp_ref, vc_ref, bias_ref, o_ref):
    seqs = range(q_ref.shape[0])
    q = [(q_ref[b] * (HEAD_DIM ** -0.5)).astype(BF16) for b in seqs]
    k = [jnp.concatenate([kp_ref[b], kc_ref[b]], axis=0) for b in seqs]
    v = [jnp.concatenate([vp_ref[b], vc_ref[b]], axis=0) for b in seqs]
    lk = k[0].shape[0]
    lo = lax.broadcasted_iota(jnp.int32, (lk, LANES), 1) < HALF
    qlane = lax.broadcasted_iota(jnp.int32, (q[0].shape[0], LANES), 1)
    for t in range(KV_WIDTH // LANES):
        kt = [kb[:, t * LANES:(t + 1) * LANES] for kb in k]
        vt = [vb[:, t * LANES:(t + 1) * LANES] for vb in v]
        kt_r = [pltpu.roll(x, HALF, 1) for x in kt]
        vt_r = [pltpu.roll(x, HALF, 1) for x in vt]
        for half in range(2):
            kv = 2 * t + half
            k_in_lo, k_in_hi = (kt, kt_r) if half == 0 else (kt_r, kt)
            v_in_lo, v_in_hi = (vt, vt_r) if half == 0 else (vt_r, vt)
            k_half = [[jnp.where(lo, x, 0.0).astype(BF16) for x in k_in_lo],
                      [jnp.where(lo, 0.0, x).astype(BF16) for x in k_in_hi]]
            v_half = [[jnp.where(lo, x, 0.0).astype(BF16) for x in v_in_lo],
                      [jnp.where(lo, 0.0, x).astype(BF16) for x in v_in_hi]]
            for qt in (2 * kv, 2 * kv + 1):
                acc = [None for _ in seqs]
                inv = [[] for _ in seqs]
                for hh in range(2):
                    h = 2 * qt + hh
                    sink = sink_ref[h]
                    s = [_nt(q[b][:, qt * LANES:(qt + 1) * LANES], k_half[hh][b]) for b in seqs]
                    e = []
                    for b in seqs:
                        s_b = s[b] + bias_ref[h]
                        m = jnp.maximum(jnp.max(s_b, axis=-1, keepdims=True), sink)
                        e_b = jnp.exp(s_b - m)
                        inv[b].append(1.0 / (jnp.sum(e_b, axis=-1, keepdims=True) + jnp.exp(sink - m)))
                        e.append(e_b.astype(BF16))
                    for b in seqs:
                        pv = jnp.dot(e[b], v_half[hh][b], preferred_element_type=F32)
                        acc[b] = pv if acc[b] is None else acc[b] + pv
                for b in seqs:
                    o2 = acc[b] * jnp.where(qlane < HALF, inv[b][0], inv[b][1])
                    o_ref[b, :, qt * LANES:(qt + 1) * LANES] = o2.astype(o_ref.dtype)


def _attn_sample(z, cache_k, cache_v, bias, sinks, li):
    b, tq, _ = z.shape
    kcol, vcol = OFF_K // KV_WIDTH, OFF_V // KV_WIDTH
    cache = pl.BlockSpec((None, b, WINDOW, KV_WIDTH), lambda i: (li, 0, 0, 0))
    return pl.pallas_call(
        _attn_sample_kernel,
        grid=(1,),
        in_specs=[pl.BlockSpec(memory_space=pltpu.SMEM),
                  pl.BlockSpec((b, tq, SWA_WIDTH), lambda i: (0, 0, 0)),
                  cache,
                  pl.BlockSpec((b, tq, KV_WIDTH), lambda i: (0, 0, kcol)),
                  cache,
                  pl.BlockSpec((b, tq, KV_WIDTH), lambda i: (0, 0, vcol)),
                  pl.BlockSpec(bias.shape, lambda i: (0, 0, 0))],
        out_specs=pl.BlockSpec((b, tq, SWA_WIDTH), lambda i: (0, 0, 0)),
        out_shape=jax.ShapeDtypeStruct((b, tq, SWA_WIDTH), BF16),
        compiler_params=_cparams(("arbitrary",)),
        name="swa_sample",
    )(sinks, z, cache_k, z, cache_v, z, bias)


def _gmlp_kernel(bu_ref, bv_ref, w_ref, bs_ref, gv_ref, o_ref, *maybe_vn_ref, c, n_chunks):
    row = lax.broadcasted_iota(jnp.int32, (c, c), 0)
    col = lax.broadcasted_iota(jnp.int32, (c, c), 1)
    causal = (row // CHUNK) >= (col // CHUNK)
    v = jax.nn.gelu(bv_ref[...])
    vc = v - jnp.mean(v, axis=-1, keepdims=True)
    vn = (vc * lax.rsqrt(jnp.mean(vc * vc, axis=-1, keepdims=True) + NORM_EPS)) * gv_ref[...]
    if maybe_vn_ref:
        maybe_vn_ref[0][...] = vn
    u = jax.nn.gelu(bu_ref[...])
    vn16 = vn.astype(BF16)
    for g in range(GMLP_GROUPS):
        wg = jnp.where(causal, w_ref[g], 0.0).astype(BF16)
        bias = bs_ref[:, g:g + 1]
        cols = slice(g * LANES, (g + 1) * LANES)
        for ci in range(n_chunks):
            rows = slice(ci * c, (ci + 1) * c)
            sv = jnp.dot(wg, vn16[rows, cols], preferred_element_type=F32) + bias
            o_ref[rows, cols] = (u[rows, cols] * sv).astype(o_ref.dtype)


def _gmlp(z, w_s, b_st, g_vnorm, li, c, tl, want_vn):
    b, l, _ = z.shape
    ucol, vcol = OFF_BU // GMLP_WIDTH, OFF_BV // GMLP_WIDTH
    out_shape = [jax.ShapeDtypeStruct((b, l, GMLP_WIDTH), BF16)]
    out_specs = [pl.BlockSpec((None, tl, GMLP_WIDTH), lambda bi, i: (bi, i, 0))]
    if want_vn:
        out_shape.append(jax.ShapeDtypeStruct((b, l, GMLP_WIDTH), F32))
        out_specs.append(pl.BlockSpec((None, tl, GMLP_WIDTH), lambda bi, i: (bi, i, 0)))
    return pl.pallas_call(
        functools.partial(_gmlp_kernel, c=c, n_chunks=tl // c),
        grid=(b, l // tl),
        in_specs=[pl.BlockSpec((None, tl, GMLP_WIDTH), lambda bi, i: (bi, i, ucol)),
                  pl.BlockSpec((None, tl, GMLP_WIDTH), lambda bi, i: (bi, i, vcol)),
                  pl.BlockSpec((None, GMLP_GROUPS, c, c), lambda bi, i: (li, 0, 0, 0)),
                  pl.BlockSpec((None, c, GMLP_GROUPS), lambda bi, i: (li, 0, 0)),
                  pl.BlockSpec((None, 1, GMLP_WIDTH), lambda bi, i: (li, 0, 0))],
        out_specs=out_specs,
        out_shape=out_shape,
        compiler_params=_cparams(("parallel", "parallel")),
        name="gmlp",
    )(z, z, w_s, b_st, g_vnorm)


def _log_sigmoid(x):
    return jnp.minimum(x, 0.0) - jnp.log1p(jnp.exp(-jnp.abs(x)))


def _cumsum_rows(x):
    n = x.shape[0]
    row = lax.broadcasted_iota(jnp.int32, x.shape, 0)
    shift = 1
    while shift < n:
        x = x + jnp.where(row >= shift, pltpu.roll(x, shift, 0), 0.0)
        shift *= 2
    return x


def _gla_specs(tl, li):
    cq, ck, cv = OFF_CQ // KV_WIDTH, OFF_CK // KV_WIDTH, OFF_CV // GLA_WIDTH
    zrow = lambda width, col: pl.BlockSpec((None, tl, width), lambda bi, i: (bi, i, col))
    return [zrow(KV_WIDTH, cq), zrow(KV_WIDTH, ck), zrow(GLA_WIDTH, cv), zrow(GLA_WIDTH, 0), zrow(CG_PAD, 0),
            pl.BlockSpec((None, CG_PAD, KV_WIDTH), lambda bi, i: (li, 0, 0)),
            pl.BlockSpec((None, 1, KV_WIDTH), lambda bi, i: (li, 0, 0)),
            pl.BlockSpec((None, 1, GLA_WIDTH), lambda bi, i: (li, 0, 0))]


def _gla_sample_kernel(cq_ref, ck_ref, cv_ref, co_ref, cg_ref, wg_ref, bg_ref, gon_ref, s0_ref,
                       o_ref, sfin_ref):
    c = cq_ref.shape[0]
    lane = lax.broadcasted_iota(jnp.int32, (c, LANES), 1)
    lo = lane < HALF
    row = lax.broadcasted_iota(jnp.int32, (c, c), 0)
    col = lax.broadcasted_iota(jnp.int32, (c, c), 1)
    tril = row >= col
    gate = jnp.dot(cg_ref[...].astype(BF16), wg_ref[...], preferred_element_type=F32) + bg_ref[...]
    la = _log_sigmoid(gate) / GLA_GATE_TEMP
    bcum = _cumsum_rows(la)
    blast = bcum[c - 1:c, :]
    q = cq_ref[...] * (GLA_DK ** -0.5)
    k = ck_ref[...]
    qd = q * jnp.exp(bcum)
    ref = bcum[0:1, :]
    qr = q * jnp.exp(bcum - ref)
    kd = k * jnp.exp(ref - bcum)
    kl = k * jnp.exp(blast - bcum)
    dec = jnp.exp(blast)
    for j in range(2):
        tile = slice(j * LANES, (j + 1) * LANES)
        qd_t, qr_t, kd_t, kl_t = qd[:, tile], qr[:, tile], kd[:, tile], kl[:, tile]
        st = s0_ref[j].T
        st16 = st.astype(BF16)
        st_new = st * dec[:, tile]
        for hh in range(2):
            h = 2 * j + hh
            keep = lo if hh == 0 else jnp.logical_not(lo)
            hc = slice(h * GLA_DV, (h + 1) * GLA_DV)
            v_h = cv_ref[:, hc].astype(BF16)
            att = _nt(qr_t.astype(BF16), jnp.where(keep, kd_t, 0.0).astype(BF16))
            att = jnp.where(tril, att, 0.0)
            o_h = jnp.dot(att.astype(BF16), v_h, preferred_element_type=F32)
            o_h = o_h + _nt(jnp.where(keep, qd_t, 0.0).astype(BF16), st16)
            st_new = st_new + _tn(v_h, jnp.where(keep, kl_t, 0.0).astype(BF16))
            o_n = _rms(o_h, gon_ref[:, hc])
            o_ref[:, hc] = (o_n * jax.nn.silu(co_ref[:, hc])).astype(o_ref.dtype)
        sfin_ref[j] = st_new.T


def _gla_sample(z, zco, zcg, w_gate, b_gate, g_onorm, state, li):
    b, tl, _ = z.shape
    st_spec = pl.BlockSpec((None, 2, LANES, GLA_DV), lambda bi, i: (bi, 0, 0, 0))
    return pl.pallas_call(
        _gla_sample_kernel,
        grid=(b, 1),
        in_specs=_gla_specs(tl, li) + [pl.BlockSpec((None, None, 2, LANES, GLA_DV),
                                                    lambda bi, i: (li, bi, 0, 0, 0))],
        out_specs=[pl.BlockSpec((None, tl, GLA_WIDTH), lambda bi, i: (bi, i, 0)), st_spec],
        out_shape=[jax.ShapeDtypeStruct((b, tl, GLA_WIDTH), BF16),
                   jax.ShapeDtypeStruct((b, 2, LANES, GLA_DV), F32)],
        compiler_params=_cparams(("parallel", "arbitrary")),
        name="gla_sample",
    )(z, z, z, zco, zcg, w_gate, b_gate, g_onorm, state)


def _gla_prompt_kernel(cq_ref, ck_ref, cv_ref, co_ref, cg_ref, wg_ref, bg_ref, gon_ref,
                       o_ref, sfin_ref, st_ref):
    tl = cq_ref.shape[0]
    n_chunks = tl // CHUNK

    @pl.when(pl.program_id(1) == 0)
    def _():
        st_ref[...] = jnp.zeros_like(st_ref)

    gate = jnp.dot(cg_ref[...].astype(BF16), wg_ref[...], preferred_element_type=F32) + bg_ref[...]
    la = _log_sigmoid(gate) / GLA_GATE_TEMP
    la_hi = la.astype(BF16)
    la_lo = (la - la_hi.astype(F32)).astype(BF16)
    row = lax.broadcasted_iota(jnp.int32, (GLA_SUPER, GLA_SUPER), 0)
    col = lax.broadcasted_iota(jnp.int32, (GLA_SUPER, GLA_SUPER), 1)
    tri = ((row // CHUNK) == (col // CHUNK)) & (row >= col)
    tri16 = jnp.where(tri, 1.0, 0.0).astype(BF16)
    bcum = jnp.concatenate(
        [jnp.dot(tri16, la_hi[s * GLA_SUPER:(s + 1) * GLA_SUPER], preferred_element_type=F32)
         + jnp.dot(tri16, la_lo[s * GLA_SUPER:(s + 1) * GLA_SUPER], preferred_element_type=F32)
         for s in range(tl // GLA_SUPER)], axis=0)
    blast = [bcum[(ci + 1) * CHUNK - 1:(ci + 1) * CHUNK, :] for ci in range(n_chunks)]
    blast_b = jnp.concatenate([jnp.broadcast_to(bl, (CHUNK, KV_WIDTH)) for bl in blast], axis=0)
    k = ck_ref[...]
    q = cq_ref[...] * (GLA_DK ** -0.5)
    qd = q * jnp.exp(bcum)
    kl = k * jnp.exp(blast_b - bcum)
    ref = jnp.concatenate([jnp.broadcast_to(bcum[r0:r0 + 1], (GLA_SUB, KV_WIDTH))
                           for r0 in range(0, tl, GLA_SUB)], axis=0)
    q_same = q * jnp.exp(bcum - ref)
    k_same = k * jnp.exp(ref - bcum)
    sub_idx = (lax.broadcasted_iota(jnp.int32, (tl, KV_WIDTH), 0) % CHUNK) // GLA_SUB
    q_cross, k_cross = [], []
    for jj in range(CHUNK // GLA_SUB - 1):
        r = jnp.concatenate([jnp.broadcast_to(bcum[ci * CHUNK + (jj + 1) * GLA_SUB:ci * CHUNK + (jj + 1) * GLA_SUB + 1],
                                              (CHUNK, KV_WIDTH)) for ci in range(n_chunks)], axis=0)
        q_cross.append(q * jnp.exp(jnp.where(sub_idx > jj, bcum - r, MASK_VALUE)))
        k_cross.append(k * jnp.exp(jnp.where(sub_idx == jj, r - bcum, MASK_VALUE)))
    same_sub = (row // GLA_SUB) == (col // GLA_SUB)
    lo = lax.broadcasted_iota(jnp.int32, (tl, LANES), 1) < HALF

    keeps = [lo, jnp.logical_not(lo)]
    chunk_rows = [slice(ci * CHUNK, (ci + 1) * CHUNK) for ci in range(n_chunks)]
    super_rows = [slice(s * GLA_SUPER, (s + 1) * GLA_SUPER) for s in range(tl // GLA_SUPER)]
    stack = lambda pair, rows: jnp.concatenate([pair[0][rows], pair[1][rows]], axis=0)

    pairs = []
    for j in range(2):
        tile = slice(j * LANES, (j + 1) * LANES)
        hcs = [slice((2 * j + hh) * GLA_DV, (2 * j + hh + 1) * GLA_DV) for hh in range(2)]
        v = [cv_ref[:, hc].astype(BF16) for hc in hcs]
        qm = [jnp.where(kp, qd[:, tile], 0.0).astype(BF16) for kp in keeps]
        klm = [jnp.where(kp, kl[:, tile], 0.0).astype(BF16) for kp in keeps]
        q_same16 = q_same[:, tile].astype(BF16)
        k_same16 = [jnp.where(kp, k_same[:, tile], 0.0).astype(BF16) for kp in keeps]
        q_cross16 = jnp.concatenate([qc[:, tile] for qc in q_cross], axis=1).astype(BF16)
        k_cross16 = [jnp.concatenate([jnp.where(kp, kc[:, tile], 0.0) for kc in k_cross], axis=1).astype(BF16)
                     for kp in keeps]
        upd = [_tn(stack(v, rows), stack(klm, rows)) for rows in chunk_rows]
        att = [(_nt(q_same16[rows], stack(k_same16, rows)), _nt(q_cross16[rows], stack(k_cross16, rows)))
               for rows in super_rows]
        pairs.append(dict(tile=tile, hcs=hcs, v=v, qm=qm, upd=upd, att=att))
    for j, p in enumerate(pairs):
        st = st_ref[j]
        st16 = []
        for ci in range(n_chunks):
            st16.append(st.astype(BF16))
            st = st * jnp.exp(blast[ci][:, p["tile"]]) + p["upd"][ci]
        st_ref[j] = st
        p["inter"] = [_nt(stack(p["qm"], rows), st16[ci]) for ci, rows in enumerate(chunk_rows)]
    for p in pairs:
        intra = [[], []]
        for rows, (att_same, att_cross) in zip(super_rows, p["att"]):
            for hh in range(2):
                cols = slice(hh * GLA_SUPER, (hh + 1) * GLA_SUPER)
                a_h = jnp.where(tri, jnp.where(same_sub, att_same[:, cols], att_cross[:, cols]), 0.0).astype(BF16)
                intra[hh].append(jnp.dot(a_h, p["v"][hh][rows], preferred_element_type=F32))
        for hh in range(2):
            hc = p["hcs"][hh]
            o_h = jnp.concatenate(intra[hh], axis=0) + jnp.concatenate(
                [blk[hh * CHUNK:(hh + 1) * CHUNK] for blk in p["inter"]], axis=0)
            o_n = _rms(o_h, gon_ref[:, hc])
            o_ref[:, hc] = (o_n * jax.nn.silu(co_ref[:, hc])).astype(o_ref.dtype)

    @pl.when(pl.program_id(1) == pl.num_programs(1) - 1)
    def _():
        for j in range(2):
            sfin_ref[j] = st_ref[j].T


def _gla_prompt(z, zco, zcg, w_gate, b_gate, g_onorm, li, tl):
    b, l, _ = z.shape
    return pl.pallas_call(
        _gla_prompt_kernel,
        grid=(b, l // tl),
        in_specs=_gla_specs(tl, li),
        out_specs=[pl.BlockSpec((None, tl, GLA_WIDTH), lambda bi, i: (bi, i, 0)),
                   pl.BlockSpec((None, 2, LANES, GLA_DV), lambda bi, i: (bi, 0, 0, 0))],
        out_shape=[jax.ShapeDtypeStruct((b, l, GLA_WIDTH), BF16),
                   jax.ShapeDtypeStruct((b, 2, LANES, GLA_DV), F32)],
        scratch_shapes=[pltpu.VMEM((2, GLA_DV, LANES), F32)],
        compiler_params=_cparams(("parallel", "arbitrary")),
        name="gla_prompt",
    )(z, z, z, zco, zcg, w_gate, b_gate, g_onorm)


def _pipelined(nsub, matmul_stage, vector_stage):
    nxt = matmul_stage(0)
    for r in range(nsub):
        cur = nxt
        if r + 1 < nsub:
            nxt = matmul_stage(r + 1)
        vector_stage(r, cur)


def _outproj_kernel(x_ref, oa_ref, ob_ref, oc_ref, w_ref, g_ref, g2_ref, x1_ref, h2_ref, *, nsub):
    sub = x_ref.shape[0] // nsub

    def matmul_stage(r):
        rows = slice(r * sub, (r + 1) * sub)
        mix = jnp.dot(oa_ref[rows, :], w_ref[0:SWA_WIDTH, :], preferred_element_type=F32)
        mix = mix + jnp.dot(ob_ref[rows, :], w_ref[SWA_WIDTH:SWA_WIDTH + GMLP_WIDTH, :], preferred_element_type=F32)
        return mix + jnp.dot(oc_ref[rows, :], w_ref[SWA_WIDTH + GMLP_WIDTH:, :], preferred_element_type=F32)

    def vector_stage(r, mix):
        rows = slice(r * sub, (r + 1) * sub)
        x1 = x_ref[rows, :] + _rms(mix, g_ref[...])
        x1_ref[rows, :] = x1
        h2_ref[rows, :] = _rms(x1, g2_ref[...]).astype(BF16)

    _pipelined(nsub, matmul_stage, vector_stage)


def _outproj(x, oa, ob, oc, w, g, g2, li, tm, nsub):
    m = x.shape[0]
    row = lambda width: pl.BlockSpec((tm, width), lambda i: (i, 0))
    gain = pl.BlockSpec((None, 1, D_MODEL), lambda i: (li, 0, 0))
    return pl.pallas_call(
        functools.partial(_outproj_kernel, nsub=nsub),
        grid=(m // tm,),
        in_specs=[row(D_MODEL), row(SWA_WIDTH), row(GMLP_WIDTH), row(GLA_WIDTH),
                  pl.BlockSpec((None, D_MODEL, D_MODEL), lambda i: (li, 0, 0)), gain, gain],
        out_specs=[row(D_MODEL), row(D_MODEL)],
        out_shape=[jax.ShapeDtypeStruct((m, D_MODEL), F32), jax.ShapeDtypeStruct((m, D_MODEL), BF16)],
        compiler_params=_cparams(("parallel",)),
        name="outproj",
    )(x, oa, ob, oc, w, g, g2)


def _ffn_zero_first(f_ref):
    @pl.when(pl.program_id(1) == 0)
    def _():
        f_ref[...] = jnp.zeros_like(f_ref)


def _ffn_kernel(h_ref, wup_ref, wdn_ref, f_ref):
    _ffn_zero_first(f_ref)
    u = jnp.maximum(jnp.dot(h_ref[...], wup_ref[...], preferred_element_type=F32), 0.0)
    f_ref[...] += jnp.dot((u * u).astype(BF16), wdn_ref[...], preferred_element_type=F32)


def _ffn_rest_kernel(h_ref, wup_ref, wdn_ref, f_first_ref, f_ref):
    _ffn_kernel(h_ref, wup_ref, wdn_ref, f_ref)


def _ffn_first_kernel(h_ref, wup_ref, wdn_ref, f_ref, fs_ref, wup16_ref, wdn16_ref):
    @pl.when(pl.program_id(1) == 0)
    def _():
        f_ref[...] = jnp.zeros_like(f_ref)
        fs_ref[...] = jnp.zeros_like(fs_ref)

    wup16_ref[...] = wup_ref[...].astype(BF16)
    wdn16_ref[...] = wdn_ref[...].astype(BF16)
    u = jnp.maximum(jnp.dot(h_ref[...], wup16_ref[...], preferred_element_type=F32), 0.0)
    d = jnp.dot((u * u).astype(BF16), wdn16_ref[...], preferred_element_type=F32)
    tm = f_ref.shape[0]
    f_ref[...] += d[:tm]
    fs_ref[...] += d[tm:]


def _ffn_specs(tm, tk, row_of):
    return [pl.BlockSpec((tm, D_MODEL), lambda i, k: (row_of(i), 0)),
            pl.BlockSpec((None, D_MODEL, tk), lambda i, k: (0, 0, k)),
            pl.BlockSpec((None, tk, D_MODEL), lambda i, k: (0, k, 0))]


def _ffn(h, hs, w_up, w_down, li, tm, tk_first, tk):
    m, ms = h.shape[0], hs.shape[0]
    f_shape = jax.ShapeDtypeStruct((m, D_MODEL), F32)
    h_first = jnp.concatenate([h[:tm], hs], axis=0)
    f_spec, wup_spec, wdn_spec = _ffn_specs(tm, tk_first, lambda i: 0)
    f, fs, wup16, wdn16 = pl.pallas_call(
        _ffn_first_kernel,
        grid=(1, D_FF // tk_first),
        in_specs=[pl.BlockSpec((tm + ms, D_MODEL), lambda i, k: (0, 0)),
                  pl.BlockSpec((None, D_MODEL, tk_first), lambda i, k: (li, 0, k)),
                  pl.BlockSpec((None, tk_first, D_MODEL), lambda i, k: (li, k, 0))],
        out_specs=[f_spec, pl.BlockSpec((ms, D_MODEL), lambda i, k: (0, 0)), wup_spec, wdn_spec],
        out_shape=[f_shape, jax.ShapeDtypeStruct((ms, D_MODEL), F32),
                   jax.ShapeDtypeStruct((1, D_MODEL, D_FF), BF16), jax.ShapeDtypeStruct((1, D_FF, D_MODEL), BF16)],
        compiler_params=_cparams(("arbitrary", "arbitrary"), FFN_VMEM_LIMIT),
        name="ffn_first",
    )(h_first, w_up, w_down)
    f = pl.pallas_call(
        _ffn_rest_kernel,
        grid=(m // tm - 1, D_FF // tk),
        in_specs=_ffn_specs(tm, tk, lambda i: i + 1) + [pl.BlockSpec(memory_space=pl.ANY)],
        out_specs=pl.BlockSpec((tm, D_MODEL), lambda i, k: (i + 1, 0)),
        out_shape=f_shape,
        input_output_aliases={3: 0},
        compiler_params=_cparams(("parallel", "arbitrary"), FFN_VMEM_LIMIT),
        name="ffn_rest",
    )(h, wup16, wdn16, f)
    return f, fs


def _ple_kernel(x1_ref, f_ref, pe_ref, gpost_ref, wp_ref, wg_ref, gnext_ref, x3_ref, *maybe_hn_ref, nsub):
    sub = x1_ref.shape[0] // nsub

    def matmul_stage(r):
        rows = slice(r * sub, (r + 1) * sub)
        proj = jnp.dot(pe_ref[rows, :].astype(BF16), wp_ref[...], preferred_element_type=F32)
        x2 = x1_ref[rows, :] + _rms(f_ref[rows, :], gpost_ref[...])
        gate = jnp.dot(x2.astype(BF16), wg_ref[...], preferred_element_type=F32)
        return x2, gate, proj

    def vector_stage(r, staged):
        rows = slice(r * sub, (r + 1) * sub)
        x2, gate, proj = staged
        x3 = x2 + proj * jax.nn.sigmoid(gate)
        x3_ref[rows, :] = x3
        if maybe_hn_ref:
            maybe_hn_ref[0][rows, :] = _rms(x3, gnext_ref[...]).astype(BF16)

    _pipelined(nsub, matmul_stage, vector_stage)


def _ple(x1, f, pe, gpost, wp, wg, gnext, li, tm, nsub, want_h):
    m = x1.shape[0]
    row = lambda: pl.BlockSpec((tm, D_MODEL), lambda i: (i, 0))
    ln = min(li + 1, DEPTH - 1)
    out_specs = [row()]
    out_shape = [jax.ShapeDtypeStruct((m, D_MODEL), F32)]
    if want_h:
        out_specs.append(row())
        out_shape.append(jax.ShapeDtypeStruct((m, D_MODEL), BF16))
    return pl.pallas_call(
        functools.partial(_ple_kernel, nsub=nsub),
        grid=(m // tm,),
        in_specs=[row(), row(),
                  pl.BlockSpec((None, tm, PLE_DIM), lambda i: (li, i, 0)),
                  pl.BlockSpec((None, 1, D_MODEL), lambda i: (li, 0, 0)),
                  pl.BlockSpec((None, PLE_DIM, D_MODEL), lambda i: (li, 0, 0)),
                  pl.BlockSpec((None, D_MODEL, D_MODEL), lambda i: (li, 0, 0)),
                  pl.BlockSpec((None, 1, D_MODEL), lambda i: (ln, 0, 0))],
        out_specs=out_specs,
        out_shape=out_shape,
        compiler_params=_cparams(("parallel",)),
        name="ple",
    )(x1, f, pe, gpost, wp, wg, gnext)


def _t5_bucket(rel):
    half = REL_BUCKETS // 2
    max_exact = half // 2
    n = -rel
    ret = jnp.where(n < 0, half, 0)
    n = jnp.abs(n)
    nf = jnp.maximum(n, 1).astype(jnp.float32)
    large = max_exact + (jnp.log(nf / max_exact) / math.log(REL_MAX_DIST / max_exact)
                         * (half - max_exact)).astype(jnp.int32)
    large = jnp.minimum(large, half - 1)
    return ret + jnp.where(n < max_exact, n, large)


def _bias_kernel(tab_ref, bkt_p_ref, valid_p_ref, bkt_s_ref, out_p_ref, out_s_ref):
    bkt_p = bkt_p_ref[...]
    bkt_s = bkt_s_ref[...]
    for h in range(SWA_Q_HEADS):
        def pick(b, accs):
            val = tab_ref[b, h]
            return (jnp.where(bkt_p == b, val, accs[0]), jnp.where(bkt_s == b, val, accs[1]))

        acc_p, acc_s = lax.fori_loop(0, REL_BUCKETS, pick,
                                     (jnp.zeros(bkt_p.shape, F32), jnp.zeros(bkt_s.shape, F32)))
        kv, hq = divmod(h, SWA_GROUP)
        for var in range(2):
            out_p_ref[var, kv, :, hq * LANES:(hq + 1) * LANES] = jnp.where(valid_p_ref[var] != 0, acc_p * LOG2E, MASK_VALUE)
        out_s_ref[h] = acc_s


def _bias_tables(rel_bias, dec_seq):
    qpos = jnp.arange(WINDOW)[None, :]
    kpos = jnp.arange(-WINDOW, WINDOW)[:, None]
    bkt_p = _t5_bucket(kpos - qpos).astype(jnp.int32)
    kchunk = jnp.floor_divide(kpos, CHUNK)
    qchunk = qpos // CHUNK
    valid = (kchunk >= qchunk - WINDOW // CHUNK) & (kchunk <= qchunk)
    valid_p = jnp.stack([valid & (kpos >= 0), valid]).astype(jnp.int32)
    bkt_s = _t5_bucket(jnp.arange(-WINDOW, dec_seq)[None, :] - jnp.arange(dec_seq)[:, None]).astype(jnp.int32)
    full = lambda a: pl.BlockSpec(a.shape, lambda: (0,) * a.ndim)
    out_shape = [jax.ShapeDtypeStruct((2, SWA_KV_HEADS, 2 * WINDOW, SWA_GROUP * LANES), F32),
                 jax.ShapeDtypeStruct((SWA_Q_HEADS, dec_seq, WINDOW + dec_seq), F32)]
    return pl.pallas_call(
        _bias_kernel,
        in_specs=[pl.BlockSpec(memory_space=pltpu.SMEM), full(bkt_p), full(valid_p), full(bkt_s)],
        out_specs=[full(s) for s in out_shape],
        out_shape=out_shape,
        name="rel_bias",
    )(rel_bias, bkt_p, valid_p, bkt_s)


def _prep_w_in(w):
    wt = jnp.swapaxes(w, 1, 2).astype(BF16)
    co = Z_WIDTH + GLA_GATE_RANK
    pad = jnp.zeros((w.shape[0], CG_PAD - GLA_GATE_RANK, w.shape[1]), BF16)
    return wt, wt[:, co:], jnp.concatenate([wt[:, Z_WIDTH:co], pad], axis=1)


def kernel(x_prompt, x_sample, cache_swa_k, cache_swa_v, state_gla, p_prompt, p_sample, w_in, w_gate, b_gate, rel_bias, attn_sinks, w_spatial, b_spatial, g_gmlp_vnorm, g_gla_onorm, w_out, g_mix_pre, g_mix_post, g_ffn_pre, g_ffn_post, w_up, w_down, w_ple, w_ple_gate):
    bp, lp, _ = x_prompt.shape
    bs, ls, _ = x_sample.shape
    mp, ms = bp * lp, bs * ls
    row3 = lambda a: a.reshape(DEPTH, 1, -1)

    bias_p, bias_s = _bias_tables(rel_bias, ls)
    sink_t = jnp.repeat(attn_sinks * LOG2E, LANES, axis=-1).reshape(DEPTH, SWA_KV_HEADS, 1, SWA_GROUP * LANES)
    sinks_flat = attn_sinks.reshape(-1)

    w_in16, w_co16, w_cg16 = _prep_w_in(w_in)
    w_out16, w_ple16, w_plg16 = w_out.astype(BF16), w_ple.astype(BF16), w_ple_gate.astype(BF16)
    wg16 = jnp.concatenate([w_gate, jnp.zeros((DEPTH, CG_PAD - GLA_GATE_RANK, KV_WIDTH), F32)], axis=1).astype(BF16)
    g_pre, g_post, g_fpre, g_fpost = row3(g_mix_pre), row3(g_mix_post), row3(g_ffn_pre), row3(g_ffn_post)
    bg, gon, gvn = row3(b_gate), row3(g_gla_onorm), row3(g_gmlp_vnorm)
    b_st = jnp.swapaxes(b_spatial, 1, 2)
    w_s_small, b_st_small = w_spatial[:, :, :ls, :ls], b_st[:, :ls, :]
    pe_p = p_prompt.reshape(DEPTH, mp, PLE_DIM)
    pe_s = p_sample.reshape(DEPTH, ms, PLE_DIM)
    cache_k = cache_swa_k.reshape(DEPTH, bs, WINDOW, KV_WIDTH)
    cache_v = cache_swa_v.reshape(DEPTH, bs, WINDOW, KV_WIDTH)
    state = state_gla.reshape(DEPTH, bs, 2, LANES, GLA_DV)

    def mix_out(x, oa, ob, oc, li, ts, nsub):
        m = x.shape[0]
        return _outproj(x, oa.reshape(m, -1), ob.reshape(m, -1), oc.reshape(m, -1), w_out16, g_post, g_fpre,
                        li, ts, nsub)

    def layer_out(x1, f, pe, li, ts, nsub):
        out = _ple(x1, f, pe, g_fpost, w_ple16, w_plg16, g_pre, li, ts, nsub, li + 1 < DEPTH)
        return out if li + 1 < DEPTH else (out[0], None)

    xp = x_prompt.reshape(mp, D_MODEL)
    xs = x_sample.reshape(ms, D_MODEL)
    hp = _prenorm(xp, g_pre, 0, FFN_ROWS)
    hs = _prenorm(xs, g_pre, 0, ms)
    kp_rows, vp_rows, ks_rows, vs_rows, sp_states, ss_states, gv_rows = [], [], [], [], [], [], []
    for li in range(DEPTH):
        zp, zco, zcg = _inproj(hp, w_in16, w_co16, w_cg16, li, FFN_ROWS, INPROJ_COLS)
        zp, zco, zcg = zp.reshape(bp, lp, -1), zco.reshape(bp, lp, -1), zcg.reshape(bp, lp, -1)
        oa = _attn_prompt(zp, bias_p, sink_t[li], ATTN_ROWS)
        ob, = _gmlp(zp, w_spatial, b_st, gvn, li, GMLP_CHUNK, GMLP_ROWS, False)
        oc, s_p = _gla_prompt(zp, zco, zcg, wg16, bg, gon, li, GLA_ROWS)
        x1p, h2p = mix_out(xp, oa, ob, oc, li, DENSE_ROWS, OUTPROJ_SUB)
        kp_rows.append(zp[:, lp - WINDOW:, OFF_K:OFF_K + KV_WIDTH].reshape(bp, WINDOW, SWA_KV_HEADS, HEAD_DIM))
        vp_rows.append(zp[:, lp - WINDOW:, OFF_V:OFF_V + KV_WIDTH].reshape(bp, WINDOW, SWA_KV_HEADS, HEAD_DIM))
        sp_states.append(s_p.reshape(bp, GLA_HEADS, GLA_DK, GLA_DV))

        zs, zco, zcg = _inproj(hs, w_in16, w_co16, w_cg16, li, ms, INPROJ_COLS)
        zs, zco, zcg = zs.reshape(bs, ls, -1), zco.reshape(bs, ls, -1), zcg.reshape(bs, ls, -1)
        oa = _attn_sample(zs, cache_k, cache_v, bias_s, sinks_flat[li * SWA_Q_HEADS:(li + 1) * SWA_Q_HEADS], li)
        ob, vn_s = _gmlp(zs, w_s_small, b_st_small, gvn, li, ls, ls, True)
        oc, s_s = _gla_sample(zs, zco, zcg, wg16, bg, gon, state, li)
        x1s, h2s = mix_out(xs, oa, ob, oc, li, ms, 1)

        fp, fs = _ffn(h2p, h2s, w_up, w_down, li, FFN_ROWS, FFN_FIRST_CHUNK, FFN_CHUNK)
        xp, hp = layer_out(x1p, fp, pe_p, li, DENSE_ROWS, PLE_SUB)
        xs, hs = layer_out(x1s, fs, pe_s, li, ms, 1)
        ks_rows.append(zs[:, :, OFF_K:OFF_K + KV_WIDTH].reshape(bs, ls, SWA_KV_HEADS, HEAD_DIM))
        vs_rows.append(zs[:, :, OFF_V:OFF_V + KV_WIDTH].reshape(bs, ls, SWA_KV_HEADS, HEAD_DIM))
        ss_states.append(s_s.reshape(bs, GLA_HEADS, GLA_DK, GLA_DV).astype(state_gla.dtype))
        gv_rows.append(vn_s)

    return (xp.reshape(bp, lp, D_MODEL), xs.reshape(bs, ls, D_MODEL),
            jnp.stack(kp_rows), jnp.stack(vp_rows), jnp.stack(ks_rows), jnp.stack(vs_rows),
            jnp.stack(sp_states), jnp.stack(ss_states), jnp.stack(gv_rows))
```

```python
import functools
import math

import jax
import jax.numpy as jnp
from jax import lax
from jax.experimental import pallas as pl
from jax.experimental.pallas import tpu as pltpu

F32 = jnp.float32
BF16 = jnp.bfloat16

D_MODEL = 2048
DEPTH = 4
CHUNK = 64
HEAD_DIM = 64
SWA_Q_HEADS = 16
SWA_KV_HEADS = 4
SWA_GROUP = SWA_Q_HEADS // SWA_KV_HEADS
SWA_WIDTH = 1024
KV_WIDTH = 256
WINDOW = 128
REL_BUCKETS = 32
REL_MAX_DIST = 128
GMLP_GROUPS = 4
GMLP_WIDTH = 512
GMLP_CHUNK = 128
GLA_HEADS = 4
GLA_DK = 64
GLA_DV = 128
GLA_WIDTH = 512
GLA_GATE_RANK = 16
GLA_GATE_TEMP = 16.0
D_FF = 4 * D_MODEL
PLE_DIM = 256
NORM_EPS = 1e-6
LANES = 128
HALF = LANES // 2
BF16_ROWS = 16

OFF_Q, OFF_K, OFF_V = 0, 1024, 1280
OFF_BU, OFF_BV = 1536, 2048
OFF_CQ, OFF_CK, OFF_CV = 2560, 2816, 3072
Z_WIDTH = 3584
CG_PAD = LANES

LOG2E = math.log2(math.e)
MASK_VALUE = -1e30
VMEM_LIMIT = 52 * 1024 * 1024
FFN_VMEM_LIMIT = 60 * 1024 * 1024

FFN_ROWS = 1024
FFN_CHUNK = 1024
FFN_FIRST_CHUNK = 512
INPROJ_COLS = 1792
DENSE_ROWS = 512
OUTPROJ_SUB = 4
PLE_SUB = 2
ATTN_ROWS = 1024
GMLP_ROWS = 1024
ATTN_LOOKAHEAD = 2
GLA_ROWS = 1024
GLA_SUPER = 2 * CHUNK
GLA_SUB = 16


def _cparams(sem, vmem=VMEM_LIMIT):
    return pltpu.CompilerParams(dimension_semantics=sem, vmem_limit_bytes=vmem)


def _rms(x, g):
    return (x * lax.rsqrt(jnp.mean(x * x, axis=-1, keepdims=True) + NORM_EPS)) * g


def _nt(a, b):
    return lax.dot_general(a, b, (((1,), (1,)), ((), ())), preferred_element_type=F32)


def _tn(a, b):
    return lax.dot_general(a, b, (((0,), (0,)), ((), ())), preferred_element_type=F32)


def _prenorm_kernel(x_ref, g_ref, h_ref):
    h_ref[...] = _rms(x_ref[...], g_ref[...]).astype(BF16)


def _prenorm(x, g, li, tm):
    m = x.shape[0]
    return pl.pallas_call(
        _prenorm_kernel,
        grid=(m // tm,),
        in_specs=[pl.BlockSpec((tm, D_MODEL), lambda i: (i, 0)),
                  pl.BlockSpec((None, 1, D_MODEL), lambda i: (li, 0, 0))],
        out_specs=pl.BlockSpec((tm, D_MODEL), lambda i: (i, 0)),
        out_shape=jax.ShapeDtypeStruct((m, D_MODEL), BF16),
        compiler_params=_cparams(("parallel",)),
        name="prenorm",
    )(x, g)


def _inproj_kernel(h_ref, w_ref, wco_ref, wcg_ref, z_ref, zco_ref, zcg_ref):
    z_ref[...] = _nt(h_ref[...], w_ref[...])

    @pl.when(pl.program_id(1) == 0)
    def _():
        zco_ref[...] = _nt(h_ref[...], wco_ref[...])
        zcg_ref[...] = _nt(h_ref[...], wcg_ref[...])


def _inproj(h, w, wco, wcg, li, tm, tn):
    m = h.shape[0]
    extra = lambda width: pl.BlockSpec((None, width, D_MODEL), lambda i, j: (li, 0, 0))
    return pl.pallas_call(
        _inproj_kernel,
        grid=(m // tm, Z_WIDTH // tn),
        in_specs=[pl.BlockSpec((tm, D_MODEL), lambda i, j: (i, 0)),
                  pl.BlockSpec((None, tn, D_MODEL), lambda i, j: (li, j, 0)),
                  extra(GLA_WIDTH), extra(CG_PAD)],
        out_specs=[pl.BlockSpec((tm, tn), lambda i, j: (i, j)),
                   pl.BlockSpec((tm, GLA_WIDTH), lambda i, j: (i, 0)),
                   pl.BlockSpec((tm, CG_PAD), lambda i, j: (i, 0))],
        out_shape=[jax.ShapeDtypeStruct((m, Z_WIDTH), F32), jax.ShapeDtypeStruct((m, GLA_WIDTH), F32),
                   jax.ShapeDtypeStruct((m, CG_PAD), F32)],
        compiler_params=_cparams(("parallel", "arbitrary")),
        name="inproj",
    )(h, w, wco, wcg)


def _attn_prompt_kernel(q_ref, kp_ref, kc_ref, vp_ref, vc_ref, bias_ref, sink_ref, o_ref, *, nsub):
    first = pl.program_id(1) == 0
    kfull = jnp.concatenate([kp_ref[...], kc_ref[...]], axis=0)
    vfull = jnp.concatenate([vp_ref[...], vc_ref[...]], axis=0)
    rows = kfull.shape[0]
    lo_k = lax.broadcasted_iota(jnp.int32, (rows, LANES), 1) < HALF
    lo_q = lax.broadcasted_iota(jnp.int32, (WINDOW, LANES), 1) < HALF
    kdup, v_t = [], []
    for t in range(KV_WIDTH // LANES):
        kt = kfull[:, t * LANES:(t + 1) * LANES]
        kt_r = pltpu.roll(kt, HALF, 1)
        kdup.append(jnp.where(lo_k, kt, kt_r).astype(BF16))
        kdup.append(jnp.where(lo_k, kt_r, kt).astype(BF16))
        vt_t = vfull[:, t * LANES:(t + 1) * LANES].T
        v_t.append(vt_t[:HALF].astype(BF16))
        v_t.append(vt_t[HALF:].astype(BF16))
    ones = jnp.ones((BF16_ROWS, 2 * WINDOW), BF16)

    def scores(j, kv):
        qa = q_ref[j * WINDOW:(j + 1) * WINDOW, (2 * kv) * LANES:(2 * kv + 1) * LANES] * (HEAD_DIM ** -0.5 * LOG2E)
        qb = q_ref[j * WINDOW:(j + 1) * WINDOW, (2 * kv + 1) * LANES:(2 * kv + 2) * LANES] * (HEAD_DIM ** -0.5 * LOG2E)
        qm = jnp.concatenate([jnp.where(lo_q, qa, 0.0), jnp.where(lo_q, 0.0, qa),
                              jnp.where(lo_q, qb, 0.0), jnp.where(lo_q, 0.0, qb)], axis=0).astype(BF16)
        return _nt(kdup[kv][j * WINDOW:(j + 2) * WINDOW], qm)

    steps = [(j, kv) for j in range(nsub) for kv in range(SWA_KV_HEADS)]
    ahead = [scores(*steps[n]) for n in range(ATTN_LOOKAHEAD)]
    for n, (j, kv) in enumerate(steps):
        s_t = ahead.pop(0)
        if n + ATTN_LOOKAHEAD < len(steps):
            ahead.append(scores(*steps[n + ATTN_LOOKAHEAD]))
        keys = slice(j * WINDOW, (j + 2) * WINDOW)
        var = jnp.where(first, 0, 1) if j == 0 else 1
        sink = sink_ref[kv]
        e_cols, m_cols = [], []
        for hq in range(SWA_GROUP):
            cols = slice(hq * LANES, (hq + 1) * LANES)
            s_h = s_t[:, cols] + bias_ref[var, kv, :, cols]
            m_h = jnp.maximum(jnp.max(s_h, axis=0, keepdims=True), sink[:, cols])
            e_cols.append(jnp.exp2(s_h - m_h).astype(BF16))
            m_cols.append(m_h)
        e = jnp.concatenate(e_cols, axis=1)
        m = jnp.concatenate(m_cols, axis=1)
        lhs = jnp.concatenate([v_t[kv][:, keys], ones], axis=0)
        r = jnp.dot(lhs, e, preferred_element_type=F32)
        denom = r[HALF:HALF + 1] + jnp.exp2(sink - m)
        o_t = r[:HALF] * (1.0 / denom)
        for qt in range(2):
            pair = jnp.concatenate([o_t[:, (2 * qt) * LANES:(2 * qt + 1) * LANES],
                                    o_t[:, (2 * qt + 1) * LANES:(2 * qt + 2) * LANES]], axis=0)
            col = (2 * kv + qt) * LANES
            o_ref[j * WINDOW:(j + 1) * WINDOW, col:col + LANES] = pair.T.astype(o_ref.dtype)


def _attn_prompt(z, bias_t, sink_t, tl):
    b, l, _ = z.shape
    kcol, vcol = OFF_K // KV_WIDTH, OFF_V // KV_WIDTH
    per = tl // WINDOW
    prev = lambda col: pl.BlockSpec((None, WINDOW, KV_WIDTH),
                                    lambda bi, i: (bi, jnp.maximum(i * per - 1, 0), col))
    cur = lambda col: pl.BlockSpec((None, tl, KV_WIDTH), lambda bi, i: (bi, i, col))
    return pl.pallas_call(
        functools.partial(_attn_prompt_kernel, nsub=per),
        grid=(b, l // tl),
        in_specs=[pl.BlockSpec((None, tl, SWA_WIDTH), lambda bi, i: (bi, i, 0)),
                  prev(kcol), cur(kcol), prev(vcol), cur(vcol),
                  pl.BlockSpec(bias_t.shape, lambda bi, i: (0, 0, 0, 0)),
                  pl.BlockSpec(sink_t.shape, lambda bi, i: (0, 0, 0))],
        out_specs=pl.BlockSpec((None, tl, SWA_WIDTH), lambda bi, i: (bi, i, 0)),
        out_shape=jax.ShapeDtypeStruct((b, l, SWA_WIDTH), BF16),
        compiler_params=_cparams(("parallel", "arbitrary")),
        name="swa_prompt",
    )(z, z, z, z, z, bias_t, sink_t)


def _attn_sample_kernel(sink_ref, q_ref, kp_ref, kc_ref, vp_ref, vc_ref, bias_ref, o_ref):
    seqs = range(q_ref.shape[0])
    q = [(q_ref[b] * (HEAD_DIM ** -0.5)).astype(BF16) for b in seqs]
    k = [jnp.concatenate([kp_ref[b], kc_ref[b]], axis=0) for b in seqs]
    v = [jnp.concatenate([vp_ref[b], vc_ref[b]], axis=0) for b in seqs]
    lk = k[0].shape[0]
    lo = lax.broadcasted_iota(jnp.int32, (lk, LANES), 1) < HALF
    qlane = lax.broadcasted_iota(jnp.int32, (q[0].shape[0], LANES), 1)
    for t in range(KV_WIDTH // LANES):
        kt = [kb[:, t * LANES:(t + 1) * LANES] for kb in k]
        vt = [vb[:, t * LANES:(t + 1) * LANES] for vb in v]
        kt_r = [pltpu.roll(x, HALF, 1) for x in kt]
        vt_r = [pltpu.roll(x, HALF, 1) for x in vt]
        for half in range(2):
            kv = 2 * t + half
            k_in_lo, k_in_hi = (kt, kt_r) if half == 0 else (kt_r, kt)
            v_in_lo, v_in_hi = (vt, vt_r) if half == 0 else (vt_r, vt)
            k_half = [[jnp.where(lo, x, 0.0).astype(BF16) for x in k_in_lo],
                      [jnp.where(lo, 0.0, x).astype(BF16) for x in k_in_hi]]
            v_half = [[jnp.where(lo, x, 0.0).astype(BF16) for x in v_in_lo],
                      [jnp.where(lo, 0.0, x).astype(BF16) for x in v_in_hi]]
            for qt in (2 * kv, 2 * kv + 1):
                acc = [None for _ in seqs]
                inv = [[] for _ in seqs]
                for hh in range(2):
                    h = 2 * qt + hh
                    sink = sink_ref[h]
                    s = [_nt(q[b][:, qt * LANES:(qt + 1) * LANES], k_half[hh][b]) for b in seqs]
                    e = []
                    for b in seqs:
                        s_b = s[b] + bias_ref[h]
                        m = jnp.maximum(jnp.max(s_b, axis=-1, keepdims=True), sink)
                        e_b = jnp.exp(s_b - m)
                        inv[b].append(1.0 / (jnp.sum(e_b, axis=-1, keepdims=True) + jnp.exp(sink - m)))
                        e.append(e_b.astype(BF16))
                    for b in seqs:
                        pv = jnp.dot(e[b], v_half[hh][b], preferred_element_type=F32)
                        acc[b] = pv if acc[b] is None else acc[b] + pv
                for b in seqs:
                    o2 = acc[b] * jnp.where(qlane < HALF, inv[b][0], inv[b][1])
                    o_ref[b, :, qt * LANES:(qt + 1) * LANES] = o2.astype(o_ref.dtype)


def _attn_sample(z, cache_k, cache_v, bias, sinks, li):
    b, tq, _ = z.shape
    kcol, vcol = OFF_K // KV_WIDTH, OFF_V // KV_WIDTH
    cache = pl.BlockSpec((None, b, WINDOW, KV_WIDTH), lambda i: (li, 0, 0, 0))
    return pl.pallas_call(
        _attn_sample_kernel,
        grid=(1,),
        in_specs=[pl.BlockSpec(memory_space=pltpu.SMEM),
                  pl.BlockSpec((b, tq, SWA_WIDTH), lambda i: (0, 0, 0)),
                  cache,
                  pl.BlockSpec((b, tq, KV_WIDTH), lambda i: (0, 0, kcol)),
                  cache,
                  pl.BlockSpec((b, tq, KV_WIDTH), lambda i: (0, 0, vcol)),
                  pl.BlockSpec(bias.shape, lambda i: (0, 0, 0))],
        out_specs=pl.BlockSpec((b, tq, SWA_WIDTH), lambda i: (0, 0, 0)),
        out_shape=jax.ShapeDtypeStruct((b, tq, SWA_WIDTH), BF16),
        compiler_params=_cparams(("arbitrary",)),
        name="swa_sample",
    )(sinks, z, cache_k, z, cache_v, z, bias)


def _gmlp_kernel(bu_ref, bv_ref, w_ref, bs_ref, gv_ref, o_ref, *maybe_vn_ref, c, n_chunks):
    row = lax.broadcasted_iota(jnp.int32, (c, c), 0)
    col = lax.broadcasted_iota(jnp.int32, (c, c), 1)
    causal = (row // CHUNK) >= (col // CHUNK)
    v = jax.nn.gelu(bv_ref[...])
    vc = v - jnp.mean(v, axis=-1, keepdims=True)
    vn = (vc * lax.rsqrt(jnp.mean(vc * vc, axis=-1, keepdims=True) + NORM_EPS)) * gv_ref[...]
    if maybe_vn_ref:
        maybe_vn_ref[0][...] = vn
    u = jax.nn.gelu(bu_ref[...])
    vn16 = vn.astype(BF16)
    for g in range(GMLP_GROUPS):
        wg = jnp.where(causal, w_ref[g], 0.0).astype(BF16)
        bias = bs_ref[:, g:g + 1]
        cols = slice(g * LANES, (g + 1) * LANES)
        for ci in range(n_chunks):
            rows = slice(ci * c, (ci + 1) * c)
            sv = jnp.dot(wg, vn16[rows, cols], preferred_element_type=F32) + bias
            o_ref[rows, cols] = (u[rows, cols] * sv).astype(o_ref.dtype)


def _gmlp(z, w_s, b_st, g_vnorm, li, c, tl, want_vn):
    b, l, _ = z.shape
    ucol, vcol = OFF_BU // GMLP_WIDTH, OFF_BV // GMLP_WIDTH
    out_shape = [jax.ShapeDtypeStruct((b, l, GMLP_WIDTH), BF16)]
    out_specs = [pl.BlockSpec((None, tl, GMLP_WIDTH), lambda bi, i: (bi, i, 0))]
    if want_vn:
        out_shape.append(jax.ShapeDtypeStruct((b, l, GMLP_WIDTH), F32))
        out_specs.append(pl.BlockSpec((None, tl, GMLP_WIDTH), lambda bi, i: (bi, i, 0)))
    return pl.pallas_call(
        functools.partial(_gmlp_kernel, c=c, n_chunks=tl // c),
        grid=(b, l // tl),
        in_specs=[pl.BlockSpec((None, tl, GMLP_WIDTH), lambda bi, i: (bi, i, ucol)),
                  pl.BlockSpec((None, tl, GMLP_WIDTH), lambda bi, i: (bi, i, vcol)),
                  pl.BlockSpec((None, GMLP_GROUPS, c, c), lambda bi, i: (li, 0, 0, 0)),
                  pl.BlockSpec((None, c, GMLP_GROUPS), lambda bi, i: (li, 0, 0)),
                  pl.BlockSpec((None, 1, GMLP_WIDTH), lambda bi, i: (li, 0, 0))],
        out_specs=out_specs,
        out_shape=out_shape,
        compiler_params=_cparams(("parallel", "parallel")),
        name="gmlp",
    )(z, z, w_s, b_st, g_vnorm)


def _log_sigmoid(x):
    return jnp.minimum(x, 0.0) - jnp.log1p(jnp.exp(-jnp.abs(x)))


def _cumsum_rows(x):
    n = x.shape[0]
    row = lax.broadcasted_iota(jnp.int32, x.shape, 0)
    shift = 1
    while shift < n:
        x = x + jnp.where(row >= shift, pltpu.roll(x, shift, 0), 0.0)
        shift *= 2
    return x


def _gla_specs(tl, li):
    cq, ck, cv = OFF_CQ // KV_WIDTH, OFF_CK // KV_WIDTH, OFF_CV // GLA_WIDTH
    zrow = lambda width, col: pl.BlockSpec((None, tl, width), lambda bi, i: (bi, i, col))
    return [zrow(KV_WIDTH, cq), zrow(KV_WIDTH, ck), zrow(GLA_WIDTH, cv), zrow(GLA_WIDTH, 0), zrow(CG_PAD, 0),
            pl.BlockSpec((None, CG_PAD, KV_WIDTH), lambda bi, i: (li, 0, 0)),
            pl.BlockSpec((None, 1, KV_WIDTH), lambda bi, i: (li, 0, 0)),
            pl.BlockSpec((None, 1, GLA_WIDTH), lambda bi, i: (li, 0, 0))]


def _gla_sample_kernel(cq_ref, ck_ref, cv_ref, co_ref, cg_ref, wg_ref, bg_ref, gon_ref, s0_ref,
                       o_ref, sfin_ref):
    seqs = range(cq_ref.shape[0])
    c = cq_ref.shape[1]
    lane = lax.broadcasted_iota(jnp.int32, (c, LANES), 1)
    lo = lane < HALF
    row = lax.broadcasted_iota(jnp.int32, (c, c), 0)
    col = lax.broadcasted_iota(jnp.int32, (c, c), 1)
    tril = row >= col
    gate = [jnp.dot(cg_ref[b].astype(BF16), wg_ref[...], preferred_element_type=F32) for b in seqs]
    qd, qr, kd, kl, dec = [], [], [], [], []
    for b in seqs:
        la = _log_sigmoid(gate[b] + bg_ref[...]) / GLA_GATE_TEMP
        bcum = _cumsum_rows(la)
        blast = bcum[c - 1:c, :]
        q = cq_ref[b] * (GLA_DK ** -0.5)
        k = ck_ref[b]
        qd.append(q * jnp.exp(bcum))
        ref = bcum[0:1, :]
        qr.append(q * jnp.exp(bcum - ref))
        kd.append(k * jnp.exp(ref - bcum))
        kl.append(k * jnp.exp(blast - bcum))
        dec.append(jnp.exp(blast))
    for j in range(2):
        tile = slice(j * LANES, (j + 1) * LANES)
        st = [s0_ref[b, j].T for b in seqs]
        st16 = [s.astype(BF16) for s in st]
        st_new = [st[b] * dec[b][:, tile] for b in seqs]
        for hh in range(2):
            h = 2 * j + hh
            keep = lo if hh == 0 else jnp.logical_not(lo)
            hc = slice(h * GLA_DV, (h + 1) * GLA_DV)
            v_h = [cv_ref[b, :, hc].astype(BF16) for b in seqs]
            att = [_nt(qr[b][:, tile].astype(BF16), jnp.where(keep, kd[b][:, tile], 0.0).astype(BF16)) for b in seqs]
            inter = [_nt(jnp.where(keep, qd[b][:, tile], 0.0).astype(BF16), st16[b]) for b in seqs]
            upd = [_tn(v_h[b], jnp.where(keep, kl[b][:, tile], 0.0).astype(BF16)) for b in seqs]
            intra = [jnp.dot(jnp.where(tril, att[b], 0.0).astype(BF16), v_h[b], preferred_element_type=F32)
                     for b in seqs]
            for b in seqs:
                st_new[b] = st_new[b] + upd[b]
                o_n = _rms(intra[b] + inter[b], gon_ref[:, hc])
                o_ref[b, :, hc] = (o_n * jax.nn.silu(co_ref[b, :, hc])).astype(o_ref.dtype)
        for b in seqs:
            sfin_ref[b, j] = st_new[b].T


def _gla_sample(z, zco, zcg, w_gate, b_gate, g_onorm, state, li):
    b, tl, _ = z.shape
    cq, ck, cv = OFF_CQ // KV_WIDTH, OFF_CK // KV_WIDTH, OFF_CV // GLA_WIDTH
    zall = lambda width, col: pl.BlockSpec((b, tl, width), lambda i: (0, 0, col))
    return pl.pallas_call(
        _gla_sample_kernel,
        grid=(1,),
        in_specs=[zall(KV_WIDTH, cq), zall(KV_WIDTH, ck), zall(GLA_WIDTH, cv), zall(GLA_WIDTH, 0), zall(CG_PAD, 0),
                  pl.BlockSpec((None, CG_PAD, KV_WIDTH), lambda i: (li, 0, 0)),
                  pl.BlockSpec((None, 1, KV_WIDTH), lambda i: (li, 0, 0)),
                  pl.BlockSpec((None, 1, GLA_WIDTH), lambda i: (li, 0, 0)),
                  pl.BlockSpec((None, b, 2, LANES, GLA_DV), lambda i: (li, 0, 0, 0, 0))],
        out_specs=[pl.BlockSpec((b, tl, GLA_WIDTH), lambda i: (0, 0, 0)),
                   pl.BlockSpec((b, 2, LANES, GLA_DV), lambda i: (0, 0, 0, 0))],
        out_shape=[jax.ShapeDtypeStruct((b, tl, GLA_WIDTH), BF16),
                   jax.ShapeDtypeStruct((b, 2, LANES, GLA_DV), F32)],
        compiler_params=_cparams(("arbitrary",)),
        name="gla_sample",
    )(z, z, z, zco, zcg, w_gate, b_gate, g_onorm, state)


def _gla_prompt_kernel(cq_ref, ck_ref, cv_ref, co_ref, cg_ref, wg_ref, bg_ref, gon_ref,
                       o_ref, sfin_ref, st_ref):
    tl = cq_ref.shape[0]
    n_chunks = tl // CHUNK

    @pl.when(pl.program_id(1) == 0)
    def _():
        st_ref[...] = jnp.zeros_like(st_ref)

    gate = jnp.dot(cg_ref[...].astype(BF16), wg_ref[...], preferred_element_type=F32) + bg_ref[...]
    la = _log_sigmoid(gate) / GLA_GATE_TEMP
    la_hi = la.astype(BF16)
    la_lo = (la - la_hi.astype(F32)).astype(BF16)
    row = lax.broadcasted_iota(jnp.int32, (GLA_SUPER, GLA_SUPER), 0)
    col = lax.broadcasted_iota(jnp.int32, (GLA_SUPER, GLA_SUPER), 1)
    tri = ((row // CHUNK) == (col // CHUNK)) & (row >= col)
    tri16 = jnp.where(tri, 1.0, 0.0).astype(BF16)
    bcum = jnp.concatenate(
        [jnp.dot(tri16, la_hi[s * GLA_SUPER:(s + 1) * GLA_SUPER], preferred_element_type=F32)
         + jnp.dot(tri16, la_lo[s * GLA_SUPER:(s + 1) * GLA_SUPER], preferred_element_type=F32)
         for s in range(tl // GLA_SUPER)], axis=0)
    blast = [bcum[(ci + 1) * CHUNK - 1:(ci + 1) * CHUNK, :] for ci in range(n_chunks)]
    blast_b = jnp.concatenate([jnp.broadcast_to(bl, (CHUNK, KV_WIDTH)) for bl in blast], axis=0)
    k = ck_ref[...]
    q = cq_ref[...] * (GLA_DK ** -0.5)
    qd = q * jnp.exp(bcum)
    kl = k * jnp.exp(blast_b - bcum)
    ref = jnp.concatenate([jnp.broadcast_to(bcum[r0:r0 + 1], (GLA_SUB, KV_WIDTH))
                           for r0 in range(0, tl, GLA_SUB)], axis=0)
    q_same = q * jnp.exp(bcum - ref)
    k_same = k * jnp.exp(ref - bcum)
    sub_idx = (lax.broadcasted_iota(jnp.int32, (tl, KV_WIDTH), 0) % CHUNK) // GLA_SUB
    q_cross, k_cross = [], []
    for jj in range(CHUNK // GLA_SUB - 1):
        r = jnp.concatenate([jnp.broadcast_to(bcum[ci * CHUNK + (jj + 1) * GLA_SUB:ci * CHUNK + (jj + 1) * GLA_SUB + 1],
                                              (CHUNK, KV_WIDTH)) for ci in range(n_chunks)], axis=0)
        q_cross.append(q * jnp.exp(jnp.where(sub_idx > jj, bcum - r, MASK_VALUE)))
        k_cross.append(k * jnp.exp(jnp.where(sub_idx == jj, r - bcum, MASK_VALUE)))
    same_sub = (row // GLA_SUB) == (col // GLA_SUB)
    lo = lax.broadcasted_iota(jnp.int32, (tl, LANES), 1) < HALF

    keeps = [lo, jnp.logical_not(lo)]
    chunk_rows = [slice(ci * CHUNK, (ci + 1) * CHUNK) for ci in range(n_chunks)]
    super_rows = [slice(s * GLA_SUPER, (s + 1) * GLA_SUPER) for s in range(tl // GLA_SUPER)]
    stack = lambda pair, rows: jnp.concatenate([pair[0][rows], pair[1][rows]], axis=0)

    pairs = []
    for j in range(2):
        tile = slice(j * LANES, (j + 1) * LANES)
        hcs = [slice((2 * j + hh) * GLA_DV, (2 * j + hh + 1) * GLA_DV) for hh in range(2)]
        v = [cv_ref[:, hc].astype(BF16) for hc in hcs]
        qm = [jnp.where(kp, qd[:, tile], 0.0).astype(BF16) for kp in keeps]
        klm = [jnp.where(kp, kl[:, tile], 0.0).astype(BF16) for kp in keeps]
        q_same16 = q_same[:, tile].astype(BF16)
        k_same16 = [jnp.where(kp, k_same[:, tile], 0.0).astype(BF16) for kp in keeps]
        q_cross16 = jnp.concatenate([qc[:, tile] for qc in q_cross], axis=1).astype(BF16)
        k_cross16 = [jnp.concatenate([jnp.where(kp, kc[:, tile], 0.0) for kc in k_cross], axis=1).astype(BF16)
                     for kp in keeps]
        upd = [_tn(stack(v, rows), stack(klm, rows)) for rows in chunk_rows]
        att = [(_nt(q_same16[rows], stack(k_same16, rows)), _nt(q_cross16[rows], stack(k_cross16, rows)))
               for rows in super_rows]
        pairs.append(dict(tile=tile, hcs=hcs, v=v, qm=qm, upd=upd, att=att))
    for j, p in enumerate(pairs):
        st = st_ref[j]
        st16 = []
        for ci in range(n_chunks):
            st16.append(st.astype(BF16))
            st = st * jnp.exp(blast[ci][:, p["tile"]]) + p["upd"][ci]
        st_ref[j] = st
        p["inter"] = [_nt(stack(p["qm"], rows), st16[ci]) for ci, rows in enumerate(chunk_rows)]
    for p in pairs:
        intra = [[], []]
        for rows, (att_same, att_cross) in zip(super_rows, p["att"]):
            for hh in range(2):
                cols = slice(hh * GLA_SUPER, (hh + 1) * GLA_SUPER)
                a_h = jnp.where(tri, jnp.where(same_sub, att_same[:, cols], att_cross[:, cols]), 0.0).astype(BF16)
                intra[hh].append(jnp.dot(a_h, p["v"][hh][rows], preferred_element_type=F32))
        for hh in range(2):
            hc = p["hcs"][hh]
            o_h = jnp.concatenate(intra[hh], axis=0) + jnp.concatenate(
                [blk[hh * CHUNK:(hh + 1) * CHUNK] for blk in p["inter"]], axis=0)
            o_n = _rms(o_h, gon_ref[:, hc])
            o_ref[:, hc] = (o_n * jax.nn.silu(co_ref[:, hc])).astype(o_ref.dtype)

    @pl.when(pl.program_id(1) == pl.num_programs(1) - 1)
    def _():
        for j in range(2):
            sfin_ref[j] = st_ref[j].T


def _gla_prompt(z, zco, zcg, w_gate, b_gate, g_onorm, li, tl):
    b, l, _ = z.shape
    return pl.pallas_call(
        _gla_prompt_kernel,
        grid=(b, l // tl),
        in_specs=_gla_specs(tl, li),
        out_specs=[pl.BlockSpec((None, tl, GLA_WIDTH), lambda bi, i: (bi, i, 0)),
                   pl.BlockSpec((None, 2, LANES, GLA_DV), lambda bi, i: (bi, 0, 0, 0))],
        out_shape=[jax.ShapeDtypeStruct((b, l, GLA_WIDTH), BF16),
                   jax.ShapeDtypeStruct((b, 2, LANES, GLA_DV), F32)],
        scratch_shapes=[pltpu.VMEM((2, GLA_DV, LANES), F32)],
        compiler_params=_cparams(("parallel", "arbitrary")),
        name="gla_prompt",
    )(z, z, z, zco, zcg, w_gate, b_gate, g_onorm)


def _pipelined(nsub, matmul_stage, vector_stage):
    nxt = matmul_stage(0)
    for r in range(nsub):
        cur = nxt
        if r + 1 < nsub:
            nxt = matmul_stage(r + 1)
        vector_stage(r, cur)


def _outproj_kernel(x_ref, oa_ref, ob_ref, oc_ref, w_ref, g_ref, g2_ref, x1_ref, h2_ref, *, nsub):
    sub = x_ref.shape[0] // nsub

    def matmul_stage(r):
        rows = slice(r * sub, (r + 1) * sub)
        mix = jnp.dot(oa_ref[rows, :], w_ref[0:SWA_WIDTH, :], preferred_element_type=F32)
        mix = mix + jnp.dot(ob_ref[rows, :], w_ref[SWA_WIDTH:SWA_WIDTH + GMLP_WIDTH, :], preferred_element_type=F32)
        return mix + jnp.dot(oc_ref[rows, :], w_ref[SWA_WIDTH + GMLP_WIDTH:, :], preferred_element_type=F32)

    def vector_stage(r, mix):
        rows = slice(r * sub, (r + 1) * sub)
        x1 = x_ref[rows, :] + _rms(mix, g_ref[...])
        x1_ref[rows, :] = x1
        h2_ref[rows, :] = _rms(x1, g2_ref[...]).astype(BF16)

    _pipelined(nsub, matmul_stage, vector_stage)


def _outproj(x, oa, ob, oc, w, g, g2, li, tm, nsub):
    m = x.shape[0]
    row = lambda width: pl.BlockSpec((tm, width), lambda i: (i, 0))
    gain = pl.BlockSpec((None, 1, D_MODEL), lambda i: (li, 0, 0))
    return pl.pallas_call(
        functools.partial(_outproj_kernel, nsub=nsub),
        grid=(m // tm,),
        in_specs=[row(D_MODEL), row(SWA_WIDTH), row(GMLP_WIDTH), row(GLA_WIDTH),
                  pl.BlockSpec((None, D_MODEL, D_MODEL), lambda i: (li, 0, 0)), gain, gain],
        out_specs=[row(D_MODEL), row(D_MODEL)],
        out_shape=[jax.ShapeDtypeStruct((m, D_MODEL), F32), jax.ShapeDtypeStruct((m, D_MODEL), BF16)],
        compiler_params=_cparams(("parallel",)),
        name="outproj",
    )(x, oa, ob, oc, w, g, g2)


def _ffn_zero_first(f_ref):
    @pl.when(pl.program_id(1) == 0)
    def _():
        f_ref[...] = jnp.zeros_like(f_ref)


def _ffn_kernel(h_ref, wup_ref, wdn_ref, f_ref):
    _ffn_zero_first(f_ref)
    u = jnp.maximum(jnp.dot(h_ref[...], wup_ref[...], preferred_element_type=F32), 0.0)
    f_ref[...] += jnp.dot((u * u).astype(BF16), wdn_ref[...], preferred_element_type=F32)


def _ffn_rest_kernel(h_ref, wup_ref, wdn_ref, f_first_ref, f_ref):
    _ffn_kernel(h_ref, wup_ref, wdn_ref, f_ref)


def _ffn_first_kernel(h_ref, wup_ref, wdn_ref, f_ref, fs_ref, wup16_ref, wdn16_ref):
    @pl.when(pl.program_id(1) == 0)
    def _():
        f_ref[...] = jnp.zeros_like(f_ref)
        fs_ref[...] = jnp.zeros_like(fs_ref)

    wup16_ref[...] = wup_ref[...].astype(BF16)
    wdn16_ref[...] = wdn_ref[...].astype(BF16)
    u = jnp.maximum(jnp.dot(h_ref[...], wup16_ref[...], preferred_element_type=F32), 0.0)
    d = jnp.dot((u * u).astype(BF16), wdn16_ref[...], preferred_element_type=F32)
    tm = f_ref.shape[0]
    f_ref[...] += d[:tm]
    fs_ref[...] += d[tm:]


def _ffn_specs(tm, tk, row_of):
    return [pl.BlockSpec((tm, D_MODEL), lambda i, k: (row_of(i), 0)),
            pl.BlockSpec((None, D_MODEL, tk), lambda i, k: (0, 0, k)),
            pl.BlockSpec((None, tk, D_MODEL), lambda i, k: (0, k, 0))]


def _ffn(h, hs, w_up, w_down, li, tm, tk_first, tk):
    m, ms = h.shape[0], hs.shape[0]
    f_shape = jax.ShapeDtypeStruct((m, D_MODEL), F32)
    h_first = jnp.concatenate([h[:tm], hs], axis=0)
    f_spec, wup_spec, wdn_spec = _ffn_specs(tm, tk_first, lambda i: 0)
    f, fs, wup16, wdn16 = pl.pallas_call(
        _ffn_first_kernel,
        grid=(1, D_FF // tk_first),
        in_specs=[pl.BlockSpec((tm + ms, D_MODEL), lambda i, k: (0, 0)),
                  pl.BlockSpec((None, D_MODEL, tk_first), lambda i, k: (li, 0, k)),
                  pl.BlockSpec((None, tk_first, D_MODEL), lambda i, k: (li, k, 0))],
        out_specs=[f_spec, pl.BlockSpec((ms, D_MODEL), lambda i, k: (0, 0)), wup_spec, wdn_spec],
        out_shape=[f_shape, jax.ShapeDtypeStruct((ms, D_MODEL), F32),
                   jax.ShapeDtypeStruct((1, D_MODEL, D_FF), BF16), jax.ShapeDtypeStruct((1, D_FF, D_MODEL), BF16)],
        compiler_params=_cparams(("arbitrary", "arbitrary"), FFN_VMEM_LIMIT),
        name="ffn_first",
    )(h_first, w_up, w_down)
    f = pl.pallas_call(
        _ffn_rest_kernel,
        grid=(m // tm - 1, D_FF // tk),
        in_specs=_ffn_specs(tm, tk, lambda i: i + 1) + [pl.BlockSpec(memory_space=pl.ANY)],
        out_specs=pl.BlockSpec((tm, D_MODEL), lambda i, k: (i + 1, 0)),
        out_shape=f_shape,
        input_output_aliases={3: 0},
        compiler_params=_cparams(("parallel", "arbitrary"), FFN_VMEM_LIMIT),
        name="ffn_rest",
    )(h, wup16, wdn16, f)
    return f, fs


def _ple_kernel(x1_ref, f_ref, pe_ref, gpost_ref, wp_ref, wg_ref, gnext_ref, x3_ref, *maybe_hn_ref, nsub):
    sub = x1_ref.shape[0] // nsub

    def matmul_stage(r):
        rows = slice(r * sub, (r + 1) * sub)
        proj = jnp.dot(pe_ref[rows, :].astype(BF16), wp_ref[...], preferred_element_type=F32)
        x2 = x1_ref[rows, :] + _rms(f_ref[rows, :], gpost_ref[...])
        gate = jnp.dot(x2.astype(BF16), wg_ref[...], preferred_element_type=F32)
        return x2, gate, proj

    def vector_stage(r, staged):
        rows = slice(r * sub, (r + 1) * sub)
        x2, gate, proj = staged
        x3 = x2 + proj * jax.nn.sigmoid(gate)
        x3_ref[rows, :] = x3
        if maybe_hn_ref:
            maybe_hn_ref[0][rows, :] = _rms(x3, gnext_ref[...]).astype(BF16)

    _pipelined(nsub, matmul_stage, vector_stage)


def _ple(x1, f, pe, gpost, wp, wg, gnext, li, tm, nsub, want_h):
    m = x1.shape[0]
    row = lambda: pl.BlockSpec((tm, D_MODEL), lambda i: (i, 0))
    ln = min(li + 1, DEPTH - 1)
    out_specs = [row()]
    out_shape = [jax.ShapeDtypeStruct((m, D_MODEL), F32)]
    if want_h:
        out_specs.append(row())
        out_shape.append(jax.ShapeDtypeStruct((m, D_MODEL), BF16))
    return pl.pallas_call(
        functools.partial(_ple_kernel, nsub=nsub),
        grid=(m // tm,),
        in_specs=[row(), row(),
                  pl.BlockSpec((None, tm, PLE_DIM), lambda i: (li, i, 0)),
                  pl.BlockSpec((None, 1, D_MODEL), lambda i: (li, 0, 0)),
                  pl.BlockSpec((None, PLE_DIM, D_MODEL), lambda i: (li, 0, 0)),
                  pl.BlockSpec((None, D_MODEL, D_MODEL), lambda i: (li, 0, 0)),
                  pl.BlockSpec((None, 1, D_MODEL), lambda i: (ln, 0, 0))],
        out_specs=out_specs,
        out_shape=out_shape,
        compiler_params=_cparams(("parallel",)),
        name="ple",
    )(x1, f, pe, gpost, wp, wg, gnext)


def _t5_bucket(rel):
    half = REL_BUCKETS // 2
    max_exact = half // 2
    n = -rel
    ret = jnp.where(n < 0, half, 0)
    n = jnp.abs(n)
    nf = jnp.maximum(n, 1).astype(jnp.float32)
    large = max_exact + (jnp.log(nf / max_exact) / math.log(REL_MAX_DIST / max_exact)
                         * (half - max_exact)).astype(jnp.int32)
    large = jnp.minimum(large, half - 1)
    return ret + jnp.where(n < max_exact, n, large)


def _bias_kernel(tab_ref, bkt_p_ref, valid_p_ref, bkt_s_ref, out_p_ref, out_s_ref):
    bkt_p = bkt_p_ref[...]
    bkt_s = bkt_s_ref[...]
    for h in range(SWA_Q_HEADS):
        def pick(b, accs):
            val = tab_ref[b, h]
            return (jnp.where(bkt_p == b, val, accs[0]), jnp.where(bkt_s == b, val, accs[1]))

        acc_p, acc_s = lax.fori_loop(0, REL_BUCKETS, pick,
                                     (jnp.zeros(bkt_p.shape, F32), jnp.zeros(bkt_s.shape, F32)))
        kv, hq = divmod(h, SWA_GROUP)
        for var in range(2):
            out_p_ref[var, kv, :, hq * LANES:(hq + 1) * LANES] = jnp.where(valid_p_ref[var] != 0, acc_p * LOG2E, MASK_VALUE)
        out_s_ref[h] = acc_s


def _bias_tables(rel_bias, dec_seq):
    qpos = jnp.arange(WINDOW)[None, :]
    kpos = jnp.arange(-WINDOW, WINDOW)[:, None]
    bkt_p = _t5_bucket(kpos - qpos).astype(jnp.int32)
    kchunk = jnp.floor_divide(kpos, CHUNK)
    qchunk = qpos // CHUNK
    valid = (kchunk >= qchunk - WINDOW // CHUNK) & (kchunk <= qchunk)
    valid_p = jnp.stack([valid & (kpos >= 0), valid]).astype(jnp.int32)
    bkt_s = _t5_bucket(jnp.arange(-WINDOW, dec_seq)[None, :] - jnp.arange(dec_seq)[:, None]).astype(jnp.int32)
    full = lambda a: pl.BlockSpec(a.shape, lambda: (0,) * a.ndim)
    out_shape = [jax.ShapeDtypeStruct((2, SWA_KV_HEADS, 2 * WINDOW, SWA_GROUP * LANES), F32),
                 jax.ShapeDtypeStruct((SWA_Q_HEADS, dec_seq, WINDOW + dec_seq), F32)]
    return pl.pallas_call(
        _bias_kernel,
        in_specs=[pl.BlockSpec(memory_space=pltpu.SMEM), full(bkt_p), full(valid_p), full(bkt_s)],
        out_specs=[full(s) for s in out_shape],
        out_shape=out_shape,
        name="rel_bias",
    )(rel_bias, bkt_p, valid_p, bkt_s)


def _prep_w_in(w):
    wt = jnp.swapaxes(w, 1, 2).astype(BF16)
    co = Z_WIDTH + GLA_GATE_RANK
    pad = jnp.zeros((w.shape[0], CG_PAD - GLA_GATE_RANK, w.shape[1]), BF16)
    return wt, wt[:, co:], jnp.concatenate([wt[:, Z_WIDTH:co], pad], axis=1)


def kernel(x_prompt, x_sample, cache_swa_k, cache_swa_v, state_gla, p_prompt, p_sample, w_in, w_gate, b_gate, rel_bias, attn_sinks, w_spatial, b_spatial, g_gmlp_vnorm, g_gla_onorm, w_out, g_mix_pre, g_mix_post, g_ffn_pre, g_ffn_post, w_up, w_down, w_ple, w_ple_gate):
    bp, lp, _ = x_prompt.shape
    bs, ls, _ = x_sample.shape
    mp, ms = bp * lp, bs * ls
    row3 = lambda a: a.reshape(DEPTH, 1, -1)

    bias_p, bias_s = _bias_tables(rel_bias, ls)
    sink_t = jnp.repeat(attn_sinks * LOG2E, LANES, axis=-1).reshape(DEPTH, SWA_KV_HEADS, 1, SWA_GROUP * LANES)
    sinks_flat = attn_sinks.reshape(-1)

    w_in16, w_co16, w_cg16 = _prep_w_in(w_in)
    w_out16, w_ple16, w_plg16 = w_out.astype(BF16), w_ple.astype(BF16), w_ple_gate.astype(BF16)
    wg16 = jnp.concatenate([w_gate, jnp.zeros((DEPTH, CG_PAD - GLA_GATE_RANK, KV_WIDTH), F32)], axis=1).astype(BF16)
    g_pre, g_post, g_fpre, g_fpost = row3(g_mix_pre), row3(g_mix_post), row3(g_ffn_pre), row3(g_ffn_post)
    bg, gon, gvn = row3(b_gate), row3(g_gla_onorm), row3(g_gmlp_vnorm)
    b_st = jnp.swapaxes(b_spatial, 1, 2)
    w_s_small, b_st_small = w_spatial[:, :, :ls, :ls], b_st[:, :ls, :]
    pe_p = p_prompt.reshape(DEPTH, mp, PLE_DIM)
    pe_s = p_sample.reshape(DEPTH, ms, PLE_DIM)
    cache_k = cache_swa_k.reshape(DEPTH, bs, WINDOW, KV_WIDTH)
    cache_v = cache_swa_v.reshape(DEPTH, bs, WINDOW, KV_WIDTH)
    state = state_gla.reshape(DEPTH, bs, 2, LANES, GLA_DV)

    def mix_out(x, oa, ob, oc, li, ts, nsub):
        m = x.shape[0]
        return _outproj(x, oa.reshape(m, -1), ob.reshape(m, -1), oc.reshape(m, -1), w_out16, g_post, g_fpre,
                        li, ts, nsub)

    def layer_out(x1, f, pe, li, ts, nsub):
        out = _ple(x1, f, pe, g_fpost, w_ple16, w_plg16, g_pre, li, ts, nsub, li + 1 < DEPTH)
        return out if li + 1 < DEPTH else (out[0], None)

    xp = x_prompt.reshape(mp, D_MODEL)
    xs = x_sample.reshape(ms, D_MODEL)
    hp = _prenorm(xp, g_pre, 0, FFN_ROWS)
    hs = _prenorm(xs, g_pre, 0, ms)
    kp_rows, vp_rows, ks_rows, vs_rows, sp_states, ss_states, gv_rows = [], [], [], [], [], [], []
    for li in range(DEPTH):
        zp, zco, zcg = _inproj(hp, w_in16, w_co16, w_cg16, li, FFN_ROWS, INPROJ_COLS)
        zp, zco, zcg = zp.reshape(bp, lp, -1), zco.reshape(bp, lp, -1), zcg.reshape(bp, lp, -1)
        oa = _attn_prompt(zp, bias_p, sink_t[li], ATTN_ROWS)
        ob, = _gmlp(zp, w_spatial, b_st, gvn, li, GMLP_CHUNK, GMLP_ROWS, False)
        oc, s_p = _gla_prompt(zp, zco, zcg, wg16, bg, gon, li, GLA_ROWS)
        x1p, h2p = mix_out(xp, oa, ob, oc, li, DENSE_ROWS, OUTPROJ_SUB)
        kp_rows.append(zp[:, lp - WINDOW:, OFF_K:OFF_K + KV_WIDTH].reshape(bp, WINDOW, SWA_KV_HEADS, HEAD_DIM))
        vp_rows.append(zp[:, lp - WINDOW:, OFF_V:OFF_V + KV_WIDTH].reshape(bp, WINDOW, SWA_KV_HEADS, HEAD_DIM))
        sp_states.append(s_p.reshape(bp, GLA_HEADS, GLA_DK, GLA_DV))

        zs, zco, zcg = _inproj(hs, w_in16, w_co16, w_cg16, li, ms, INPROJ_COLS)
        zs, zco, zcg = zs.reshape(bs, ls, -1), zco.reshape(bs, ls, -1), zcg.reshape(bs, ls, -1)
        oa = _attn_sample(zs, cache_k, cache_v, bias_s, sinks_flat[li * SWA_Q_HEADS:(li + 1) * SWA_Q_HEADS], li)
        ob, vn_s = _gmlp(zs.reshape(1, ms, -1), w_s_small, b_st_small, gvn, li, ls, ms, True)
        oc, s_s = _gla_sample(zs, zco, zcg, wg16, bg, gon, state, li)
        x1s, h2s = mix_out(xs, oa, ob, oc, li, ms, 1)

        fp, fs = _ffn(h2p, h2s, w_up, w_down, li, FFN_ROWS, FFN_FIRST_CHUNK, FFN_CHUNK)
        xp, hp = layer_out(x1p, fp, pe_p, li, DENSE_ROWS, PLE_SUB)
        xs, hs = layer_out(x1s, fs, pe_s, li, ms, 1)
        ks_rows.append(zs[:, :, OFF_K:OFF_K + KV_WIDTH].reshape(bs, ls, SWA_KV_HEADS, HEAD_DIM))
        vs_rows.append(zs[:, :, OFF_V:OFF_V + KV_WIDTH].reshape(bs, ls, SWA_KV_HEADS, HEAD_DIM))
        ss_states.append(s_s.reshape(bs, GLA_HEADS, GLA_DK, GLA_DV).astype(state_gla.dtype))
        gv_rows.append(vn_s.reshape(bs, ls, GMLP_WIDTH))

    return (xp.reshape(bp, lp, D_MODEL), xs.reshape(bs, ls, D_MODEL),
            jnp.stack(kp_rows), jnp.stack(vp_rows), jnp.stack(ks_rows), jnp.stack(vs_rows),
            jnp.stack(sp_states), jnp.stack(ss_states), jnp.stack(gv_rows))
```

```python
import functools
import math

import jax
import jax.numpy as jnp
from jax import lax
from jax.experimental import pallas as pl
from jax.experimental.pallas import tpu as pltpu

F32 = jnp.float32
BF16 = jnp.bfloat16

D_MODEL = 2048
DEPTH = 4
CHUNK = 64
HEAD_DIM = 64
SWA_Q_HEADS = 16
SWA_KV_HEADS = 4
SWA_GROUP = SWA_Q_HEADS // SWA_KV_HEADS
SWA_WIDTH = 1024
KV_WIDTH = 256
WINDOW = 128
REL_BUCKETS = 32
REL_MAX_DIST = 128
GMLP_GROUPS = 4
GMLP_WIDTH = 512
GMLP_CHUNK = 128
GLA_HEADS = 4
GLA_DK = 64
GLA_DV = 128
GLA_WIDTH = 512
GLA_GATE_RANK = 16
GLA_GATE_TEMP = 16.0
D_FF = 4 * D_MODEL
PLE_DIM = 256
NORM_EPS = 1e-6
LANES = 128
HALF = LANES // 2
BF16_ROWS = 16

OFF_Q, OFF_K, OFF_V = 0, 1024, 1280
OFF_BU, OFF_BV = 1536, 2048
OFF_CQ, OFF_CK, OFF_CV = 2560, 2816, 3072
Z_WIDTH = 3584
CG_PAD = LANES

LOG2E = math.log2(math.e)
MASK_VALUE = -1e30
VMEM_LIMIT = 52 * 1024 * 1024
FFN_VMEM_LIMIT = 60 * 1024 * 1024

FFN_ROWS = 1024
FFN_CHUNK = 1024
FFN_FIRST_CHUNK = 512
INPROJ_COLS = 1792
DENSE_ROWS = 512
OUTPROJ_SUB = 4
PLE_SUB = 2
ATTN_ROWS = 1024
GMLP_ROWS = 1024
ATTN_LOOKAHEAD = 2
GLA_ROWS = 1024
GLA_SUPER = 2 * CHUNK
GLA_SUB = 16


def _cparams(sem, vmem=VMEM_LIMIT):
    return pltpu.CompilerParams(dimension_semantics=sem, vmem_limit_bytes=vmem)


def _rms(x, g):
    return (x * lax.rsqrt(jnp.mean(x * x, axis=-1, keepdims=True) + NORM_EPS)) * g


def _nt(a, b):
    return lax.dot_general(a, b, (((1,), (1,)), ((), ())), preferred_element_type=F32)


def _tn(a, b):
    return lax.dot_general(a, b, (((0,), (0,)), ((), ())), preferred_element_type=F32)


def _prenorm_kernel(x_ref, g_ref, h_ref):
    h_ref[...] = _rms(x_ref[...], g_ref[...]).astype(BF16)


def _prenorm(x, g, li, tm):
    m = x.shape[0]
    return pl.pallas_call(
        _prenorm_kernel,
        grid=(m // tm,),
        in_specs=[pl.BlockSpec((tm, D_MODEL), lambda i: (i, 0)),
                  pl.BlockSpec((None, 1, D_MODEL), lambda i: (li, 0, 0))],
        out_specs=pl.BlockSpec((tm, D_MODEL), lambda i: (i, 0)),
        out_shape=jax.ShapeDtypeStruct((m, D_MODEL), BF16),
        compiler_params=_cparams(("parallel",)),
        name="prenorm",
    )(x, g)


def _inproj_kernel(h_ref, w_ref, wco_ref, wcg_ref, z_ref, zco_ref, zcg_ref):
    z_ref[...] = _nt(h_ref[...], w_ref[...])

    @pl.when(pl.program_id(1) == 0)
    def _():
        zco_ref[...] = _nt(h_ref[...], wco_ref[...])
        zcg_ref[...] = _nt(h_ref[...], wcg_ref[...])


def _inproj(h, w, wco, wcg, li, tm, tn):
    m = h.shape[0]
    extra = lambda width: pl.BlockSpec((None, width, D_MODEL), lambda i, j: (li, 0, 0))
    return pl.pallas_call(
        _inproj_kernel,
        grid=(m // tm, Z_WIDTH // tn),
        in_specs=[pl.BlockSpec((tm, D_MODEL), lambda i, j: (i, 0)),
                  pl.BlockSpec((None, tn, D_MODEL), lambda i, j: (li, j, 0)),
                  extra(GLA_WIDTH), extra(CG_PAD)],
        out_specs=[pl.BlockSpec((tm, tn), lambda i, j: (i, j)),
                   pl.BlockSpec((tm, GLA_WIDTH), lambda i, j: (i, 0)),
                   pl.BlockSpec((tm, CG_PAD), lambda i, j: (i, 0))],
        out_shape=[jax.ShapeDtypeStruct((m, Z_WIDTH), F32), jax.ShapeDtypeStruct((m, GLA_WIDTH), F32),
                   jax.ShapeDtypeStruct((m, CG_PAD), F32)],
        compiler_params=_cparams(("parallel", "arbitrary")),
        name="inproj",
    )(h, w, wco, wcg)


def _attn_prompt_kernel(q_ref, kp_ref, kc_ref, vp_ref, vc_ref, bias_ref, sink_ref, o_ref, *, nsub):
    first = pl.program_id(1) == 0
    kfull = jnp.concatenate([kp_ref[...], kc_ref[...]], axis=0)
    vfull = jnp.concatenate([vp_ref[...], vc_ref[...]], axis=0)
    rows = kfull.shape[0]
    lo_k = lax.broadcasted_iota(jnp.int32, (rows, LANES), 1) < HALF
    lo_q = lax.broadcasted_iota(jnp.int32, (WINDOW, LANES), 1) < HALF
    kdup, v_t = [], []
    for t in range(KV_WIDTH // LANES):
        kt = kfull[:, t * LANES:(t + 1) * LANES]
        kt_r = pltpu.roll(kt, HALF, 1)
        kdup.append(jnp.where(lo_k, kt, kt_r).astype(BF16))
        kdup.append(jnp.where(lo_k, kt_r, kt).astype(BF16))
        vt_t = vfull[:, t * LANES:(t + 1) * LANES].T
        v_t.append(vt_t[:HALF].astype(BF16))
        v_t.append(vt_t[HALF:].astype(BF16))
    ones = jnp.ones((BF16_ROWS, 2 * WINDOW), BF16)

    def scores(j, kv):
        qa = q_ref[j * WINDOW:(j + 1) * WINDOW, (2 * kv) * LANES:(2 * kv + 1) * LANES] * (HEAD_DIM ** -0.5 * LOG2E)
        qb = q_ref[j * WINDOW:(j + 1) * WINDOW, (2 * kv + 1) * LANES:(2 * kv + 2) * LANES] * (HEAD_DIM ** -0.5 * LOG2E)
        qm = jnp.concatenate([jnp.where(lo_q, qa, 0.0), jnp.where(lo_q, 0.0, qa),
                              jnp.where(lo_q, qb, 0.0), jnp.where(lo_q, 0.0, qb)], axis=0).astype(BF16)
        return _nt(kdup[kv][j * WINDOW:(j + 2) * WINDOW], qm)

    steps = [(j, kv) for j in range(nsub) for kv in range(SWA_KV_HEADS)]
    ahead = [scores(*steps[n]) for n in range(ATTN_LOOKAHEAD)]
    for n, (j, kv) in enumerate(steps):
        s_t = ahead.pop(0)
        if n + ATTN_LOOKAHEAD < len(steps):
            ahead.append(scores(*steps[n + ATTN_LOOKAHEAD]))
        keys = slice(j * WINDOW, (j + 2) * WINDOW)
        var = jnp.where(first, 0, 1) if j == 0 else 1
        sink = sink_ref[kv]
        e_cols, m_cols = [], []
        for hq in range(SWA_GROUP):
            cols = slice(hq * LANES, (hq + 1) * LANES)
            s_h = s_t[:, cols] + bias_ref[var, kv, :, cols]
            m_h = jnp.maximum(jnp.max(s_h, axis=0, keepdims=True), sink[:, cols])
            e_cols.append(jnp.exp2(s_h - m_h).astype(BF16))
            m_cols.append(m_h)
        e = jnp.concatenate(e_cols, axis=1)
        m = jnp.concatenate(m_cols, axis=1)
        lhs = jnp.concatenate([v_t[kv][:, keys], ones], axis=0)
        r = jnp.dot(lhs, e, preferred_element_type=F32)
        denom = r[HALF:HALF + 1] + jnp.exp2(sink - m)
        o_t = r[:HALF] * (1.0 / denom)
        for qt in range(2):
            pair = jnp.concatenate([o_t[:, (2 * qt) * LANES:(2 * qt + 1) * LANES],
                                    o_t[:, (2 * qt + 1) * LANES:(2 * qt + 2) * LANES]], axis=0)
            col = (2 * kv + qt) * LANES
            o_ref[j * WINDOW:(j + 1) * WINDOW, col:col + LANES] = pair.T.astype(o_ref.dtype)


def _attn_prompt(z, bias_t, sink_t, tl):
    b, l, _ = z.shape
    kcol, vcol = OFF_K // KV_WIDTH, OFF_V // KV_WIDTH
    per = tl // WINDOW
    prev = lambda col: pl.BlockSpec((None, WINDOW, KV_WIDTH),
                                    lambda bi, i: (bi, jnp.maximum(i * per - 1, 0), col))
    cur = lambda col: pl.BlockSpec((None, tl, KV_WIDTH), lambda bi, i: (bi, i, col))
    return pl.pallas_call(
        functools.partial(_attn_prompt_kernel, nsub=per),
        grid=(b, l // tl),
        in_specs=[pl.BlockSpec((None, tl, SWA_WIDTH), lambda bi, i: (bi, i, 0)),
                  prev(kcol), cur(kcol), prev(vcol), cur(vcol),
                  pl.BlockSpec(bias_t.shape, lambda bi, i: (0, 0, 0, 0)),
                  pl.BlockSpec(sink_t.shape, lambda bi, i: (0, 0, 0))],
        out_specs=pl.BlockSpec((None, tl, SWA_WIDTH), lambda bi, i: (bi, i, 0)),
        out_shape=jax.ShapeDtypeStruct((b, l, SWA_WIDTH), BF16),
        compiler_params=_cparams(("parallel", "arbitrary")),
        name="swa_prompt",
    )(z, z, z, z, z, bias_t, sink_t)


def _attn_sample_kernel(sink_ref, q_ref, kp_ref, kc_ref, vp_ref, vc_ref, bias_ref, o_ref):
    seqs = range(q_ref.shape[0])
    q = [(q_ref[b] * (HEAD_DIM ** -0.5)).astype(BF16) for b in seqs]
    k = [jnp.concatenate([kp_ref[b], kc_ref[b]], axis=0) for b in seqs]
    v = [jnp.concatenate([vp_ref[b], vc_ref[b]], axis=0) for b in seqs]
    lk = k[0].shape[0]
    lo = lax.broadcasted_iota(jnp.int32, (lk, LANES), 1) < HALF
    qlane = lax.broadcasted_iota(jnp.int32, (q[0].shape[0], LANES), 1)
    for t in range(KV_WIDTH // LANES):
        kt = [kb[:, t * LANES:(t + 1) * LANES] for kb in k]
        vt = [vb[:, t * LANES:(t + 1) * LANES] for vb in v]
        kt_r = [pltpu.roll(x, HALF, 1) for x in kt]
        vt_r = [pltpu.roll(x, HALF, 1) for x in vt]
        for half in range(2):
            kv = 2 * t + half
            k_in_lo, k_in_hi = (kt, kt_r) if half == 0 else (kt_r, kt)
            v_in_lo, v_in_hi = (vt, vt_r) if half == 0 else (vt_r, vt)
            k_half = [[jnp.where(lo, x, 0.0).astype(BF16) for x in k_in_lo],
                      [jnp.where(lo, 0.0, x).astype(BF16) for x in k_in_hi]]
            v_half = [[jnp.where(lo, x, 0.0).astype(BF16) for x in v_in_lo],
                      [jnp.where(lo, 0.0, x).astype(BF16) for x in v_in_hi]]
            for qt in (2 * kv, 2 * kv + 1):
                acc = [None for _ in seqs]
                inv = [[] for _ in seqs]
                for hh in range(2):
                    h = 2 * qt + hh
                    sink = sink_ref[h]
                    s = [_nt(q[b][:, qt * LANES:(qt + 1) * LANES], k_half[hh][b]) for b in seqs]
                    e = []
                    for b in seqs:
                        s_b = s[b] + bias_ref[h]
                        m = jnp.maximum(jnp.max(s_b, axis=-1, keepdims=True), sink)
                        e_b = jnp.exp(s_b - m)
                        inv[b].append(1.0 / (jnp.sum(e_b, axis=-1, keepdims=True) + jnp.exp(sink - m)))
                        e.append(e_b.astype(BF16))
                    for b in seqs:
                        pv = jnp.dot(e[b], v_half[hh][b], preferred_element_type=F32)
                        acc[b] = pv if acc[b] is None else acc[b] + pv
                for b in seqs:
                    o2 = acc[b] * jnp.where(qlane < HALF, inv[b][0], inv[b][1])
                    o_ref[b, :, qt * LANES:(qt + 1) * LANES] = o2.astype(o_ref.dtype)


def _attn_sample(z, cache_k, cache_v, bias, sinks, li):
    b, tq, _ = z.shape
    kcol, vcol = OFF_K // KV_WIDTH, OFF_V // KV_WIDTH
    cache = pl.BlockSpec((None, b, WINDOW, KV_WIDTH), lambda i: (li, 0, 0, 0))
    return pl.pallas_call(
        _attn_sample_kernel,
        grid=(1,),
        in_specs=[pl.BlockSpec(memory_space=pltpu.SMEM),
                  pl.BlockSpec((b, tq, SWA_WIDTH), lambda i: (0, 0, 0)),
                  cache,
                  pl.BlockSpec((b, tq, KV_WIDTH), lambda i: (0, 0, kcol)),
                  cache,
                  pl.BlockSpec((b, tq, KV_WIDTH), lambda i: (0, 0, vcol)),
                  pl.BlockSpec(bias.shape, lambda i: (0, 0, 0))],
        out_specs=pl.BlockSpec((b, tq, SWA_WIDTH), lambda i: (0, 0, 0)),
        out_shape=jax.ShapeDtypeStruct((b, tq, SWA_WIDTH), BF16),
        compiler_params=_cparams(("arbitrary",)),
        name="swa_sample",
    )(sinks, z, cache_k, z, cache_v, z, bias)


def _gelu_tanh(x):
    c = math.sqrt(2.0 / math.pi)
    hx = 0.5 * x
    return hx + hx * jnp.tanh(x * (c + (c * 0.044715) * (x * x)))


def _gmlp_kernel(bu_ref, bv_ref, w_ref, bs_ref, gv_ref, o_ref, *maybe_vn_ref, c, n_chunks):
    row = lax.broadcasted_iota(jnp.int32, (c, c), 0)
    col = lax.broadcasted_iota(jnp.int32, (c, c), 1)
    causal = (row // CHUNK) >= (col // CHUNK)
    v = _gelu_tanh(bv_ref[...])
    vc = v - jnp.mean(v, axis=-1, keepdims=True)
    vn = (vc * lax.rsqrt(jnp.mean(vc * vc, axis=-1, keepdims=True) + NORM_EPS)) * gv_ref[...]
    if maybe_vn_ref:
        maybe_vn_ref[0][...] = vn
    u = _gelu_tanh(bu_ref[...])
    vn16 = vn.astype(BF16)
    for g in range(GMLP_GROUPS):
        wg = jnp.where(causal, w_ref[g], 0.0).astype(BF16)
        bias = bs_ref[:, g:g + 1]
        cols = slice(g * LANES, (g + 1) * LANES)
        for ci in range(n_chunks):
            rows = slice(ci * c, (ci + 1) * c)
            sv = jnp.dot(wg, vn16[rows, cols], preferred_element_type=F32) + bias
            o_ref[rows, cols] = (u[rows, cols] * sv).astype(o_ref.dtype)


def _gmlp(z, w_s, b_st, g_vnorm, li, c, tl, want_vn):
    b, l, _ = z.shape
    ucol, vcol = OFF_BU // GMLP_WIDTH, OFF_BV // GMLP_WIDTH
    out_shape = [jax.ShapeDtypeStruct((b, l, GMLP_WIDTH), BF16)]
    out_specs = [pl.BlockSpec((None, tl, GMLP_WIDTH), lambda bi, i: (bi, i, 0))]
    if want_vn:
        out_shape.append(jax.ShapeDtypeStruct((b, l, GMLP_WIDTH), F32))
        out_specs.append(pl.BlockSpec((None, tl, GMLP_WIDTH), lambda bi, i: (bi, i, 0)))
    return pl.pallas_call(
        functools.partial(_gmlp_kernel, c=c, n_chunks=tl // c),
        grid=(b, l // tl),
        in_specs=[pl.BlockSpec((None, tl, GMLP_WIDTH), lambda bi, i: (bi, i, ucol)),
                  pl.BlockSpec((None, tl, GMLP_WIDTH), lambda bi, i: (bi, i, vcol)),
                  pl.BlockSpec((None, GMLP_GROUPS, c, c), lambda bi, i: (li, 0, 0, 0)),
                  pl.BlockSpec((None, c, GMLP_GROUPS), lambda bi, i: (li, 0, 0)),
                  pl.BlockSpec((None, 1, GMLP_WIDTH), lambda bi, i: (li, 0, 0))],
        out_specs=out_specs,
        out_shape=out_shape,
        compiler_params=_cparams(("parallel", "parallel")),
        name="gmlp",
    )(z, z, w_s, b_st, g_vnorm)


def _log_sigmoid(x):
    return jnp.minimum(x, 0.0) - jnp.log1p(jnp.exp(-jnp.abs(x)))


def _cumsum_rows(x):
    n = x.shape[0]
    row = lax.broadcasted_iota(jnp.int32, x.shape, 0)
    shift = 1
    while shift < n:
        x = x + jnp.where(row >= shift, pltpu.roll(x, shift, 0), 0.0)
        shift *= 2
    return x


def _gla_specs(tl, li):
    cq, ck, cv = OFF_CQ // KV_WIDTH, OFF_CK // KV_WIDTH, OFF_CV // GLA_WIDTH
    zrow = lambda width, col: pl.BlockSpec((None, tl, width), lambda bi, i: (bi, i, col))
    return [zrow(KV_WIDTH, cq), zrow(KV_WIDTH, ck), zrow(GLA_WIDTH, cv), zrow(GLA_WIDTH, 0), zrow(CG_PAD, 0),
            pl.BlockSpec((None, CG_PAD, KV_WIDTH), lambda bi, i: (li, 0, 0)),
            pl.BlockSpec((None, 1, KV_WIDTH), lambda bi, i: (li, 0, 0)),
            pl.BlockSpec((None, 1, GLA_WIDTH), lambda bi, i: (li, 0, 0))]


def _gla_sample_kernel(cq_ref, ck_ref, cv_ref, co_ref, cg_ref, wg_ref, bg_ref, gon_ref, s0_ref,
                       o_ref, sfin_ref):
    seqs = range(cq_ref.shape[0])
    c = cq_ref.shape[1]
    lane = lax.broadcasted_iota(jnp.int32, (c, LANES), 1)
    lo = lane < HALF
    row = lax.broadcasted_iota(jnp.int32, (c, c), 0)
    col = lax.broadcasted_iota(jnp.int32, (c, c), 1)
    tril = row >= col
    gate = [jnp.dot(cg_ref[b].astype(BF16), wg_ref[...], preferred_element_type=F32) for b in seqs]
    qd, qr, kd, kl, dec = [], [], [], [], []
    for b in seqs:
        la = _log_sigmoid(gate[b] + bg_ref[...]) / GLA_GATE_TEMP
        bcum = _cumsum_rows(la)
        blast = bcum[c - 1:c, :]
        q = cq_ref[b] * (GLA_DK ** -0.5)
        k = ck_ref[b]
        qd.append(q * jnp.exp(bcum))
        ref = bcum[0:1, :]
        qr.append(q * jnp.exp(bcum - ref))
        kd.append(k * jnp.exp(ref - bcum))
        kl.append(k * jnp.exp(blast - bcum))
        dec.append(jnp.exp(blast))
    for j in range(2):
        tile = slice(j * LANES, (j + 1) * LANES)
        st = [s0_ref[b, j].T for b in seqs]
        st16 = [s.astype(BF16) for s in st]
        st_new = [st[b] * dec[b][:, tile] for b in seqs]
        for hh in range(2):
            h = 2 * j + hh
            keep = lo if hh == 0 else jnp.logical_not(lo)
            hc = slice(h * GLA_DV, (h + 1) * GLA_DV)
            v_h = [cv_ref[b, :, hc].astype(BF16) for b in seqs]
            att = [_nt(qr[b][:, tile].astype(BF16), jnp.where(keep, kd[b][:, tile], 0.0).astype(BF16)) for b in seqs]
            inter = [_nt(jnp.where(keep, qd[b][:, tile], 0.0).astype(BF16), st16[b]) for b in seqs]
            upd = [_tn(v_h[b], jnp.where(keep, kl[b][:, tile], 0.0).astype(BF16)) for b in seqs]
            intra = [jnp.dot(jnp.where(tril, att[b], 0.0).astype(BF16), v_h[b], preferred_element_type=F32)
                     for b in seqs]
            for b in seqs:
                st_new[b] = st_new[b] + upd[b]
                o_n = _rms(intra[b] + inter[b], gon_ref[:, hc])
                o_ref[b, :, hc] = (o_n * jax.nn.silu(co_ref[b, :, hc])).astype(o_ref.dtype)
        for b in seqs:
            sfin_ref[b, j] = st_new[b].T


def _gla_sample(z, zco, zcg, w_gate, b_gate, g_onorm, state, li):
    b, tl, _ = z.shape
    cq, ck, cv = OFF_CQ // KV_WIDTH, OFF_CK // KV_WIDTH, OFF_CV // GLA_WIDTH
    zall = lambda width, col: pl.BlockSpec((b, tl, width), lambda i: (0, 0, col))
    return pl.pallas_call(
        _gla_sample_kernel,
        grid=(1,),
        in_specs=[zall(KV_WIDTH, cq), zall(KV_WIDTH, ck), zall(GLA_WIDTH, cv), zall(GLA_WIDTH, 0), zall(CG_PAD, 0),
                  pl.BlockSpec((None, CG_PAD, KV_WIDTH), lambda i: (li, 0, 0)),
                  pl.BlockSpec((None, 1, KV_WIDTH), lambda i: (li, 0, 0)),
                  pl.BlockSpec((None, 1, GLA_WIDTH), lambda i: (li, 0, 0)),
                  pl.BlockSpec((None, b, 2, LANES, GLA_DV), lambda i: (li, 0, 0, 0, 0))],
        out_specs=[pl.BlockSpec((b, tl, GLA_WIDTH), lambda i: (0, 0, 0)),
                   pl.BlockSpec((b, 2, LANES, GLA_DV), lambda i: (0, 0, 0, 0))],
        out_shape=[jax.ShapeDtypeStruct((b, tl, GLA_WIDTH), BF16),
                   jax.ShapeDtypeStruct((b, 2, LANES, GLA_DV), F32)],
        compiler_params=_cparams(("arbitrary",)),
        name="gla_sample",
    )(z, z, z, zco, zcg, w_gate, b_gate, g_onorm, state)


def _gla_prompt_kernel(cq_ref, ck_ref, cv_ref, co_ref, cg_ref, wg_ref, bg_ref, gon_ref,
                       o_ref, sfin_ref, st_ref):
    tl = cq_ref.shape[0]
    n_chunks = tl // CHUNK

    @pl.when(pl.program_id(1) == 0)
    def _():
        st_ref[...] = jnp.zeros_like(st_ref)

    gate = jnp.dot(cg_ref[...].astype(BF16), wg_ref[...], preferred_element_type=F32) + bg_ref[...]
    la = _log_sigmoid(gate) / GLA_GATE_TEMP
    la_hi = la.astype(BF16)
    la_lo = (la - la_hi.astype(F32)).astype(BF16)
    row = lax.broadcasted_iota(jnp.int32, (GLA_SUPER, GLA_SUPER), 0)
    col = lax.broadcasted_iota(jnp.int32, (GLA_SUPER, GLA_SUPER), 1)
    tri = ((row // CHUNK) == (col // CHUNK)) & (row >= col)
    tri16 = jnp.where(tri, 1.0, 0.0).astype(BF16)
    bcum = jnp.concatenate(
        [jnp.dot(tri16, la_hi[s * GLA_SUPER:(s + 1) * GLA_SUPER], preferred_element_type=F32)
         + jnp.dot(tri16, la_lo[s * GLA_SUPER:(s + 1) * GLA_SUPER], preferred_element_type=F32)
         for s in range(tl // GLA_SUPER)], axis=0)
    blast = [bcum[(ci + 1) * CHUNK - 1:(ci + 1) * CHUNK, :] for ci in range(n_chunks)]
    blast_b = jnp.concatenate([jnp.broadcast_to(bl, (CHUNK, KV_WIDTH)) for bl in blast], axis=0)
    k = ck_ref[...]
    q = cq_ref[...] * (GLA_DK ** -0.5)
    qd = q * jnp.exp(bcum)
    kl = k * jnp.exp(blast_b - bcum)
    ref = jnp.concatenate([jnp.broadcast_to(bcum[r0:r0 + 1], (GLA_SUB, KV_WIDTH))
                           for r0 in range(0, tl, GLA_SUB)], axis=0)
    q_same = q * jnp.exp(bcum - ref)
    k_same = k * jnp.exp(ref - bcum)
    sub_idx = (lax.broadcasted_iota(jnp.int32, (tl, KV_WIDTH), 0) % CHUNK) // GLA_SUB
    q_cross, k_cross = [], []
    for jj in range(CHUNK // GLA_SUB - 1):
        r = jnp.concatenate([jnp.broadcast_to(bcum[ci * CHUNK + (jj + 1) * GLA_SUB:ci * CHUNK + (jj + 1) * GLA_SUB + 1],
                                              (CHUNK, KV_WIDTH)) for ci in range(n_chunks)], axis=0)
        q_cross.append(q * jnp.exp(jnp.where(sub_idx > jj, bcum - r, MASK_VALUE)))
        k_cross.append(k * jnp.exp(jnp.where(sub_idx == jj, r - bcum, MASK_VALUE)))
    same_sub = (row // GLA_SUB) == (col // GLA_SUB)
    lo = lax.broadcasted_iota(jnp.int32, (tl, LANES), 1) < HALF

    keeps = [lo, jnp.logical_not(lo)]
    chunk_rows = [slice(ci * CHUNK, (ci + 1) * CHUNK) for ci in range(n_chunks)]
    super_rows = [slice(s * GLA_SUPER, (s + 1) * GLA_SUPER) for s in range(tl // GLA_SUPER)]
    stack = lambda pair, rows: jnp.concatenate([pair[0][rows], pair[1][rows]], axis=0)

    pairs = []
    for j in range(2):
        tile = slice(j * LANES, (j + 1) * LANES)
        hcs = [slice((2 * j + hh) * GLA_DV, (2 * j + hh + 1) * GLA_DV) for hh in range(2)]
        v = [cv_ref[:, hc].astype(BF16) for hc in hcs]
        qm = [jnp.where(kp, qd[:, tile], 0.0).astype(BF16) for kp in keeps]
        klm = [jnp.where(kp, kl[:, tile], 0.0).astype(BF16) for kp in keeps]
        q_same16 = q_same[:, tile].astype(BF16)
        k_same16 = [jnp.where(kp, k_same[:, tile], 0.0).astype(BF16) for kp in keeps]
        q_cross16 = jnp.concatenate([qc[:, tile] for qc in q_cross], axis=1).astype(BF16)
        k_cross16 = [jnp.concatenate([jnp.where(kp, kc[:, tile], 0.0) for kc in k_cross], axis=1).astype(BF16)
                     for kp in keeps]
        upd = [_tn(stack(v, rows), stack(klm, rows)) for rows in chunk_rows]
        att = [(_nt(q_same16[rows], stack(k_same16, rows)), _nt(q_cross16[rows], stack(k_cross16, rows)))
               for rows in super_rows]
        pairs.append(dict(tile=tile, hcs=hcs, v=v, qm=qm, upd=upd, att=att))
    for j, p in enumerate(pairs):
        st = st_ref[j]
        st16 = []
        for ci in range(n_chunks):
            st16.append(st.astype(BF16))
            st = st * jnp.exp(blast[ci][:, p["tile"]]) + p["upd"][ci]
        st_ref[j] = st
        p["inter"] = [_nt(stack(p["qm"], rows), st16[ci]) for ci, rows in enumerate(chunk_rows)]
    for p in pairs:
        intra = [[], []]
        for rows, (att_same, att_cross) in zip(super_rows, p["att"]):
            for hh in range(2):
                cols = slice(hh * GLA_SUPER, (hh + 1) * GLA_SUPER)
                a_h = jnp.where(tri, jnp.where(same_sub, att_same[:, cols], att_cross[:, cols]), 0.0).astype(BF16)
                intra[hh].append(jnp.dot(a_h, p["v"][hh][rows], preferred_element_type=F32))
        for hh in range(2):
            hc = p["hcs"][hh]
            o_h = jnp.concatenate(intra[hh], axis=0) + jnp.concatenate(
                [blk[hh * CHUNK:(hh + 1) * CHUNK] for blk in p["inter"]], axis=0)
            o_n = _rms(o_h, gon_ref[:, hc])
            o_ref[:, hc] = (o_n * jax.nn.silu(co_ref[:, hc])).astype(o_ref.dtype)

    @pl.when(pl.program_id(1) == pl.num_programs(1) - 1)
    def _():
        for j in range(2):
            sfin_ref[j] = st_ref[j].T


def _gla_prompt(z, zco, zcg, w_gate, b_gate, g_onorm, li, tl):
    b, l, _ = z.shape
    return pl.pallas_call(
        _gla_prompt_kernel,
        grid=(b, l // tl),
        in_specs=_gla_specs(tl, li),
        out_specs=[pl.BlockSpec((None, tl, GLA_WIDTH), lambda bi, i: (bi, i, 0)),
                   pl.BlockSpec((None, 2, LANES, GLA_DV), lambda bi, i: (bi, 0, 0, 0))],
        out_shape=[jax.ShapeDtypeStruct((b, l, GLA_WIDTH), BF16),
                   jax.ShapeDtypeStruct((b, 2, LANES, GLA_DV), F32)],
        scratch_shapes=[pltpu.VMEM((2, GLA_DV, LANES), F32)],
        compiler_params=_cparams(("parallel", "arbitrary")),
        name="gla_prompt",
    )(z, z, z, zco, zcg, w_gate, b_gate, g_onorm)


def _pipelined(nsub, matmul_stage, vector_stage):
    nxt = matmul_stage(0)
    for r in range(nsub):
        cur = nxt
        if r + 1 < nsub:
            nxt = matmul_stage(r + 1)
        vector_stage(r, cur)


def _outproj_kernel(x_ref, oa_ref, ob_ref, oc_ref, w_ref, g_ref, g2_ref, x1_ref, h2_ref, *, nsub):
    sub = x_ref.shape[0] // nsub

    def matmul_stage(r):
        rows = slice(r * sub, (r + 1) * sub)
        mix = jnp.dot(oa_ref[rows, :], w_ref[0:SWA_WIDTH, :], preferred_element_type=F32)
        mix = mix + jnp.dot(ob_ref[rows, :], w_ref[SWA_WIDTH:SWA_WIDTH + GMLP_WIDTH, :], preferred_element_type=F32)
        return mix + jnp.dot(oc_ref[rows, :], w_ref[SWA_WIDTH + GMLP_WIDTH:, :], preferred_element_type=F32)

    def vector_stage(r, mix):
        rows = slice(r * sub, (r + 1) * sub)
        x1 = x_ref[rows, :] + _rms(mix, g_ref[...])
        x1_ref[rows, :] = x1
        h2_ref[rows, :] = _rms(x1, g2_ref[...]).astype(BF16)

    _pipelined(nsub, matmul_stage, vector_stage)


def _outproj(x, oa, ob, oc, w, g, g2, li, tm, nsub):
    m = x.shape[0]
    row = lambda width: pl.BlockSpec((tm, width), lambda i: (i, 0))
    gain = pl.BlockSpec((None, 1, D_MODEL), lambda i: (li, 0, 0))
    return pl.pallas_call(
        functools.partial(_outproj_kernel, nsub=nsub),
        grid=(m // tm,),
        in_specs=[row(D_MODEL), row(SWA_WIDTH), row(GMLP_WIDTH), row(GLA_WIDTH),
                  pl.BlockSpec((None, D_MODEL, D_MODEL), lambda i: (li, 0, 0)), gain, gain],
        out_specs=[row(D_MODEL), row(D_MODEL)],
        out_shape=[jax.ShapeDtypeStruct((m, D_MODEL), F32), jax.ShapeDtypeStruct((m, D_MODEL), BF16)],
        compiler_params=_cparams(("parallel",)),
        name="outproj",
    )(x, oa, ob, oc, w, g, g2)


def _ffn_zero_first(f_ref):
    @pl.when(pl.program_id(1) == 0)
    def _():
        f_ref[...] = jnp.zeros_like(f_ref)


def _ffn_kernel(h_ref, wup_ref, wdn_ref, f_ref):
    _ffn_zero_first(f_ref)
    u = jnp.maximum(jnp.dot(h_ref[...], wup_ref[...], preferred_element_type=F32), 0.0)
    f_ref[...] += jnp.dot((u * u).astype(BF16), wdn_ref[...], preferred_element_type=F32)


def _ffn_rest_kernel(h_ref, wup_ref, wdn_ref, f_first_ref, f_ref):
    _ffn_kernel(h_ref, wup_ref, wdn_ref, f_ref)


def _ffn_first_kernel(h_ref, wup_ref, wdn_ref, f_ref, fs_ref, wup16_ref, wdn16_ref):
    @pl.when(pl.program_id(1) == 0)
    def _():
        f_ref[...] = jnp.zeros_like(f_ref)
        fs_ref[...] = jnp.zeros_like(fs_ref)

    wup16_ref[...] = wup_ref[...].astype(BF16)
    wdn16_ref[...] = wdn_ref[...].astype(BF16)
    u = jnp.maximum(jnp.dot(h_ref[...], wup16_ref[...], preferred_element_type=F32), 0.0)
    d = jnp.dot((u * u).astype(BF16), wdn16_ref[...], preferred_element_type=F32)
    tm = f_ref.shape[0]
    f_ref[...] += d[:tm]
    fs_ref[...] += d[tm:]


def _ffn_specs(tm, tk, row_of):
    return [pl.BlockSpec((tm, D_MODEL), lambda i, k: (row_of(i), 0)),
            pl.BlockSpec((None, D_MODEL, tk), lambda i, k: (0, 0, k)),
            pl.BlockSpec((None, tk, D_MODEL), lambda i, k: (0, k, 0))]


def _ffn(h, hs, w_up, w_down, li, tm, tk_first, tk):
    m, ms = h.shape[0], hs.shape[0]
    f_shape = jax.ShapeDtypeStruct((m, D_MODEL), F32)
    h_first = jnp.concatenate([h[:tm], hs], axis=0)
    f_spec, wup_spec, wdn_spec = _ffn_specs(tm, tk_first, lambda i: 0)
    f, fs, wup16, wdn16 = pl.pallas_call(
        _ffn_first_kernel,
        grid=(1, D_FF // tk_first),
        in_specs=[pl.BlockSpec((tm + ms, D_MODEL), lambda i, k: (0, 0)),
                  pl.BlockSpec((None, D_MODEL, tk_first), lambda i, k: (li, 0, k)),
                  pl.BlockSpec((None, tk_first, D_MODEL), lambda i, k: (li, k, 0))],
        out_specs=[f_spec, pl.BlockSpec((ms, D_MODEL), lambda i, k: (0, 0)), wup_spec, wdn_spec],
        out_shape=[f_shape, jax.ShapeDtypeStruct((ms, D_MODEL), F32),
                   jax.ShapeDtypeStruct((1, D_MODEL, D_FF), BF16), jax.ShapeDtypeStruct((1, D_FF, D_MODEL), BF16)],
        compiler_params=_cparams(("arbitrary", "arbitrary"), FFN_VMEM_LIMIT),
        name="ffn_first",
    )(h_first, w_up, w_down)
    f = pl.pallas_call(
        _ffn_rest_kernel,
        grid=(m // tm - 1, D_FF // tk),
        in_specs=_ffn_specs(tm, tk, lambda i: i + 1) + [pl.BlockSpec(memory_space=pl.ANY)],
        out_specs=pl.BlockSpec((tm, D_MODEL), lambda i, k: (i + 1, 0)),
        out_shape=f_shape,
        input_output_aliases={3: 0},
        compiler_params=_cparams(("parallel", "arbitrary"), FFN_VMEM_LIMIT),
        name="ffn_rest",
    )(h, wup16, wdn16, f)
    return f, fs


def _ple_kernel(x1_ref, f_ref, pe_ref, gpost_ref, wp_ref, wg_ref, gnext_ref, x3_ref, *maybe_hn_ref, nsub):
    sub = x1_ref.shape[0] // nsub

    def matmul_stage(r):
        rows = slice(r * sub, (r + 1) * sub)
        proj = jnp.dot(pe_ref[rows, :].astype(BF16), wp_ref[...], preferred_element_type=F32)
        x2 = x1_ref[rows, :] + _rms(f_ref[rows, :], gpost_ref[...])
        gate = jnp.dot(x2.astype(BF16), wg_ref[...], preferred_element_type=F32)
        return x2, gate, proj

    def vector_stage(r, staged):
        rows = slice(r * sub, (r + 1) * sub)
        x2, gate, proj = staged
        x3 = x2 + proj * jax.nn.sigmoid(gate)
        x3_ref[rows, :] = x3
        if maybe_hn_ref:
            maybe_hn_ref[0][rows, :] = _rms(x3, gnext_ref[...]).astype(BF16)

    _pipelined(nsub, matmul_stage, vector_stage)


def _ple(x1, f, pe, gpost, wp, wg, gnext, li, tm, nsub, want_h):
    m = x1.shape[0]
    row = lambda: pl.BlockSpec((tm, D_MODEL), lambda i: (i, 0))
    ln = min(li + 1, DEPTH - 1)
    out_specs = [row()]
    out_shape = [jax.ShapeDtypeStruct((m, D_MODEL), F32)]
    if want_h:
        out_specs.append(row())
        out_shape.append(jax.ShapeDtypeStruct((m, D_MODEL), BF16))
    return pl.pallas_call(
        functools.partial(_ple_kernel, nsub=nsub),
        grid=(m // tm,),
        in_specs=[row(), row(),
                  pl.BlockSpec((None, tm, PLE_DIM), lambda i: (li, i, 0)),
                  pl.BlockSpec((None, 1, D_MODEL), lambda i: (li, 0, 0)),
                  pl.BlockSpec((None, PLE_DIM, D_MODEL), lambda i: (li, 0, 0)),
                  pl.BlockSpec((None, D_MODEL, D_MODEL), lambda i: (li, 0, 0)),
                  pl.BlockSpec((None, 1, D_MODEL), lambda i: (ln, 0, 0))],
        out_specs=out_specs,
        out_shape=out_shape,
        compiler_params=_cparams(("parallel",)),
        name="ple",
    )(x1, f, pe, gpost, wp, wg, gnext)


def _t5_bucket(rel):
    half = REL_BUCKETS // 2
    max_exact = half // 2
    n = -rel
    ret = jnp.where(n < 0, half, 0)
    n = jnp.abs(n)
    nf = jnp.maximum(n, 1).astype(jnp.float32)
    large = max_exact + (jnp.log(nf / max_exact) / math.log(REL_MAX_DIST / max_exact)
                         * (half - max_exact)).astype(jnp.int32)
    large = jnp.minimum(large, half - 1)
    return ret + jnp.where(n < max_exact, n, large)


def _bias_kernel(tab_ref, bkt_p_ref, valid_p_ref, bkt_s_ref, out_p_ref, out_s_ref):
    bkt_p = bkt_p_ref[...]
    bkt_s = bkt_s_ref[...]
    for h in range(SWA_Q_HEADS):
        def pick(b, accs):
            val = tab_ref[b, h]
            return (jnp.where(bkt_p == b, val, accs[0]), jnp.where(bkt_s == b, val, accs[1]))

        acc_p, acc_s = lax.fori_loop(0, REL_BUCKETS, pick,
                                     (jnp.zeros(bkt_p.shape, F32), jnp.zeros(bkt_s.shape, F32)))
        kv, hq = divmod(h, SWA_GROUP)
        for var in range(2):
            out_p_ref[var, kv, :, hq * LANES:(hq + 1) * LANES] = jnp.where(valid_p_ref[var] != 0, acc_p * LOG2E, MASK_VALUE)
        out_s_ref[h] = acc_s


def _bias_tables(rel_bias, dec_seq):
    qpos = jnp.arange(WINDOW)[None, :]
    kpos = jnp.arange(-WINDOW, WINDOW)[:, None]
    bkt_p = _t5_bucket(kpos - qpos).astype(jnp.int32)
    kchunk = jnp.floor_divide(kpos, CHUNK)
    qchunk = qpos // CHUNK
    valid = (kchunk >= qchunk - WINDOW // CHUNK) & (kchunk <= qchunk)
    valid_p = jnp.stack([valid & (kpos >= 0), valid]).astype(jnp.int32)
    bkt_s = _t5_bucket(jnp.arange(-WINDOW, dec_seq)[None, :] - jnp.arange(dec_seq)[:, None]).astype(jnp.int32)
    full = lambda a: pl.BlockSpec(a.shape, lambda: (0,) * a.ndim)
    out_shape = [jax.ShapeDtypeStruct((2, SWA_KV_HEADS, 2 * WINDOW, SWA_GROUP * LANES), F32),
                 jax.ShapeDtypeStruct((SWA_Q_HEADS, dec_seq, WINDOW + dec_seq), F32)]
    return pl.pallas_call(
        _bias_kernel,
        in_specs=[pl.BlockSpec(memory_space=pltpu.SMEM), full(bkt_p), full(valid_p), full(bkt_s)],
        out_specs=[full(s) for s in out_shape],
        out_shape=out_shape,
        name="rel_bias",
    )(rel_bias, bkt_p, valid_p, bkt_s)


def _prep_w_in(w):
    wt = jnp.swapaxes(w, 1, 2).astype(BF16)
    co = Z_WIDTH + GLA_GATE_RANK
    pad = jnp.zeros((w.shape[0], CG_PAD - GLA_GATE_RANK, w.shape[1]), BF16)
    return wt, wt[:, co:], jnp.concatenate([wt[:, Z_WIDTH:co], pad], axis=1)


def kernel(x_prompt, x_sample, cache_swa_k, cache_swa_v, state_gla, p_prompt, p_sample, w_in, w_gate, b_gate, rel_bias, attn_sinks, w_spatial, b_spatial, g_gmlp_vnorm, g_gla_onorm, w_out, g_mix_pre, g_mix_post, g_ffn_pre, g_ffn_post, w_up, w_down, w_ple, w_ple_gate):
    bp, lp, _ = x_prompt.shape
    bs, ls, _ = x_sample.shape
    mp, ms = bp * lp, bs * ls
    row3 = lambda a: a.reshape(DEPTH, 1, -1)

    bias_p, bias_s = _bias_tables(rel_bias, ls)
    sink_t = jnp.repeat(attn_sinks * LOG2E, LANES, axis=-1).reshape(DEPTH, SWA_KV_HEADS, 1, SWA_GROUP * LANES)
    sinks_flat = attn_sinks.reshape(-1)

    w_in16, w_co16, w_cg16 = _prep_w_in(w_in)
    w_out16, w_ple16, w_plg16 = w_out.astype(BF16), w_ple.astype(BF16), w_ple_gate.astype(BF16)
    wg16 = jnp.concatenate([w_gate, jnp.zeros((DEPTH, CG_PAD - GLA_GATE_RANK, KV_WIDTH), F32)], axis=1).astype(BF16)
    g_pre, g_post, g_fpre, g_fpost = row3(g_mix_pre), row3(g_mix_post), row3(g_ffn_pre), row3(g_ffn_post)
    bg, gon, gvn = row3(b_gate), row3(g_gla_onorm), row3(g_gmlp_vnorm)
    b_st = jnp.swapaxes(b_spatial, 1, 2)
    w_s_small, b_st_small = w_spatial[:, :, :ls, :ls], b_st[:, :ls, :]
    pe_p = p_prompt.reshape(DEPTH, mp, PLE_DIM)
    pe_s = p_sample.reshape(DEPTH, ms, PLE_DIM)
    cache_k = cache_swa_k.reshape(DEPTH, bs, WINDOW, KV_WIDTH)
    cache_v = cache_swa_v.reshape(DEPTH, bs, WINDOW, KV_WIDTH)
    state = state_gla.reshape(DEPTH, bs, 2, LANES, GLA_DV)

    def mix_out(x, oa, ob, oc, li, ts, nsub):
        m = x.shape[0]
        return _outproj(x, oa.reshape(m, -1), ob.reshape(m, -1), oc.reshape(m, -1), w_out16, g_post, g_fpre,
                        li, ts, nsub)

    def layer_out(x1, f, pe, li, ts, nsub):
        out = _ple(x1, f, pe, g_fpost, w_ple16, w_plg16, g_pre, li, ts, nsub, li + 1 < DEPTH)
        return out if li + 1 < DEPTH else (out[0], None)

    xp = x_prompt.reshape(mp, D_MODEL)
    xs = x_sample.reshape(ms, D_MODEL)
    hp = _prenorm(xp, g_pre, 0, FFN_ROWS)
    hs = _prenorm(xs, g_pre, 0, ms)
    kp_rows, vp_rows, ks_rows, vs_rows, sp_states, ss_states, gv_rows = [], [], [], [], [], [], []
    for li in range(DEPTH):
        zp, zco, zcg = _inproj(hp, w_in16, w_co16, w_cg16, li, FFN_ROWS, INPROJ_COLS)
        zp, zco, zcg = zp.reshape(bp, lp, -1), zco.reshape(bp, lp, -1), zcg.reshape(bp, lp, -1)
        oa = _attn_prompt(zp, bias_p, sink_t[li], ATTN_ROWS)
        ob, = _gmlp(zp, w_spatial, b_st, gvn, li, GMLP_CHUNK, GMLP_ROWS, False)
        oc, s_p = _gla_prompt(zp, zco, zcg, wg16, bg, gon, li, GLA_ROWS)
        x1p, h2p = mix_out(xp, oa, ob, oc, li, DENSE_ROWS, OUTPROJ_SUB)
        kp_rows.append(zp[:, lp - WINDOW:, OFF_K:OFF_K + KV_WIDTH].reshape(bp, WINDOW, SWA_KV_HEADS, HEAD_DIM))
        vp_rows.append(zp[:, lp - WINDOW:, OFF_V:OFF_V + KV_WIDTH].reshape(bp, WINDOW, SWA_KV_HEADS, HEAD_DIM))
        sp_states.append(s_p.reshape(bp, GLA_HEADS, GLA_DK, GLA_DV))

        zs, zco, zcg = _inproj(hs, w_in16, w_co16, w_cg16, li, ms, INPROJ_COLS)
        zs, zco, zcg = zs.reshape(bs, ls, -1), zco.reshape(bs, ls, -1), zcg.reshape(bs, ls, -1)
        oa = _attn_sample(zs, cache_k, cache_v, bias_s, sinks_flat[li * SWA_Q_HEADS:(li + 1) * SWA_Q_HEADS], li)
        ob, vn_s = _gmlp(zs.reshape(1, ms, -1), w_s_small, b_st_small, gvn, li, ls, ms, True)
        oc, s_s = _gla_sample(zs, zco, zcg, wg16, bg, gon, state, li)
        x1s, h2s = mix_out(xs, oa, ob, oc, li, ms, 1)

        fp, fs = _ffn(h2p, h2s, w_up, w_down, li, FFN_ROWS, FFN_FIRST_CHUNK, FFN_CHUNK)
        xp, hp = layer_out(x1p, fp, pe_p, li, DENSE_ROWS, PLE_SUB)
        xs, hs = layer_out(x1s, fs, pe_s, li, ms, 1)
        ks_rows.append(zs[:, :, OFF_K:OFF_K + KV_WIDTH].reshape(bs, ls, SWA_KV_HEADS, HEAD_DIM))
        vs_rows.append(zs[:, :, OFF_V:OFF_V + KV_WIDTH].reshape(bs, ls, SWA_KV_HEADS, HEAD_DIM))
        ss_states.append(s_s.reshape(bs, GLA_HEADS, GLA_DK, GLA_DV).astype(state_gla.dtype))
        gv_rows.append(vn_s.reshape(bs, ls, GMLP_WIDTH))

    return (xp.reshape(bp, lp, D_MODEL), xs.reshape(bs, ls, D_MODEL),
            jnp.stack(kp_rows), jnp.stack(vp_rows), jnp.stack(ks_rows), jnp.stack(vs_rows),
            jnp.stack(sp_states), jnp.stack(ss_states), jnp.stack(gv_rows))
```
